```python
import jax, jax.numpy as jnp
from jax import lax
import numpy as np

D_MODEL = 1024
BATCH = 8
SEQ = 2048
DEPTH = 2
DEC_BATCH = 128
DEC_SEQ = 8
PAST_LEN = 16384
PAGE_SIZE = 128

N_PAIRS = DEPTH // 2
GLA_HEADS = 4
GLA_DK = D_MODEL // 16
GLA_DV = D_MODEL // 8
GLA_RANK = 16
GLA_TAU = 16.0
RET_HEADS = 4
RET_DK = D_MODEL // 16
RET_DV = D_MODEL // 8
AB_CHUNK = 64
ROPE_BASE = 10000.0
GDN_HEADS = 8
GDN_DK = D_MODEL // 8
GDN_DV = D_MODEL // 8
CONV_W = 4
CONV_CH = GDN_HEADS * (2 * GDN_DK + GDN_DV)
GDN_CHUNK = 64
EPS = 1e-6

kernel_name = "hybrid_gla_retnet_gdn_adaln_step"

F32 = jnp.float32


def _offsets(sizes):
    out, acc = [], 0
    for s in sizes[:-1]:
        acc += s
        out.append(acc)
    return out


def _chunk_len(T, chunk):
    return chunk if T % chunk == 0 else T


def rms_norm(x, g):
    xf = x.astype(F32)
    y = xf * lax.rsqrt(jnp.mean(xf * xf, axis=-1, keepdims=True) + EPS)
    return y * g.astype(F32)


def group_norm(x, g):
    xf = x.astype(F32)
    mu = jnp.mean(xf, axis=-1, keepdims=True)
    d = xf - mu
    return d * lax.rsqrt(jnp.mean(d * d, axis=-1, keepdims=True) + EPS) * g.astype(F32)


def l2norm(x):
    return x * lax.rsqrt(jnp.sum(x * x, axis=-1, keepdims=True) + EPS)


def rotary(x, pos):
    half = x.shape[-1] // 2
    inv = ROPE_BASE ** (-jnp.arange(half, dtype=F32) / half)
    ang = pos.astype(F32)[:, None] * inv[None, :]
    cos = jnp.cos(ang)[None, :, None, :]
    sin = jnp.sin(ang)[None, :, None, :]
    x1, x2 = x[..., :half], x[..., half:]
    return jnp.concatenate([x1 * cos - x2 * sin, x2 * cos + x1 * sin], axis=-1)


def chunked_gated_linear_attn(q, k, v, log_g, s0):
    B, T, H, K = q.shape
    V = v.shape[-1]
    L = _chunk_len(T, AB_CHUNK)
    n = T // L

    def blk(a):
        return a.reshape(B, n, L, H, a.shape[-1]).transpose(1, 0, 3, 2, 4)

    qc, kc, vc, gc = blk(q), blk(k), blk(v), blk(log_g)
    b = jnp.cumsum(gc, axis=3)
    b_last = b[:, :, :, -1:, :]
    q_dec = qc * jnp.exp(b)
    k_inv = kc * jnp.exp(-b)
    k_out = kc * jnp.exp(b_last - b)
    causal = jnp.tril(jnp.ones((L, L), dtype=bool))
    scores = jnp.where(causal, jnp.einsum('nbhik,nbhjk->nbhij', q_dec, k_inv), 0.0)
    o_intra = jnp.einsum('nbhij,nbhjv->nbhiv', scores, vc)

    def step(S, xs):
        q_d, k_o, v_c, bl = xs
        o_inter = jnp.einsum('bhik,bhkv->bhiv', q_d, S)
        S = S * jnp.exp(bl)[:, :, 0, :, None] + jnp.einsum('bhjk,bhjv->bhkv', k_o, v_c)
        return S, o_inter

    s_final, o_inter = lax.scan(step, s0, (q_dec, k_out, vc, b_last))
    o = (o_intra + o_inter).transpose(1, 0, 3, 2, 4).reshape(B, T, H, V)
    return o, s_final


def chunked_gated_delta(q, k, v, beta, log_g, s0):
    B, T, H, K = q.shape
    V = v.shape[-1]
    L = _chunk_len(T, GDN_CHUNK)
    n = T // L

    def blk(a):
        return a.reshape(B, n, L, H, a.shape[-1]).transpose(1, 0, 3, 2, 4)

    def blk_s(a):
        return a.reshape(B, n, L, H).transpose(1, 0, 3, 2)

    qc, kc, vc = blk(q), blk(k), blk(v)
    bc, gc = blk_s(beta), blk_s(log_g)
    b = jnp.cumsum(gc, axis=-1)
    b_last = b[..., -1:]
    lower = jnp.tril(jnp.ones((L, L), dtype=bool))
    strict = jnp.tril(jnp.ones((L, L), dtype=bool), -1)
    diff = b[..., :, None] - b[..., None, :]
    seg = jnp.where(lower, jnp.exp(jnp.where(lower, diff, 0.0)), 0.0)
    k_beta = kc * bc[..., None]
    v_beta = vc * bc[..., None]
    kk = jnp.einsum('nbhik,nbhjk->nbhij', k_beta, kc) * seg
    a_mat = jnp.eye(L, dtype=F32) + jnp.where(strict, kk, 0.0)
    rhs = jnp.concatenate([v_beta, k_beta * jnp.exp(b)[..., None]], axis=-1)
    sol = lax.linalg.triangular_solve(a_mat, rhs, left_side=True, lower=True)
    u, w = sol[..., :V], sol[..., V:]
    q_dec = qc * jnp.exp(b)[..., None]
    attn = jnp.einsum('nbhik,nbhjk->nbhij', qc, kc) * seg
    k_out = kc * jnp.exp(b_last - b)[..., None]

    def step(S, xs):
        u_c, w_c, q_d, a_c, k_o, bl = xs
        v_new = u_c - jnp.einsum('bhik,bhkv->bhiv', w_c, S)
        o = jnp.einsum('bhik,bhkv->bhiv', q_d, S) + jnp.einsum('bhij,bhjv->bhiv', a_c, v_new)
        S = S * jnp.exp(bl)[..., None] + jnp.einsum('bhjk,bhjv->bhkv', k_o, v_new)
        return S, o

    s_final, o = lax.scan(step, s0, (u, w, q_dec, attn, k_out, b_last))
    o = o.transpose(1, 0, 3, 2, 4).reshape(B, T, H, V)
    return o, s_final


def mixer_ab(h, pos, s_gla, s_ret, w_in, w_gate_up, b_gate, g_gla, g_ret, w_out):
    B, T, _ = h.shape
    sizes = (GLA_HEADS * GLA_DK, GLA_HEADS * GLA_DK, GLA_HEADS * GLA_DV, GLA_HEADS * GLA_DV, GLA_RANK,
             RET_HEADS * RET_DK, RET_HEADS * RET_DK, RET_HEADS * RET_DV, RET_HEADS * RET_DV)
    gq, gk, gv, gz, glr, rq, rk, rv, rz = jnp.split(h @ w_in, _offsets(sizes), axis=-1)

    def heads(a, H):
        return a.astype(F32).reshape(B, T, H, -1)

    log_a = jax.nn.log_sigmoid((glr @ w_gate_up + b_gate).astype(F32)) / GLA_TAU
    log_a = log_a.reshape(B, T, GLA_HEADS, GLA_DK)
    q_g = heads(gq, GLA_HEADS) * (GLA_DK ** -0.5)
    k_g = heads(gk, GLA_HEADS)
    v_g = heads(gv, GLA_HEADS)
    q_r = rotary(heads(rq, RET_HEADS), pos) * (RET_DK ** -0.5)
    k_r = rotary(heads(rk, RET_HEADS), pos)
    v_r = heads(rv, RET_HEADS)
    log_gamma = jnp.log1p(-jnp.exp2(-5.0 - jnp.arange(RET_HEADS, dtype=F32)))
    log_r = jnp.broadcast_to(log_gamma[None, None, :, None], (B, T, RET_HEADS, RET_DK))

    q = jnp.concatenate([q_g, q_r], axis=2)
    k = jnp.concatenate([k_g, k_r], axis=2)
    v = jnp.concatenate([v_g, v_r], axis=2)
    lg = jnp.concatenate([log_a, log_r], axis=2)
    s0 = jnp.concatenate([s_gla.astype(F32), s_ret.astype(F32)], axis=1)
    o, s = chunked_gated_linear_attn(q, k, v, lg, s0)

    o_g = rms_norm(o[:, :, :GLA_HEADS], g_gla)
    o_r = group_norm(o[:, :, GLA_HEADS:], g_ret)
    o = jnp.concatenate([o_g, o_r], axis=2).reshape(B, T, -1)
    z = jnp.concatenate([gz, rz], axis=-1).astype(F32)
    y = (o * jax.nn.silu(z)).astype(h.dtype) @ w_out
    return y, s[:, :GLA_HEADS], s[:, GLA_HEADS:]


def mixer_c(h, s_delta, s_conv, w_in, conv_w, a_log, dt_bias, g_norm, w_out):
    B, T, _ = h.shape
    sizes = (CONV_CH, GDN_HEADS * GDN_DV, GDN_HEADS, GDN_HEADS)
    qkv, z, b_raw, a_raw = jnp.split(h @ w_in, _offsets(sizes), axis=-1)
    buf = jnp.concatenate([s_conv.astype(qkv.dtype), qkv], axis=1)
    conv = buf[:, 0:T] * conv_w[0]
    for i in range(1, CONV_W):
        conv = conv + buf[:, i:i + T] * conv_w[i]
    new_conv = buf[:, buf.shape[1] - (CONV_W - 1):]
    qkv = jax.nn.silu(conv.astype(F32))
    q, k, v = jnp.split(qkv, _offsets((GDN_HEADS * GDN_DK, GDN_HEADS * GDN_DK, GDN_HEADS * GDN_DV)), axis=-1)
    q = l2norm(q.reshape(B, T, GDN_HEADS, GDN_DK)) * (GDN_DK ** -0.5)
    k = l2norm(k.reshape(B, T, GDN_HEADS, GDN_DK))
    v = v.reshape(B, T, GDN_HEADS, GDN_DV)
    beta = jax.nn.sigmoid(b_raw.astype(F32))
    log_g = -jnp.exp(a_log.astype(F32)) * jax.nn.softplus(a_raw.astype(F32) + dt_bias.astype(F32))
    o, s = chunked_gated_delta(q, k, v, beta, log_g, s_delta.astype(F32))
    o = rms_norm(o, g_norm).reshape(B, T, -1)
    y = (o * jax.nn.silu(z.astype(F32))).astype(h.dtype) @ w_out
    return y, s, new_conv


def trunk(x, c, pos, st_gla, st_ret, st_delta, st_conv,
          w_ada, b_ada, g_pre, g_post, w_in_ab, w_gla_gate_up, b_gla_gate, g_norm_gla, g_norm_ret,
          w_out_ab, w_in_c, conv_c, a_log_c, dt_bias_c, g_norm_c, w_out_c):
    new_gla, new_ret, new_delta, new_conv = [], [], [], []
    for l in range(DEPTH):
        p = l // 2
        mod = jax.nn.silu(c) @ w_ada[l] + b_ada[l]
        shift, scale, gate = jnp.split(mod[:, None, :].astype(F32), 3, axis=-1)
        h = (rms_norm(x, g_pre[l]) * (1.0 + scale) + shift).astype(x.dtype)
        if l % 2 == 0:
            y, s1, s2 = mixer_ab(h, pos, st_gla[p], st_ret[p], w_in_ab[p], w_gla_gate_up[p], b_gla_gate[p],
                                 g_norm_gla[p], g_norm_ret[p], w_out_ab[p])
            new_gla.append(s1.astype(x.dtype))
            new_ret.append(s2.astype(x.dtype))
        else:
            y, s1, s2 = mixer_c(h, st_delta[p], st_conv[p], w_in_c[p], conv_c[p], a_log_c[p], dt_bias_c[p],
                                g_norm_c[p], w_out_c[p])
            new_delta.append(s1.astype(x.dtype))
            new_conv.append(s2.astype(x.dtype))
        x = (x.astype(F32) + gate * rms_norm(y, g_post[l])).astype(x.dtype)
    return x, jnp.stack(new_gla), jnp.stack(new_ret), jnp.stack(new_delta), jnp.stack(new_conv)


def setup_inputs(seed: int = 0) -> dict:
    key = jax.random.key(seed)
    ks = jax.random.split(key, 24)
    nrm = jax.random.normal
    n_ab = 2 * GLA_HEADS * GLA_DK + 2 * GLA_HEADS * GLA_DV + GLA_RANK + 2 * RET_HEADS * RET_DK + 2 * RET_HEADS * RET_DV
    w_ab = GLA_HEADS * GLA_DV + RET_HEADS * RET_DV
    n_c = CONV_CH + GDN_HEADS * GDN_DV + 2 * GDN_HEADS
    w_c = GDN_HEADS * GDN_DV
    dt = jnp.exp(jax.random.uniform(ks[20], (N_PAIRS, GDN_HEADS), minval=np.log(1e-3), maxval=np.log(1e-1)))
    return {
        "x_prompt": nrm(ks[0], (BATCH, SEQ, D_MODEL), F32),
        "x_sample": nrm(ks[1], (DEC_BATCH, DEC_SEQ, D_MODEL), F32),
        "state_gla": 0.1 * nrm(ks[2], (N_PAIRS, DEC_BATCH, GLA_HEADS, GLA_DK, GLA_DV), F32),
        "state_ret": 0.1 * nrm(ks[3], (N_PAIRS, DEC_BATCH, RET_HEADS, RET_DK, RET_DV), F32),
        "state_delta": 0.1 * nrm(ks[4], (N_PAIRS, DEC_BATCH, GDN_HEADS, GDN_DK, GDN_DV), F32),
        "state_conv": nrm(ks[5], (N_PAIRS, DEC_BATCH, CONV_W - 1, CONV_CH), F32),
        "c_prompt": nrm(ks[6], (BATCH, D_MODEL), F32),
        "c_sample": nrm(ks[7], (DEC_BATCH, D_MODEL), F32),
        "w_ada": nrm(ks[8], (DEPTH, D_MODEL, 3 * D_MODEL), F32) * D_MODEL ** -0.5,
        "b_ada": 0.02 * nrm(ks[9], (DEPTH, 3 * D_MODEL), F32),
        "g_pre": 1.0 + 0.02 * nrm(ks[10], (DEPTH, D_MODEL), F32),
        "g_post": 1.0 + 0.02 * nrm(ks[11], (DEPTH, D_MODEL), F32),
        "w_in_ab": nrm(ks[12], (N_PAIRS, D_MODEL, n_ab), F32) * D_MODEL ** -0.5,
        "w_gla_gate_up": nrm(ks[13], (N_PAIRS, GLA_RANK, GLA_HEADS * GLA_DK), F32) * GLA_RANK ** -0.5,
        "b_gla_gate": 0.1 * nrm(ks[14], (N_PAIRS, GLA_HEADS * GLA_DK), F32),
        "g_norm_gla": 1.0 + 0.02 * nrm(ks[15], (N_PAIRS, GLA_DV), F32),
        "g_norm_ret": 1.0 + 0.02 * nrm(ks[16], (N_PAIRS, RET_DV), F32),
        "w_out_ab": nrm(ks[17], (N_PAIRS, w_ab, D_MODEL), F32) * w_ab ** -0.5,
        "w_in_c": nrm(ks[18], (N_PAIRS, D_MODEL, n_c), F32) * D_MODEL ** -0.5,
        "conv_c": nrm(ks[19], (N_PAIRS, CONV_W, CONV_CH), F32) * CONV_W ** -0.5,
        "a_log_c": jnp.log(jax.random.uniform(ks[21], (N_PAIRS, GDN_HEADS), minval=1.0, maxval=16.0)),
        "dt_bias_c": dt + jnp.log(-jnp.expm1(-dt)),
        "g_norm_c": 1.0 + 0.02 * nrm(ks[22], (N_PAIRS, GDN_DV), F32),
        "w_out_c": nrm(ks[23], (N_PAIRS, w_c, D_MODEL), F32) * w_c ** -0.5,
    }


def reference(x_prompt, x_sample, state_gla, state_ret, state_delta, state_conv, c_prompt, c_sample,
              w_ada, b_ada, g_pre, g_post, w_in_ab, w_gla_gate_up, b_gla_gate, g_norm_gla, g_norm_ret,
              w_out_ab, w_in_c, conv_c, a_log_c, dt_bias_c, g_norm_c, w_out_c):
    B, T = x_prompt.shape[0], x_prompt.shape[1]
    Td = x_sample.shape[1]
    dt_ = x_prompt.dtype
    z_gla = jnp.zeros((N_PAIRS, B, GLA_HEADS, GLA_DK, GLA_DV), dt_)
    z_ret = jnp.zeros((N_PAIRS, B, RET_HEADS, RET_DK, RET_DV), dt_)
    z_delta = jnp.zeros((N_PAIRS, B, GDN_HEADS, GDN_DK, GDN_DV), dt_)
    z_conv = jnp.zeros((N_PAIRS, B, CONV_W - 1, CONV_CH), dt_)
    pos_p = jnp.arange(T, dtype=jnp.int32)
    pos_s = PAST_LEN + jnp.arange(Td, dtype=jnp.int32)
    y_prompt, gla_p, ret_p, delta_p, conv_p = trunk(
        x_prompt, c_prompt, pos_p, z_gla, z_ret, z_delta, z_conv,
        w_ada, b_ada, g_pre, g_post, w_in_ab, w_gla_gate_up, b_gla_gate, g_norm_gla, g_norm_ret,
        w_out_ab, w_in_c, conv_c, a_log_c, dt_bias_c, g_norm_c, w_out_c)
    y_sample, gla_s, ret_s, delta_s, conv_s = trunk(
        x_sample, c_sample, pos_s, state_gla, state_ret, state_delta, state_conv,
        w_ada, b_ada, g_pre, g_post, w_in_ab, w_gla_gate_up, b_gla_gate, g_norm_gla, g_norm_ret,
        w_out_ab, w_in_c, conv_c, a_log_c, dt_bias_c, g_norm_c, w_out_c)
    return (y_prompt, y_sample, gla_p, ret_p, delta_p, conv_p, gla_s, ret_s, delta_s, conv_s)
```

```python
import functools
import math

import numpy as np
import jax
import jax.numpy as jnp
from jax import lax
from jax.experimental import pallas as pl
from jax.experimental.pallas import tpu as pltpu

F32 = jnp.float32
BF16 = jnp.bfloat16

D_MODEL = 1024
EPS = 1e-6
ROPE_BASE = 10000.0
PAST_LEN = 16384
GLA_TAU = 16.0
HEADS_AB = 8
DK_AB = 64
DV = 128
GLA_RANK_PAD = 128
N_AB = 3072 + GLA_RANK_PAD
GDN_HEADS = 8
CONV_W = 4
CONV_CH = 3072
N_C = 4096 + 2 * 128
TILE = 64
CARRY = 8
VMEM_LIMIT_BYTES = 52 * 1024 * 1024


def _mm(a, b):
    return jnp.dot(a.astype(BF16), b.astype(BF16), preferred_element_type=F32)


def _mm_nt(a, b):
    return lax.dot_general(a.astype(BF16), b.astype(BF16), (((1,), (1,)), ((), ())),
                           preferred_element_type=F32)


def _mm_tn(a, b):
    return lax.dot_general(a.astype(BF16), b.astype(BF16), (((0,), (0,)), ((), ())),
                           preferred_element_type=F32)


def _split3(x):
    hi = x.astype(BF16)
    r = x - hi.astype(F32)
    mid = r.astype(BF16)
    lo = (r - mid.astype(F32)).astype(BF16)
    return hi, mid, lo


def _sigmoid(x):
    return 1.0 / (1.0 + jnp.exp(-x))


def _silu(x):
    return x * _sigmoid(x)


def _softplus(x):
    return jnp.maximum(x, 0.0) + jnp.log(1.0 + jnp.exp(-jnp.abs(x)))


def _masks(seglen):
    r = lax.broadcasted_iota(jnp.int32, (TILE, TILE), 0)
    c = lax.broadcasted_iota(jnp.int32, (TILE, TILE), 1)
    sh = int(math.log2(seglen))
    same = (r >> sh) == (c >> sh)
    causal = same & (c <= r)
    eye = r == c
    cum = jnp.concatenate([causal.astype(F32), same.astype(F32)], axis=0).astype(BF16)
    return causal, eye, cum


def _level_masks(seglen):
    r = lax.broadcasted_iota(jnp.int32, (TILE, TILE), 0)
    c = lax.broadcasted_iota(jnp.int32, (TILE, TILE), 1)
    masks = []
    for sh in range(int(math.log2(seglen))):
        masks.append(((r >> (sh + 1)) == (c >> (sh + 1))) & (((r >> sh) & 1) == 1) & (((c >> sh) & 1) == 0))
    return masks


def _seq_index(tile, seg, seglen, bb):
    return 0 if bb == 1 else tile * (TILE // seglen) + seg


def _prefix_and_total(cum, g):
    hi, mid, lo = _split3(g)
    r = (jnp.dot(cum, hi, preferred_element_type=F32) + jnp.dot(cum, mid, preferred_element_type=F32)
         + jnp.dot(cum, lo, preferred_element_type=F32))
    return r[:TILE], r[TILE:]


def _row_to_col(eye, row):
    return jnp.sum(jnp.where(eye, jnp.broadcast_to(row, (TILE, TILE)), 0.0), axis=1, keepdims=True)


def _col_to_row(eye, col):
    return jnp.sum(jnp.where(eye, jnp.broadcast_to(col, (TILE, TILE)), 0.0), axis=0, keepdims=True)


def _prenorm(x_ref, mod_ref, gpre_ref, h_ref, rows):
    x3 = x_ref[...]
    shift = mod_ref[:, :, 0:D_MODEL]
    scale = mod_ref[:, :, D_MODEL:2 * D_MODEL]
    xn = x3 * lax.rsqrt(jnp.mean(x3 * x3, axis=-1, keepdims=True) + EPS) * gpre_ref[...]
    h3 = xn * (1.0 + scale) + shift
    h_ref[...] = h3.reshape(rows, D_MODEL).astype(BF16)


def _postnorm(x_ref, mod_ref, gpost_ref, y, y_ref, bb, tt):
    gate = mod_ref[:, :, 2 * D_MODEL:3 * D_MODEL]
    y3 = y.reshape(bb, tt, D_MODEL)
    yn = y3 * lax.rsqrt(jnp.mean(y3 * y3, axis=-1, keepdims=True) + EPS) * gpost_ref[...]
    y_ref[...] = x_ref[...] + gate * yn


def _ada_kernel(c_ref, w_ref, b_ref, o_ref):
    o_ref[0] = _mm(_silu(c_ref[...]), w_ref[0]) + b_ref[0]


def _ada_call(c_all, w_ada, b_ada):
    n = c_all.shape[0]
    depth = w_ada.shape[0]
    nj = 3
    return pl.pallas_call(
        _ada_kernel,
        grid=(depth, nj),
        in_specs=[
            pl.BlockSpec((n, D_MODEL), lambda l, j: (0, 0)),
            pl.BlockSpec((1, D_MODEL, D_MODEL), lambda l, j: (l, 0, j)),
            pl.BlockSpec((1, 1, D_MODEL), lambda l, j: (l, 0, j)),
        ],
        out_specs=pl.BlockSpec((1, n, D_MODEL), lambda l, j: (l, 0, j)),
        out_shape=jax.ShapeDtypeStruct((depth, n, 3 * D_MODEL), F32),
        compiler_params=pltpu.CompilerParams(dimension_semantics=("arbitrary", "arbitrary")),
        name="adaln_mod",
    )(c_all, w_ada, b_ada.reshape(depth, 1, 3 * D_MODEL))


def _ab_kernel(*refs, bb, tt, seglen, has_state):
    if has_state:
        (x_ref, mod_ref, cos_ref, sin_ref, sg_ref, sr_ref, gpre_ref, win_ref, wgu_ref, bgate_ref, lgam_ref,
         ggla_ref, gret_ref, wout_ref, gpost_ref, y_ref, glao_ref, reto_ref, h_ref, proj_ref, og_ref) = refs
        s_in = (sg_ref, sr_ref)
    else:
        (x_ref, mod_ref, cos_ref, sin_ref, gpre_ref, win_ref, wgu_ref, bgate_ref, lgam_ref,
         ggla_ref, gret_ref, wout_ref, gpost_ref, y_ref, glao_ref, reto_ref, h_ref, proj_ref, og_ref) = refs
        s_in = (glao_ref, reto_ref)
    s_out = (glao_ref, reto_ref)
    rows = bb * tt
    nseg = TILE // seglen
    n_tiles = rows // TILE
    t_id = pl.program_id(1)

    if not has_state:
        @pl.when(t_id == 0)
        def _():
            glao_ref[...] = jnp.zeros_like(glao_ref)
            reto_ref[...] = jnp.zeros_like(reto_ref)

    _prenorm(x_ref, mod_ref, gpre_ref, h_ref, rows)
    for c0 in range(0, N_AB, 640):
        proj_ref[:, c0:c0 + 640] = jnp.dot(h_ref[...], win_ref[:, c0:c0 + 640], preferred_element_type=F32)

    causal, eye, cum = _masks(seglen)
    lane =lax.broadcasted_iota(jnp.int32, (TILE, 256), 1)
    first_half = (lane & 63) < 32
    gnorm = (ggla_ref[...], gret_ref[...])

    def tile_body(i, carry):
        r0 = pl.multiple_of(i * TILE, TILE)
        rs = pl.ds(r0, TILE)

        def pj(a, b):
            return proj_ref[rs, a:b]

        def rot(x):
            swapped = jnp.where(first_half, pltpu.roll(x, 256 - 32, 1), pltpu.roll(x, 32, 1))
            return x * cos_ref[rs, :] + swapped * sin_ref[rs, :]

        log_a = _mm(pj(3072, 3200), wgu_ref[...]) + bgate_ref[...]
        log_a = (jnp.minimum(log_a, 0.0) - jnp.log(1.0 + jnp.exp(-jnp.abs(log_a)))) / GLA_TAU
        q_parts = (pj(0, 256) * (DK_AB ** -0.5), rot(pj(1536, 1792)) * (DK_AB ** -0.5))
        k_parts = (pj(256, 512), rot(pj(1792, 2048)))
        lg_parts = (log_a, jnp.broadcast_to(lgam_ref[...], (TILE, 256)))
        v_off = (512, 2048)
        z_off = (1024, 2560)

        for part in range(2):
            b, btot = _prefix_and_total(cum, lg_parts[part])
            q_dec = q_parts[part] * jnp.exp(b)
            k_inv = k_parts[part] * jnp.exp(-b)
            k_out = k_parts[part] * jnp.exp(btot - b)
            dec = jnp.exp(btot)
            for hh in range(4):
                ks = slice(hh * DK_AB, (hh + 1) * DK_AB)
                v = pj(v_off[part] + hh * DV, v_off[part] + (hh + 1) * DV)
                z = pj(z_off[part] + hh * DV, z_off[part] + (hh + 1) * DV)
                qd, ki, ko = q_dec[:, ks], k_inv[:, ks], k_out[:, ks]
                scores = jnp.where(causal, _mm_nt(qd, ki), 0.0)
                o = _mm(scores, v)
                o_inter = []
                for s in range(nseg):
                    sr = slice(s * seglen, (s + 1) * seglen)
                    bidx = _seq_index(i, s, seglen, bb)
                    st = s_in[part][bidx, hh]
                    o_inter.append(_mm(qd[sr], st))
                    d_col = _row_to_col(eye, dec[s * seglen:s * seglen + 1, ks])
                    s_out[part][bidx, hh] = st * d_col + _mm_tn(ko[sr], v[sr])
                o = o + (o_inter[0] if nseg == 1 else jnp.concatenate(o_inter, axis=0))
                if part == 0:
                    on = o * lax.rsqrt(jnp.mean(o * o, axis=-1, keepdims=True) + EPS)
                else:
                    dlt = o - jnp.mean(o, axis=-1, keepdims=True)
                    on = dlt * lax.rsqrt(jnp.mean(dlt * dlt, axis=-1, keepdims=True) + EPS)
                col = (part * 4 + hh) * DV
                og_ref[rs, col:col + DV] = (on * gnorm[part] * _silu(z)).astype(BF16)
        return carry

    lax.fori_loop(0, n_tiles, tile_body, 0)
    y = jnp.dot(og_ref[...], wout_ref[...], preferred_element_type=F32)
    _postnorm(x_ref, mod_ref, gpost_ref, y, y_ref, bb, tt)


def _ab_call(x, mod, cos, sin, states, weights, *, bb, tt):
    nb_total, t_total, _ = x.shape
    has_state = states is not None
    seglen = min(TILE, tt)
    rows = bb * tt
    assert nb_total % bb == 0 and t_total % tt == 0 and rows % TILE == 0 and TILE % seglen == 0
    assert tt % 8 == 0 and (bb == 1 or tt == seglen)
    grid = (nb_total // bb, t_total // tt)
    assert not has_state or grid[1] == 1
    const2 = lambda b, t: (0, 0)
    st_spec = pl.BlockSpec((bb, 4, DK_AB, DV), lambda b, t: (b, 0, 0, 0))
    in_specs = [
        pl.BlockSpec((bb, tt, D_MODEL), lambda b, t: (b, t, 0)),
        pl.BlockSpec((bb, 1, 3 * D_MODEL), lambda b, t: (b, 0, 0)),
    ]
    if has_state:
        in_specs += [pl.BlockSpec((rows, 256), const2)] * 2 + [st_spec, st_spec]
    else:
        in_specs += [pl.BlockSpec((tt, 256), lambda b, t: (t, 0))] * 2
    in_specs += [pl.BlockSpec(w.shape, const2) for w in weights]
    args = (x, mod, cos, sin) + (tuple(states) if has_state else ()) + tuple(weights)
    st_shape = jax.ShapeDtypeStruct((nb_total, 4, DK_AB, DV), F32)
    return pl.pallas_call(
        functools.partial(_ab_kernel, bb=bb, tt=tt, seglen=seglen, has_state=has_state),
        grid=grid,
        in_specs=in_specs,
        out_specs=[pl.BlockSpec((bb, tt, D_MODEL), lambda b, t: (b, t, 0)), st_spec, st_spec],
        out_shape=[jax.ShapeDtypeStruct(x.shape, F32), st_shape, st_shape],
        scratch_shapes=[pltpu.VMEM((rows, D_MODEL), BF16), pltpu.VMEM((rows, N_AB), F32),
                        pltpu.VMEM((rows, D_MODEL), BF16)],
        compiler_params=pltpu.CompilerParams(dimension_semantics=("arbitrary", "arbitrary"),
                                             vmem_limit_bytes=VMEM_LIMIT_BYTES),
        name="gla_ret_layer_" + ("sample" if has_state else "prompt"),
    )(*args)


def _c_kernel(*refs, bb, tt, seglen, has_state):
    if has_state:
        (x_ref, mod_ref, sd_ref, sc_ref, gpre_ref, win_ref, convw_ref, alog_ref, dtb_ref, gn_ref, wout_ref,
         gpost_ref, y_ref, do_ref, co_ref, h_ref, cbuf_ref, qkv_ref, zab_ref, og_ref) = refs
        s_in = sd_ref
    else:
        (x_ref, mod_ref, gpre_ref, win_ref, convw_ref, alog_ref, dtb_ref, gn_ref, wout_ref,
         gpost_ref, y_ref, do_ref, co_ref, h_ref, cbuf_ref, qkv_ref, zab_ref, og_ref) = refs
        s_in = do_ref
    rows = bb * tt
    nseg = TILE // seglen
    n_tiles = rows // TILE
    t_id = pl.program_id(1)
    keep =slice(CARRY + tt - (CONV_W - 1), CARRY + tt)
    slot = slice(CARRY - (CONV_W - 1), CARRY)

    @pl.when(t_id > 0)
    def _():
        cbuf_ref[:, slot, :] = cbuf_ref[:, keep, :]

    @pl.when(t_id == 0)
    def _():
        if has_state:
            cbuf_ref[:, slot, :] = sc_ref[...]
        else:
            cbuf_ref[:, slot, :] = jnp.zeros((bb, CONV_W - 1, CONV_CH), F32)
            do_ref[...] = jnp.zeros_like(do_ref)

    _prenorm(x_ref, mod_ref, gpre_ref, h_ref, rows)
    for c0 in range(0, CONV_CH, 512):
        pr = jnp.dot(h_ref[...], win_ref[:, c0:c0 + 512], preferred_element_type=F32)
        cbuf_ref[:, CARRY:CARRY + tt, c0:c0 + 512] = pr.reshape(bb, tt, 512)
    for c0 in range(CONV_CH, N_C, 640):
        zab_ref[:, c0 - CONV_CH:c0 - CONV_CH + 640] = jnp.dot(h_ref[...], win_ref[:, c0:c0 + 640],
                                                             preferred_element_type=F32)
    co_ref[...] = cbuf_ref[:, keep, :]

    for c0 in range(0, CONV_CH, 512):
        acc = None
        for i in range(CONV_W):
            off = CARRY - (CONV_W - 1) + i
            term = cbuf_ref[:, off:off + tt, c0:c0 + 512] * convw_ref[i:i + 1, c0:c0 + 512]
            acc = term if acc is None else acc + term
        qkv_ref[:, c0:c0 + 512] = _silu(acc).reshape(rows, 512)

    causal, eye, cum = _masks(seglen)
    level_masks = _level_masks(seglen)
    eye_f = eye.astype(F32)
    neg_decay_rate = -jnp.exp(alog_ref[...])

    def tile_body(i, carry):
        r0 = pl.multiple_of(i * TILE, TILE)
        rs = pl.ds(r0, TILE)
        beta_all = _sigmoid(zab_ref[rs, 1024:1152])
        log_g = neg_decay_rate * _softplus(zab_ref[rs, 1152:1280] + dtb_ref[...])
        b_all, btot_all = _prefix_and_total(cum, log_g)

        for hh in range(GDN_HEADS):
            hs = slice(hh * DV, (hh + 1) * DV)
            q = qkv_ref[rs, hh * DV:(hh + 1) * DV]
            k = qkv_ref[rs, 1024 + hh * DV:1024 + (hh + 1) * DV]
            v = qkv_ref[rs, 2048 + hh * DV:2048 + (hh + 1) * DV]
            z = zab_ref[rs, hs]
            qn = q * lax.rsqrt(jnp.sum(q * q, axis=-1, keepdims=True) + EPS) * (DV ** -0.5)
            kn = k * lax.rsqrt(jnp.sum(k * k, axis=-1, keepdims=True) + EPS)
            beta = beta_all[:, hh:hh + 1]
            b_c = b_all[:, hh:hh + 1]
            btot_c = btot_all[:, hh:hh + 1]
            b_r = _col_to_row(eye, b_c)
            seg = jnp.where(causal, jnp.exp(jnp.where(causal, b_c - b_r, 0.0)), 0.0)
            k_beta = kn * beta
            v_beta = v * beta
            a_mat = _mm_nt(k_beta, kn) * seg
            inv = eye_f - jnp.where(level_masks[0], a_mat, 0.0)
            for lm in level_masks[1:]:
                inv = inv - _mm(_mm(inv, jnp.where(lm, a_mat, 0.0)), inv)
            e_b = jnp.exp(b_c)
            sol = _mm(inv, jnp.concatenate([v_beta, k_beta * e_b], axis=1))
            u, w = sol[:, :DV], sol[:, DV:]
            q_dec = qn * e_b
            attn = _mm_nt(qn, kn) * seg
            k_out = kn * jnp.exp(btot_c - b_c)
            v_new, o_inter, states = [], [], []
            for s in range(nseg):
                sr = slice(s * seglen, (s + 1) * seglen)
                st = s_in[_seq_index(i, s, seglen, bb), hh]
                ws = _mm(jnp.concatenate([w[sr], q_dec[sr]], axis=0), st)
                v_new.append(u[sr] - ws[:seglen])
                o_inter.append(ws[seglen:])
                states.append(st)
            v_new = v_new[0] if nseg == 1 else jnp.concatenate(v_new, axis=0)
            o_inter = o_inter[0] if nseg == 1 else jnp.concatenate(o_inter, axis=0)
            o = o_inter + _mm(attn, v_new)
            for s in range(nseg):
                sr = slice(s * seglen, (s + 1) * seglen)
                dec = jnp.exp(btot_c[s * seglen:s * seglen + 1, :])
                do_ref[_seq_index(i, s, seglen, bb), hh] = states[s] * dec + _mm_tn(k_out[sr], v_new[sr])
            on = o * lax.rsqrt(jnp.mean(o * o, axis=-1, keepdims=True) + EPS) * gn_ref[...]
            og_ref[rs, hs] = (on * _silu(z)).astype(BF16)
        return carry

    lax.fori_loop(0, n_tiles, tile_body, 0)
    y = jnp.dot(og_ref[...], wout_ref[...], preferred_element_type=F32)
    _postnorm(x_ref, mod_ref, gpost_ref, y, y_ref, bb, tt)


def _c_call(x, mod, states, weights, *, bb, tt):
    nb_total, t_total, _ = x.shape
    has_state = states is not None
    seglen = min(TILE, tt)
    rows = bb * tt
    assert nb_total % bb == 0 and t_total % tt == 0 and rows % TILE == 0 and TILE % seglen == 0
    assert tt % 8 == 0 and tt >= CONV_W - 1 and (bb == 1 or tt == seglen)
    grid = (nb_total // bb, t_total // tt)
    assert not has_state or grid[1] == 1
    const2 = lambda b, t: (0, 0)
    sd_spec = pl.BlockSpec((bb, GDN_HEADS, DV, DV), lambda b, t: (b, 0, 0, 0))
    sc_spec = pl.BlockSpec((bb, CONV_W - 1, CONV_CH), lambda b, t: (b, 0, 0))
    in_specs = [
        pl.BlockSpec((bb, tt, D_MODEL), lambda b, t: (b, t, 0)),
        pl.BlockSpec((bb, 1, 3 * D_MODEL), lambda b, t: (b, 0, 0)),
    ]
    if has_state:
        in_specs += [sd_spec, sc_spec]
    in_specs += [pl.BlockSpec(w.shape, const2) for w in weights]
    args = (x, mod) + (tuple(states) if has_state else ()) + tuple(weights)
    return pl.pallas_call(
        functools.partial(_c_kernel, bb=bb, tt=tt, seglen=seglen, has_state=has_state),
        grid=grid,
        in_specs=in_specs,
        out_specs=[pl.BlockSpec((bb, tt, D_MODEL), lambda b, t: (b, t, 0)), sd_spec, sc_spec],
        out_shape=[jax.ShapeDtypeStruct(x.shape, F32),
                   jax.ShapeDtypeStruct((nb_total, GDN_HEADS, DV, DV), F32),
                   jax.ShapeDtypeStruct((nb_total, CONV_W - 1, CONV_CH), F32)],
        scratch_shapes=[pltpu.VMEM((rows, D_MODEL), BF16),
                        pltpu.VMEM((bb, CARRY + tt, CONV_CH), F32),
                        pltpu.VMEM((rows, CONV_CH), F32),
                        pltpu.VMEM((rows, N_C - CONV_CH), F32),
                        pltpu.VMEM((rows, D_MODEL), BF16)],
        compiler_params=pltpu.CompilerParams(dimension_semantics=("arbitrary", "arbitrary"),
                                             vmem_limit_bytes=VMEM_LIMIT_BYTES),
        name="gdn_layer_" + ("sample" if has_state else "prompt"),
    )(*args)


def _rope_tables(pos):
    half = DK_AB // 2
    inv = ROPE_BASE ** (-jnp.arange(half, dtype=F32) / half)
    ang = pos.astype(F32)[:, None] * inv[None, :]
    cos, sin = jnp.cos(ang), jnp.sin(ang)
    cos_h = jnp.concatenate([cos, cos], axis=-1)
    sin_h = jnp.concatenate([-sin, sin], axis=-1)
    return jnp.tile(cos_h, (1, 4)), jnp.tile(sin_h, (1, 4))


def _row(a, width=None):
    a = a.reshape(1, -1).astype(F32)
    if width is not None and a.shape[1] < width:
        a = jnp.pad(a, ((0, 0), (0, width - a.shape[1])))
    return a


def kernel(x_prompt, x_sample, state_gla, state_ret, state_delta, state_conv, c_prompt, c_sample, w_ada, b_ada, g_pre, g_post, w_in_ab, w_gla_gate_up, b_gla_gate, g_norm_gla, g_norm_ret, w_out_ab, w_in_c, conv_c, a_log_c, dt_bias_c, g_norm_c, w_out_c):
    nbp, t_p, _ = x_prompt.shape
    nbs, t_s, _ = x_sample.shape
    assert w_ada.shape[0] == 2 and w_in_ab.shape[0] == 1

    mod = _ada_call(jnp.concatenate([c_prompt, c_sample], axis=0), w_ada, b_ada)
    mod_p = mod[:, :nbp, None, :]
    mod_s = mod[:, nbp:, None, :]

    w = w_in_ab[0]
    w_ab = jnp.concatenate([w[:, :1536], w[:, 1552:], w[:, 1536:1552],
                            jnp.zeros((D_MODEL, GLA_RANK_PAD - 16), F32)], axis=1).astype(BF16)
    wgu = jnp.pad(w_gla_gate_up[0], ((0, GLA_RANK_PAD - 16), (0, 0))).astype(BF16)
    lgam = jnp.repeat(jnp.log1p(-jnp.exp2(-5.0 - jnp.arange(4, dtype=F32))), DK_AB).reshape(1, 256)
    weights_ab = (_row(g_pre[0]), w_ab, wgu, _row(b_gla_gate[0]), lgam, _row(g_norm_gla[0]),
                  _row(g_norm_ret[0]), w_out_ab[0].astype(BF16), _row(g_post[0]))

    w = w_in_c[0]
    zpad = jnp.zeros((D_MODEL, 128 - GDN_HEADS), F32)
    w_c = jnp.concatenate([w[:, :4096], w[:, 4096:4104], zpad, w[:, 4104:4112], zpad], axis=1).astype(BF16)
    weights_c = (_row(g_pre[1]), w_c, conv_c[0], _row(a_log_c[0], 128), _row(dt_bias_c[0], 128),
                 _row(g_norm_c[0]), w_out_c[0].astype(BF16), _row(g_post[1]))

    cos_p, sin_p = _rope_tables(jnp.arange(t_p, dtype=jnp.int32))
    cos_s, sin_s = _rope_tables(PAST_LEN + jnp.arange(t_s, dtype=jnp.int32))
    bb_s = TILE // t_s
    cos_s, sin_s = jnp.tile(cos_s, (bb_s, 1)), jnp.tile(sin_s, (bb_s, 1))

    tt_p = 256
    xp, gla_p, ret_p = _ab_call(x_prompt, mod_p[0], cos_p, sin_p, None, weights_ab, bb=1, tt=tt_p)
    y_prompt, delta_p, conv_p = _c_call(xp, mod_p[1], None, weights_c, bb=1, tt=tt_p)

    xs, gla_s, ret_s = _ab_call(x_sample, mod_s[0], cos_s, sin_s,
                                (state_gla.reshape(state_gla.shape[1:]), state_ret.reshape(state_ret.shape[1:])),
                                weights_ab, bb=bb_s, tt=t_s)
    y_sample, delta_s, conv_s = _c_call(xs, mod_s[1],
                                        (state_delta.reshape(state_delta.shape[1:]),
                                         state_conv.reshape(state_conv.shape[1:])),
                                        weights_c, bb=bb_s, tt=t_s)
    return (y_prompt, y_sample, gla_p[None], ret_p[None], delta_p[None], conv_p[None],
            gla_s[None], ret_s[None], delta_s[None], conv_s[None])
```

```python
import functools
import math

import numpy as np
import jax
import jax.numpy as jnp
from jax import lax
from jax.experimental import pallas as pl
from jax.experimental.pallas import tpu as pltpu

F32 = jnp.float32
BF16 = jnp.bfloat16

D_MODEL = 1024
EPS = 1e-6
ROPE_BASE = 10000.0
PAST_LEN = 16384
GLA_TAU = 16.0
HEADS_AB = 8
DK_AB = 64
DV = 128
GLA_RANK_PAD = 128
N_AB = 3072 + GLA_RANK_PAD
GDN_HEADS = 8
CONV_W = 4
CONV_CH = 3072
N_C = 4096 + 2 * 128
TILE = 64
CARRY = 8
VMEM_LIMIT_BYTES = 52 * 1024 * 1024


def _mm(a, b):
    return jnp.dot(a.astype(BF16), b.astype(BF16), preferred_element_type=F32)


def _mm_nt(a, b):
    return lax.dot_general(a.astype(BF16), b.astype(BF16), (((1,), (1,)), ((), ())),
                           preferred_element_type=F32)


def _mm_tn(a, b):
    return lax.dot_general(a.astype(BF16), b.astype(BF16), (((0,), (0,)), ((), ())),
                           preferred_element_type=F32)


def _bmm(a, b):
    return lax.dot_general(a.astype(BF16), b.astype(BF16), (((2,), (1,)), ((0,), (0,))),
                           preferred_element_type=F32)


def _bmm_nt(a, b):
    return lax.dot_general(a.astype(BF16), b.astype(BF16), (((2,), (2,)), ((0,), (0,))),
                           preferred_element_type=F32)


def _bmm_tn(a, b):
    return lax.dot_general(a.astype(BF16), b.astype(BF16), (((1,), (1,)), ((0,), (0,))),
                           preferred_element_type=F32)


def _split3(x):
    hi = x.astype(BF16)
    r = x - hi.astype(F32)
    mid = r.astype(BF16)
    lo = (r - mid.astype(F32)).astype(BF16)
    return hi, mid, lo


def _sigmoid(x):
    return 1.0 / (1.0 + jnp.exp(-x))


def _silu(x):
    return x * _sigmoid(x)


def _softplus(x):
    return jnp.maximum(x, 0.0) + jnp.log(1.0 + jnp.exp(-jnp.abs(x)))


def _masks(seglen, nh):
    r = lax.broadcasted_iota(jnp.int32, (nh, TILE, TILE), 1)
    c = lax.broadcasted_iota(jnp.int32, (nh, TILE, TILE), 2)
    sh = int(math.log2(seglen))
    same = (r >> sh) == (c >> sh)
    causal = same & (c <= r)
    eye = r == c
    cum = jnp.concatenate([causal[0].astype(F32), same[0].astype(F32)], axis=0).astype(BF16)
    return causal, eye, cum


def _level_masks(seglen, nh):
    r = lax.broadcasted_iota(jnp.int32, (nh, TILE, TILE), 1)
    c = lax.broadcasted_iota(jnp.int32, (nh, TILE, TILE), 2)
    masks = []
    for sh in range(int(math.log2(seglen))):
        masks.append(((r >> (sh + 1)) == (c >> (sh + 1))) & (((r >> sh) & 1) == 1) & (((c >> sh) & 1) == 0))
    return masks


def _seq_index(tile, seg, seglen, bb):
    return 0 if bb == 1 else tile * (TILE // seglen) + seg


def _prefix_and_total(cum, g):
    hi, mid, lo = _split3(g)
    r = (jnp.dot(cum, hi, preferred_element_type=F32) + jnp.dot(cum, mid, preferred_element_type=F32)
         + jnp.dot(cum, lo, preferred_element_type=F32))
    return r[:TILE], r[TILE:]


def _row_to_col(eye, row):
    return jnp.sum(jnp.where(eye, jnp.broadcast_to(row, eye.shape), 0.0), axis=2, keepdims=True)


def _col_to_row(eye, col):
    return jnp.sum(jnp.where(eye, jnp.broadcast_to(col, eye.shape), 0.0), axis=1, keepdims=True)


def _prenorm(x_ref, mod_ref, gpre_ref, h_ref, rows):
    x3 = x_ref[...]
    shift = mod_ref[:, :, 0:D_MODEL]
    scale = mod_ref[:, :, D_MODEL:2 * D_MODEL]
    xn = x3 * lax.rsqrt(jnp.mean(x3 * x3, axis=-1, keepdims=True) + EPS) * gpre_ref[...]
    h3 = xn * (1.0 + scale) + shift
    h_ref[...] = h3.reshape(rows, D_MODEL).astype(BF16)


def _postnorm(x_ref, mod_ref, gpost_ref, y, y_ref, bb, tt):
    gate = mod_ref[:, :, 2 * D_MODEL:3 * D_MODEL]
    y3 = y.reshape(bb, tt, D_MODEL)
    yn = y3 * lax.rsqrt(jnp.mean(y3 * y3, axis=-1, keepdims=True) + EPS) * gpost_ref[...]
    y_ref[...] = x_ref[...] + gate * yn


def _ada_kernel(c_ref, w_ref, b_ref, o_ref):
    o_ref[0] = _mm(_silu(c_ref[...]), w_ref[0]) + b_ref[0]


def _ada_call(c_all, w_ada, b_ada):
    n = c_all.shape[0]
    depth = w_ada.shape[0]
    nj = 3
    return pl.pallas_call(
        _ada_kernel,
        grid=(depth, nj),
        in_specs=[
            pl.BlockSpec((n, D_MODEL), lambda l, j: (0, 0)),
            pl.BlockSpec((1, D_MODEL, D_MODEL), lambda l, j: (l, 0, j)),
            pl.BlockSpec((1, 1, D_MODEL), lambda l, j: (l, 0, j)),
        ],
        out_specs=pl.BlockSpec((1, n, D_MODEL), lambda l, j: (l, 0, j)),
        out_shape=jax.ShapeDtypeStruct((depth, n, 3 * D_MODEL), F32),
        compiler_params=pltpu.CompilerParams(dimension_semantics=("arbitrary", "arbitrary")),
        name="adaln_mod",
    )(c_all, w_ada, b_ada.reshape(depth, 1, 3 * D_MODEL))


def _ab_kernel(*refs, bb, tt, seglen, has_state):
    if has_state:
        (x_ref, mod_ref, cos_ref, sin_ref, sg_ref, sr_ref, gpre_ref, win_ref, wgu_ref, bgate_ref, lgam_ref,
         ggla_ref, gret_ref, wout_ref, gpost_ref, y_ref, glao_ref, reto_ref, h_ref, proj_ref, og_ref) = refs
        s_in = (sg_ref, sr_ref)
    else:
        (x_ref, mod_ref, cos_ref, sin_ref, gpre_ref, win_ref, wgu_ref, bgate_ref, lgam_ref,
         ggla_ref, gret_ref, wout_ref, gpost_ref, y_ref, glao_ref, reto_ref, h_ref, proj_ref, og_ref) = refs
        s_in = (glao_ref, reto_ref)
    s_out = (glao_ref, reto_ref)
    rows = bb * tt
    nseg = TILE // seglen
    n_tiles = rows // TILE
    t_id = pl.program_id(1)

    if not has_state:
        @pl.when(t_id == 0)
        def _():
            glao_ref[...] = jnp.zeros_like(glao_ref)
            reto_ref[...] = jnp.zeros_like(reto_ref)

    _prenorm(x_ref, mod_ref, gpre_ref, h_ref, rows)
    for c0 in range(0, N_AB, 640):
        proj_ref[:, c0:c0 + 640] = jnp.dot(h_ref[...], win_ref[:, c0:c0 + 640], preferred_element_type=F32)

    nh = HEADS_AB
    causal, eye, cum = _masks(seglen, nh)
    lane = lax.broadcasted_iota(jnp.int32, (TILE, 256), 1)
    first_half = (lane & 63) < 32
    gnorm = (ggla_ref[...], gret_ref[...])

    def tile_body(i, carry):
        r0 = pl.multiple_of(i * TILE, TILE)
        rs = pl.ds(r0, TILE)

        def pj(a, b):
            return proj_ref[rs, a:b]

        def rot(x):
            swapped = jnp.where(first_half, pltpu.roll(x, 256 - 32, 1), pltpu.roll(x, 32, 1))
            return x * cos_ref[rs, :] + swapped * sin_ref[rs, :]

        def heads(parts, width):
            return jnp.stack([p[:, h * width:(h + 1) * width] for p in parts for h in range(4)])

        log_a = _mm(pj(3072, 3200), wgu_ref[...]) + bgate_ref[...]
        log_a = (jnp.minimum(log_a, 0.0) - jnp.log(1.0 + jnp.exp(-jnp.abs(log_a)))) / GLA_TAU
        q_parts = (pj(0, 256) * (DK_AB ** -0.5), rot(pj(1536, 1792)) * (DK_AB ** -0.5))
        k_parts = (pj(256, 512), rot(pj(1792, 2048)))
        lg_parts = (log_a, jnp.broadcast_to(lgam_ref[...], (TILE, 256)))
        q_dec, k_inv, k_out, dec = [], [], [], []
        for part in range(2):
            b, btot = _prefix_and_total(cum, lg_parts[part])
            q_dec.append(q_parts[part] * jnp.exp(b))
            k_inv.append(k_parts[part] * jnp.exp(-b))
            k_out.append(k_parts[part] * jnp.exp(btot - b))
            dec.append(jnp.exp(btot))
        qd, ki, ko = heads(q_dec, DK_AB), heads(k_inv, DK_AB), heads(k_out, DK_AB)
        v = heads((pj(512, 1024), pj(2048, 2560)), DV)
        z = heads((pj(1024, 1536), pj(2560, 3072)), DV)
        scores = jnp.where(causal, _bmm_nt(qd, ki), 0.0)
        o = _bmm(scores, v)
        o_inter = []
        for s in range(nseg):
            sr = slice(s * seglen, (s + 1) * seglen)
            bidx = _seq_index(i, s, seglen, bb)
            st = jnp.concatenate([s_in[0][bidx], s_in[1][bidx]], axis=0)
            o_inter.append(_bmm(qd[:, sr], st))
            dec_row = heads([d[s * seglen:s * seglen + 1] for d in dec], DK_AB)
            st = st * _row_to_col(eye, dec_row) + _bmm_tn(ko[:, sr], v[:, sr])
            s_out[0][bidx] = st[:4]
            s_out[1][bidx] = st[4:]
        o = o + (o_inter[0] if nseg == 1 else jnp.concatenate(o_inter, axis=1))
        o_g, o_r = o[:4], o[4:]
        on_g = o_g * lax.rsqrt(jnp.mean(o_g * o_g, axis=-1, keepdims=True) + EPS) * gnorm[0]
        dlt = o_r - jnp.mean(o_r, axis=-1, keepdims=True)
        on_r = dlt * lax.rsqrt(jnp.mean(dlt * dlt, axis=-1, keepdims=True) + EPS) * gnorm[1]
        og = (jnp.concatenate([on_g, on_r], axis=0) * _silu(z)).astype(BF16)
        for h in range(nh):
            og_ref[rs, h * DV:(h + 1) * DV] = og[h]
        return carry

    lax.fori_loop(0, n_tiles, tile_body, 0)
    y = jnp.dot(og_ref[...], wout_ref[...], preferred_element_type=F32)
    _postnorm(x_ref, mod_ref, gpost_ref, y, y_ref, bb, tt)


def _ab_call(x, mod, cos, sin, states, weights, *, bb, tt):
    nb_total, t_total, _ = x.shape
    has_state = states is not None
    seglen = min(TILE, tt)
    rows = bb * tt
    assert nb_total % bb == 0 and t_total % tt == 0 and rows % TILE == 0 and TILE % seglen == 0
    assert tt % 8 == 0 and (bb == 1 or tt == seglen)
    grid = (nb_total // bb, t_total // tt)
    assert not has_state or grid[1] == 1
    const2 = lambda b, t: (0, 0)
    st_spec = pl.BlockSpec((bb, 4, DK_AB, DV), lambda b, t: (b, 0, 0, 0))
    in_specs = [
        pl.BlockSpec((bb, tt, D_MODEL), lambda b, t: (b, t, 0)),
        pl.BlockSpec((bb, 1, 3 * D_MODEL), lambda b, t: (b, 0, 0)),
    ]
    if has_state:
        in_specs += [pl.BlockSpec((rows, 256), const2)] * 2 + [st_spec, st_spec]
    else:
        in_specs += [pl.BlockSpec((tt, 256), lambda b, t: (t, 0))] * 2
    in_specs += [pl.BlockSpec(w.shape, const2) for w in weights]
    args = (x, mod, cos, sin) + (tuple(states) if has_state else ()) + tuple(weights)
    st_shape = jax.ShapeDtypeStruct((nb_total, 4, DK_AB, DV), F32)
    return pl.pallas_call(
        functools.partial(_ab_kernel, bb=bb, tt=tt, seglen=seglen, has_state=has_state),
        grid=grid,
        in_specs=in_specs,
        out_specs=[pl.BlockSpec((bb, tt, D_MODEL), lambda b, t: (b, t, 0)), st_spec, st_spec],
        out_shape=[jax.ShapeDtypeStruct(x.shape, F32), st_shape, st_shape],
        scratch_shapes=[pltpu.VMEM((rows, D_MODEL), BF16), pltpu.VMEM((rows, N_AB), F32),
                        pltpu.VMEM((rows, D_MODEL), BF16)],
        compiler_params=pltpu.CompilerParams(dimension_semantics=("arbitrary", "arbitrary"),
                                             vmem_limit_bytes=VMEM_LIMIT_BYTES),
        name="gla_ret_layer_" + ("sample" if has_state else "prompt"),
    )(*args)


def _c_kernel(*refs, bb, tt, seglen, has_state):
    if has_state:
        (x_ref, mod_ref, sd_ref, sc_ref, gpre_ref, win_ref, convw_ref, alog_ref, dtb_ref, gn_ref, wout_ref,
         gpost_ref, y_ref, do_ref, co_ref, h_ref, cbuf_ref, qkv_ref, zab_ref, og_ref) = refs
        s_in = sd_ref
    else:
        (x_ref, mod_ref, gpre_ref, win_ref, convw_ref, alog_ref, dtb_ref, gn_ref, wout_ref,
         gpost_ref, y_ref, do_ref, co_ref, h_ref, cbuf_ref, qkv_ref, zab_ref, og_ref) = refs
        s_in = do_ref
    rows = bb * tt
    nseg = TILE // seglen
    n_tiles = rows // TILE
    t_id = pl.program_id(1)
    keep =slice(CARRY + tt - (CONV_W - 1), CARRY + tt)
    slot = slice(CARRY - (CONV_W - 1), CARRY)

    @pl.when(t_id > 0)
    def _():
        cbuf_ref[:, slot, :] = cbuf_ref[:, keep, :]

    @pl.when(t_id == 0)
    def _():
        if has_state:
            cbuf_ref[:, slot, :] = sc_ref[...]
        else:
            cbuf_ref[:, slot, :] = jnp.zeros((bb, CONV_W - 1, CONV_CH), F32)
            do_ref[...] = jnp.zeros_like(do_ref)

    _prenorm(x_ref, mod_ref, gpre_ref, h_ref, rows)
    for c0 in range(0, CONV_CH, 512):
        pr = jnp.dot(h_ref[...], win_ref[:, c0:c0 + 512], preferred_element_type=F32)
        cbuf_ref[:, CARRY:CARRY + tt, c0:c0 + 512] = pr.reshape(bb, tt, 512)
    for c0 in range(CONV_CH, N_C, 640):
        zab_ref[:, c0 - CONV_CH:c0 - CONV_CH + 640] = jnp.dot(h_ref[...], win_ref[:, c0:c0 + 640],
                                                             preferred_element_type=F32)
    co_ref[...] = cbuf_ref[:, keep, :]

    for c0 in range(0, CONV_CH, 512):
        acc = None
        for i in range(CONV_W):
            off = CARRY - (CONV_W - 1) + i
            term = cbuf_ref[:, off:off + tt, c0:c0 + 512] * convw_ref[i:i + 1, c0:c0 + 512]
            acc = term if acc is None else acc + term
        qkv_ref[:, c0:c0 + 512] = _silu(acc).reshape(rows, 512)

    nh = GDN_HEADS
    causal, eye, cum = _masks(seglen, nh)
    level_masks = _level_masks(seglen, nh)
    eye_f = eye.astype(F32)
    neg_decay_rate = -jnp.exp(alog_ref[...])

    def tile_body(i, carry):
        r0 = pl.multiple_of(i * TILE, TILE)
        rs = pl.ds(r0, TILE)
        beta_all = _sigmoid(zab_ref[rs, 1024:1152])
        log_g = neg_decay_rate * _softplus(zab_ref[rs, 1152:1280] + dtb_ref[...])
        b_all, btot_all = _prefix_and_total(cum, log_g)

        def heads(ref, base):
            return jnp.stack([ref[rs, base + h * DV:base + (h + 1) * DV] for h in range(nh)])

        def cols(a):
            return jnp.stack([a[:, h:h + 1] for h in range(nh)])

        q, k, v = heads(qkv_ref, 0), heads(qkv_ref, 1024), heads(qkv_ref, 2048)
        z = heads(zab_ref, 0)
        qn = q * lax.rsqrt(jnp.sum(q * q, axis=-1, keepdims=True) + EPS) * (DV ** -0.5)
        kn = k * lax.rsqrt(jnp.sum(k * k, axis=-1, keepdims=True) + EPS)
        beta, b_c, btot_c = cols(beta_all), cols(b_all), cols(btot_all)
        b_r = _col_to_row(eye, b_c)
        seg = jnp.where(causal, jnp.exp(jnp.where(causal, b_c - b_r, 0.0)), 0.0)
        k_beta = kn * beta
        v_beta = v * beta
        a_mat = _bmm_nt(k_beta, kn) * seg
        inv = eye_f - jnp.where(level_masks[0], a_mat, 0.0)
        for lm in level_masks[1:]:
            inv = inv - _bmm(_bmm(inv, jnp.where(lm, a_mat, 0.0)), inv)
        e_b = jnp.exp(b_c)
        sol = _bmm(inv, jnp.concatenate([v_beta, k_beta * e_b], axis=2))
        u, w = sol[:, :, :DV], sol[:, :, DV:]
        q_dec = qn * e_b
        attn = _bmm_nt(qn, kn) * seg
        k_out = kn * jnp.exp(btot_c - b_c)
        v_new, o_inter, states = [], [], []
        for s in range(nseg):
            sr = slice(s * seglen, (s + 1) * seglen)
            st = s_in[_seq_index(i, s, seglen, bb)]
            ws = _bmm(jnp.concatenate([w[:, sr], q_dec[:, sr]], axis=1), st)
            v_new.append(u[:, sr] - ws[:, :seglen])
            o_inter.append(ws[:, seglen:])
            states.append(st)
        v_new = v_new[0] if nseg == 1 else jnp.concatenate(v_new, axis=1)
        o_inter = o_inter[0] if nseg == 1 else jnp.concatenate(o_inter, axis=1)
        o = o_inter + _bmm(attn, v_new)
        for s in range(nseg):
            sr = slice(s * seglen, (s + 1) * seglen)
            dec = jnp.exp(btot_c[:, s * seglen:s * seglen + 1, :])
            do_ref[_seq_index(i, s, seglen, bb)] = states[s] * dec + _bmm_tn(k_out[:, sr], v_new[:, sr])
        on = o * lax.rsqrt(jnp.mean(o * o, axis=-1, keepdims=True) + EPS) * gn_ref[...]
        og = (on * _silu(z)).astype(BF16)
        for h in range(nh):
            og_ref[rs, h * DV:(h + 1) * DV] = og[h]
        return carry

    lax.fori_loop(0, n_tiles, tile_body, 0)
    y = jnp.dot(og_ref[...], wout_ref[...], preferred_element_type=F32)
    _postnorm(x_ref, mod_ref, gpost_ref, y, y_ref, bb, tt)


def _c_call(x, mod, states, weights, *, bb, tt):
    nb_total, t_total, _ = x.shape
    has_state = states is not None
    seglen = min(TILE, tt)
    rows = bb * tt
    assert nb_total % bb == 0 and t_total % tt == 0 and rows % TILE == 0 and TILE % seglen == 0
    assert tt % 8 == 0 and tt >= CONV_W - 1 and (bb == 1 or tt == seglen)
    grid = (nb_total // bb, t_total // tt)
    assert not has_state or grid[1] == 1
    const2 = lambda b, t: (0, 0)
    sd_spec = pl.BlockSpec((bb, GDN_HEADS, DV, DV), lambda b, t: (b, 0, 0, 0))
    sc_spec = pl.BlockSpec((bb, CONV_W - 1, CONV_CH), lambda b, t: (b, 0, 0))
    in_specs = [
        pl.BlockSpec((bb, tt, D_MODEL), lambda b, t: (b, t, 0)),
        pl.BlockSpec((bb, 1, 3 * D_MODEL), lambda b, t: (b, 0, 0)),
    ]
    if has_state:
        in_specs += [sd_spec, sc_spec]
    in_specs += [pl.BlockSpec(w.shape, const2) for w in weights]
    args = (x, mod) + (tuple(states) if has_state else ()) + tuple(weights)
    return pl.pallas_call(
        functools.partial(_c_kernel, bb=bb, tt=tt, seglen=seglen, has_state=has_state),
        grid=grid,
        in_specs=in_specs,
        out_specs=[pl.BlockSpec((bb, tt, D_MODEL), lambda b, t: (b, t, 0)), sd_spec, sc_spec],
        out_shape=[jax.ShapeDtypeStruct(x.shape, F32),
                   jax.ShapeDtypeStruct((nb_total, GDN_HEADS, DV, DV), F32),
                   jax.ShapeDtypeStruct((nb_total, CONV_W - 1, CONV_CH), F32)],
        scratch_shapes=[pltpu.VMEM((rows, D_MODEL), BF16),
                        pltpu.VMEM((bb, CARRY + tt, CONV_CH), F32),
                        pltpu.VMEM((rows, CONV_CH), F32),
                        pltpu.VMEM((rows, N_C - CONV_CH), F32),
                        pltpu.VMEM((rows, D_MODEL), BF16)],
        compiler_params=pltpu.CompilerParams(dimension_semantics=("arbitrary", "arbitrary"),
                                             vmem_limit_bytes=VMEM_LIMIT_BYTES),
        name="gdn_layer_" + ("sample" if has_state else "prompt"),
    )(*args)


def _rope_tables(pos):
    half = DK_AB // 2
    inv = ROPE_BASE ** (-jnp.arange(half, dtype=F32) / half)
    ang = pos.astype(F32)[:, None] * inv[None, :]
    cos, sin = jnp.cos(ang), jnp.sin(ang)
    cos_h = jnp.concatenate([cos, cos], axis=-1)
    sin_h = jnp.concatenate([-sin, sin], axis=-1)
    return jnp.tile(cos_h, (1, 4)), jnp.tile(sin_h, (1, 4))


def _row(a, width=None):
    a = a.reshape(1, -1).astype(F32)
    if width is not None and a.shape[1] < width:
        a = jnp.pad(a, ((0, 0), (0, width - a.shape[1])))
    return a


def kernel(x_prompt, x_sample, state_gla, state_ret, state_delta, state_conv, c_prompt, c_sample, w_ada, b_ada, g_pre, g_post, w_in_ab, w_gla_gate_up, b_gla_gate, g_norm_gla, g_norm_ret, w_out_ab, w_in_c, conv_c, a_log_c, dt_bias_c, g_norm_c, w_out_c):
    nbp, t_p, _ = x_prompt.shape
    nbs, t_s, _ = x_sample.shape
    assert w_ada.shape[0] == 2 and w_in_ab.shape[0] == 1

    mod = _ada_call(jnp.concatenate([c_prompt, c_sample], axis=0), w_ada, b_ada)
    mod_p = mod[:, :nbp, None, :]
    mod_s = mod[:, nbp:, None, :]

    w = w_in_ab[0]
    w_ab = jnp.concatenate([w[:, :1536], w[:, 1552:], w[:, 1536:1552],
                            jnp.zeros((D_MODEL, GLA_RANK_PAD - 16), F32)], axis=1).astype(BF16)
    wgu = jnp.pad(w_gla_gate_up[0], ((0, GLA_RANK_PAD - 16), (0, 0))).astype(BF16)
    lgam = jnp.repeat(jnp.log1p(-jnp.exp2(-5.0 - jnp.arange(4, dtype=F32))), DK_AB).reshape(1, 256)
    weights_ab = (_row(g_pre[0]), w_ab, wgu, _row(b_gla_gate[0]), lgam, _row(g_norm_gla[0]),
                  _row(g_norm_ret[0]), w_out_ab[0].astype(BF16), _row(g_post[0]))

    w = w_in_c[0]
    zpad = jnp.zeros((D_MODEL, 128 - GDN_HEADS), F32)
    w_c = jnp.concatenate([w[:, :4096], w[:, 4096:4104], zpad, w[:, 4104:4112], zpad], axis=1).astype(BF16)
    weights_c = (_row(g_pre[1]), w_c, conv_c[0], _row(a_log_c[0], 128), _row(dt_bias_c[0], 128),
                 _row(g_norm_c[0]), w_out_c[0].astype(BF16), _row(g_post[1]))

    cos_p, sin_p = _rope_tables(jnp.arange(t_p, dtype=jnp.int32))
    cos_s, sin_s = _rope_tables(PAST_LEN + jnp.arange(t_s, dtype=jnp.int32))
    bb_s = TILE // t_s
    cos_s, sin_s = jnp.tile(cos_s, (bb_s, 1)), jnp.tile(sin_s, (bb_s, 1))

    tt_p = 256
    xp, gla_p, ret_p = _ab_call(x_prompt, mod_p[0], cos_p, sin_p, None, weights_ab, bb=1, tt=tt_p)
    y_prompt, delta_p, conv_p = _c_call(xp, mod_p[1], None, weights_c, bb=1, tt=tt_p)

    xs, gla_s, ret_s = _ab_call(x_sample, mod_s[0], cos_s, sin_s,
                                (state_gla.reshape(state_gla.shape[1:]), state_ret.reshape(state_ret.shape[1:])),
                                weights_ab, bb=bb_s, tt=t_s)
    y_sample, delta_s, conv_s = _c_call(xs, mod_s[1],
                                        (state_delta.reshape(state_delta.shape[1:]),
                                         state_conv.reshape(state_conv.shape[1:])),
                                        weights_c, bb=bb_s, tt=t_s)
    return (y_prompt, y_sample, gla_p[None], ret_p[None], delta_p[None], conv_p[None],
            gla_s[None], ret_s[None], delta_s[None], conv_s[None])
```

```python
import functools
import math

import numpy as np
import jax
import jax.numpy as jnp
from jax import lax
from jax.experimental import pallas as pl
from jax.experimental.pallas import tpu as pltpu

F32 = jnp.float32
BF16 = jnp.bfloat16

D_MODEL = 1024
EPS = 1e-6
ROPE_BASE = 10000.0
PAST_LEN = 16384
GLA_TAU = 16.0
HEADS_AB = 8
DK_AB = 64
DV = 128
GLA_RANK_PAD = 128
N_AB = 3072 + GLA_RANK_PAD
GDN_HEADS = 8
CONV_W = 4
CONV_CH = 3072
N_C = 4096 + 2 * 128
TILE = 64
CARRY = 8
PREP_TILES = 4
VMEM_LIMIT_BYTES = 52 * 1024 * 1024


def _mm(a, b):
    return jnp.dot(a.astype(BF16), b.astype(BF16), preferred_element_type=F32)


def _mm_nt(a, b):
    return lax.dot_general(a.astype(BF16), b.astype(BF16), (((1,), (1,)), ((), ())),
                           preferred_element_type=F32)


def _mm_tn(a, b):
    return lax.dot_general(a.astype(BF16), b.astype(BF16), (((0,), (0,)), ((), ())),
                           preferred_element_type=F32)


def _bmm(a, b):
    return lax.dot_general(a.astype(BF16), b.astype(BF16), (((2,), (1,)), ((0,), (0,))),
                           preferred_element_type=F32)


def _bmm_nt(a, b):
    return lax.dot_general(a.astype(BF16), b.astype(BF16), (((2,), (2,)), ((0,), (0,))),
                           preferred_element_type=F32)


def _bmm_tn(a, b):
    return lax.dot_general(a.astype(BF16), b.astype(BF16), (((1,), (1,)), ((0,), (0,))),
                           preferred_element_type=F32)


def _split3(x):
    hi = x.astype(BF16)
    r = x - hi.astype(F32)
    mid = r.astype(BF16)
    lo = (r - mid.astype(F32)).astype(BF16)
    return hi, mid, lo


def _sigmoid(x):
    return 1.0 / (1.0 + jnp.exp(-x))


def _silu(x):
    return x * _sigmoid(x)


def _softplus(x):
    return jnp.maximum(x, 0.0) + jnp.log(1.0 + jnp.exp(-jnp.abs(x)))


def _masks(seglen, nh):
    r = lax.broadcasted_iota(jnp.int32, (nh, TILE, TILE), 1)
    c = lax.broadcasted_iota(jnp.int32, (nh, TILE, TILE), 2)
    sh = int(math.log2(seglen))
    same = (r >> sh) == (c >> sh)
    causal = same & (c <= r)
    eye = r == c
    cum = jnp.concatenate([causal[0].astype(F32), same[0].astype(F32)], axis=0).astype(BF16)
    return causal, eye, cum


def _level_masks(seglen, nh):
    r = lax.broadcasted_iota(jnp.int32, (nh, TILE, TILE), 1)
    c = lax.broadcasted_iota(jnp.int32, (nh, TILE, TILE), 2)
    masks = []
    for sh in range(int(math.log2(seglen))):
        masks.append(((r >> (sh + 1)) == (c >> (sh + 1))) & (((r >> sh) & 1) == 1) & (((c >> sh) & 1) == 0))
    return masks


def _seq_index(tile, seg, seglen, bb):
    return 0 if bb == 1 else tile * (TILE // seglen) + seg


def _prefix_and_total(cum, g):
    hi, mid, lo = _split3(g)
    r = (jnp.dot(cum, hi, preferred_element_type=F32) + jnp.dot(cum, mid, preferred_element_type=F32)
         + jnp.dot(cum, lo, preferred_element_type=F32))
    return r[:TILE], r[TILE:]


def _row_to_col(eye, row):
    return jnp.sum(jnp.where(eye, jnp.broadcast_to(row, eye.shape), 0.0), axis=2, keepdims=True)


def _col_to_row(eye, col):
    return jnp.sum(jnp.where(eye, jnp.broadcast_to(col, eye.shape), 0.0), axis=1, keepdims=True)


def _prenorm(x_ref, mod_ref, gpre_ref, h_ref, rows):
    x3 = x_ref[...]
    shift = mod_ref[:, :, 0:D_MODEL]
    scale = mod_ref[:, :, D_MODEL:2 * D_MODEL]
    xn = x3 * lax.rsqrt(jnp.mean(x3 * x3, axis=-1, keepdims=True) + EPS) * gpre_ref[...]
    h3 = xn * (1.0 + scale) + shift
    h_ref[...] = h3.reshape(rows, D_MODEL).astype(BF16)


def _postnorm(x_ref, mod_ref, gpost_ref, y, y_ref, bb, tt):
    gate = mod_ref[:, :, 2 * D_MODEL:3 * D_MODEL]
    y3 = y.reshape(bb, tt, D_MODEL)
    yn = y3 * lax.rsqrt(jnp.mean(y3 * y3, axis=-1, keepdims=True) + EPS) * gpost_ref[...]
    y_ref[...] = x_ref[...] + gate * yn


def _ada_kernel(c_ref, w_ref, b_ref, o_ref):
    o_ref[0] = _mm(_silu(c_ref[...]), w_ref[0]) + b_ref[0]


def _ada_call(c_all, w_ada, b_ada):
    n = c_all.shape[0]
    depth = w_ada.shape[0]
    nj = 3
    return pl.pallas_call(
        _ada_kernel,
        grid=(depth, nj),
        in_specs=[
            pl.BlockSpec((n, D_MODEL), lambda l, j: (0, 0)),
            pl.BlockSpec((1, D_MODEL, D_MODEL), lambda l, j: (l, 0, j)),
            pl.BlockSpec((1, 1, D_MODEL), lambda l, j: (l, 0, j)),
        ],
        out_specs=pl.BlockSpec((1, n, D_MODEL), lambda l, j: (l, 0, j)),
        out_shape=jax.ShapeDtypeStruct((depth, n, 3 * D_MODEL), F32),
        compiler_params=pltpu.CompilerParams(dimension_semantics=("arbitrary", "arbitrary")),
        name="adaln_mod",
    )(c_all, w_ada, b_ada.reshape(depth, 1, 3 * D_MODEL))


def _ab_kernel(*refs, bb, tt, seglen, has_state):
    if has_state:
        (x_ref, mod_ref, cos_ref, sin_ref, sg_ref, sr_ref, gpre_ref, win_ref, wgu_ref, bgate_ref, lgam_ref,
         ggla_ref, gret_ref, wout_ref, gpost_ref, y_ref, glao_ref, reto_ref, h_ref, proj_ref, og_ref,
         qd_scr, ko_scr, dec_scr, oi_scr) = refs
        s_in = (sg_ref, sr_ref)
    else:
        (x_ref, mod_ref, cos_ref, sin_ref, gpre_ref, win_ref, wgu_ref, bgate_ref, lgam_ref,
         ggla_ref, gret_ref, wout_ref, gpost_ref, y_ref, glao_ref, reto_ref, h_ref, proj_ref, og_ref,
         qd_scr, ko_scr, dec_scr, oi_scr) = refs
        s_in = (glao_ref, reto_ref)
    s_out = (glao_ref, reto_ref)
    rows = bb * tt
    nseg = TILE // seglen
    n_tiles = rows // TILE
    t_id = pl.program_id(1)

    if not has_state:
        @pl.when(t_id == 0)
        def _():
            glao_ref[...] = jnp.zeros_like(glao_ref)
            reto_ref[...] = jnp.zeros_like(reto_ref)

    _prenorm(x_ref, mod_ref, gpre_ref, h_ref, rows)
    for c0 in range(0, N_AB, 640):
        proj_ref[:, c0:c0 + 640] = jnp.dot(h_ref[...], win_ref[:, c0:c0 + 640], preferred_element_type=F32)

    nh = HEADS_AB
    tb = min(PREP_TILES, n_tiles)
    causal, eye, cum = _masks(seglen, nh * tb)
    eye = eye[:nh]
    lane = lax.broadcasted_iota(jnp.int32, (TILE, 256), 1)
    first_half = (lane & 63) < 32
    gnorm = (ggla_ref[...], gret_ref[...])

    def heads(parts, width):
        return jnp.stack([p[:, h * width:(h + 1) * width] for p in parts for h in range(4)])

    def prep_body(j, carry):
        qd, ki, v = [], [], []
        for t in range(tb):
            rs = pl.ds(pl.multiple_of((j * tb + t) * TILE, TILE), TILE)

            def pj(a, b):
                return proj_ref[rs, a:b]

            def rot(x):
                swapped = jnp.where(first_half, pltpu.roll(x, 256 - 32, 1), pltpu.roll(x, 32, 1))
                return x * cos_ref[rs, :] + swapped * sin_ref[rs, :]

            log_a = _mm(pj(3072, 3200), wgu_ref[...]) + bgate_ref[...]
            log_a = (jnp.minimum(log_a, 0.0) - jnp.log(1.0 + jnp.exp(-jnp.abs(log_a)))) / GLA_TAU
            q_parts = (pj(0, 256) * (DK_AB ** -0.5), rot(pj(1536, 1792)) * (DK_AB ** -0.5))
            k_parts = (pj(256, 512), rot(pj(1792, 2048)))
            lg_parts = (log_a, jnp.broadcast_to(lgam_ref[...], (TILE, 256)))
            q_dec, k_inv = [], []
            for part in range(2):
                b, btot = _prefix_and_total(cum, lg_parts[part])
                q_dec.append(q_parts[part] * jnp.exp(b))
                k_inv.append(k_parts[part] * jnp.exp(-b))
                cs = slice(part * 256, (part + 1) * 256)
                qd_scr[rs, cs] = q_dec[part]
                ko_scr[rs, cs] = k_parts[part] * jnp.exp(btot - b)
                dec_scr[rs, cs] = jnp.exp(btot)
            qd.append(heads(q_dec, DK_AB))
            ki.append(heads(k_inv, DK_AB))
            v.append(heads((pj(512, 1024), pj(2048, 2560)), DV))
        qd, ki, v = (x[0] if tb == 1 else jnp.concatenate(x, axis=0) for x in (qd, ki, v))
        scores = jnp.where(causal, _bmm_nt(qd, ki), 0.0)
        o_intra = _bmm(scores, v)
        for t in range(tb):
            oi_scr[j * tb + t] = o_intra[t * nh:(t + 1) * nh]
        return carry

    assert n_tiles % tb == 0
    lax.fori_loop(0, n_tiles // tb, prep_body, 0)

    def tile_body(i, carry):
        r0 = pl.multiple_of(i * TILE, TILE)
        rs = pl.ds(r0, TILE)
        qd = heads((qd_scr[rs, 0:256], qd_scr[rs, 256:512]), DK_AB)
        ko = heads((ko_scr[rs, 0:256], ko_scr[rs, 256:512]), DK_AB)
        v = heads((proj_ref[rs, 512:1024], proj_ref[rs, 2048:2560]), DV)
        z = heads((proj_ref[rs, 1024:1536], proj_ref[rs, 2560:3072]), DV)
        o_inter = []
        for s in range(nseg):
            sr = slice(s * seglen, (s + 1) * seglen)
            bidx = _seq_index(i, s, seglen, bb)
            st = jnp.concatenate([s_in[0][bidx], s_in[1][bidx]], axis=0)
            o_inter.append(_bmm(qd[:, sr], st))
            drow = pl.ds(r0 + s * seglen, 1)
            dec_row = heads((dec_scr[drow, 0:256], dec_scr[drow, 256:512]), DK_AB)
            st = st * _row_to_col(eye, dec_row) + _bmm_tn(ko[:, sr], v[:, sr])
            s_out[0][bidx] = st[:4]
            s_out[1][bidx] = st[4:]
        o = oi_scr[i] + (o_inter[0] if nseg == 1 else jnp.concatenate(o_inter, axis=1))
        o_g, o_r = o[:4], o[4:]
        on_g = o_g * lax.rsqrt(jnp.mean(o_g * o_g, axis=-1, keepdims=True) + EPS) * gnorm[0]
        dlt = o_r - jnp.mean(o_r, axis=-1, keepdims=True)
        on_r = dlt * lax.rsqrt(jnp.mean(dlt * dlt, axis=-1, keepdims=True) + EPS) * gnorm[1]
        og = (jnp.concatenate([on_g, on_r], axis=0) * _silu(z)).astype(BF16)
        for h in range(nh):
            og_ref[rs, h * DV:(h + 1) * DV] = og[h]
        return carry

    lax.fori_loop(0, n_tiles, tile_body, 0)
    y = jnp.dot(og_ref[...], wout_ref[...], preferred_element_type=F32)
    _postnorm(x_ref, mod_ref, gpost_ref, y, y_ref, bb, tt)


def _ab_call(x, mod, cos, sin, states, weights, *, bb, tt):
    nb_total, t_total, _ = x.shape
    has_state = states is not None
    seglen = min(TILE, tt)
    rows = bb * tt
    assert nb_total % bb == 0 and t_total % tt == 0 and rows % TILE == 0 and TILE % seglen == 0
    assert tt % 8 == 0 and (bb == 1 or tt == seglen)
    grid = (nb_total // bb, t_total // tt)
    assert not has_state or grid[1] == 1
    const2 = lambda b, t: (0, 0)
    st_spec = pl.BlockSpec((bb, 4, DK_AB, DV), lambda b, t: (b, 0, 0, 0))
    in_specs = [
        pl.BlockSpec((bb, tt, D_MODEL), lambda b, t: (b, t, 0)),
        pl.BlockSpec((bb, 1, 3 * D_MODEL), lambda b, t: (b, 0, 0)),
    ]
    if has_state:
        in_specs += [pl.BlockSpec((rows, 256), const2)] * 2 + [st_spec, st_spec]
    else:
        in_specs += [pl.BlockSpec((tt, 256), lambda b, t: (t, 0))] * 2
    in_specs += [pl.BlockSpec(w.shape, const2) for w in weights]
    args = (x, mod, cos, sin) + (tuple(states) if has_state else ()) + tuple(weights)
    st_shape = jax.ShapeDtypeStruct((nb_total, 4, DK_AB, DV), F32)
    return pl.pallas_call(
        functools.partial(_ab_kernel, bb=bb, tt=tt, seglen=seglen, has_state=has_state),
        grid=grid,
        in_specs=in_specs,
        out_specs=[pl.BlockSpec((bb, tt, D_MODEL), lambda b, t: (b, t, 0)), st_spec, st_spec],
        out_shape=[jax.ShapeDtypeStruct(x.shape, F32), st_shape, st_shape],
        scratch_shapes=[pltpu.VMEM((rows, D_MODEL), BF16), pltpu.VMEM((rows, N_AB), F32),
                        pltpu.VMEM((rows, D_MODEL), BF16)]
                       + [pltpu.VMEM((rows, HEADS_AB * DK_AB), F32)] * 3
                       + [pltpu.VMEM((rows // TILE, HEADS_AB, TILE, DV), F32)],
        compiler_params=pltpu.CompilerParams(dimension_semantics=("arbitrary", "arbitrary"),
                                             vmem_limit_bytes=VMEM_LIMIT_BYTES),
        name="gla_ret_layer_" + ("sample" if has_state else "prompt"),
    )(*args)


def _c_kernel(*refs, bb, tt, seglen, has_state):
    if has_state:
        (x_ref, mod_ref, sd_ref, sc_ref, gpre_ref, win_ref, convw_ref, alog_ref, dtb_ref, gn_ref, wout_ref,
         gpost_ref, y_ref, do_ref, co_ref, h_ref, cbuf_ref, qkv_ref, zab_ref, og_ref,
         u_scr, w_scr, qd_scr, ko_scr, at_scr, dec_scr) = refs
        s_in = sd_ref
    else:
        (x_ref, mod_ref, gpre_ref, win_ref, convw_ref, alog_ref, dtb_ref, gn_ref, wout_ref,
         gpost_ref, y_ref, do_ref, co_ref, h_ref, cbuf_ref, qkv_ref, zab_ref, og_ref,
         u_scr, w_scr, qd_scr, ko_scr, at_scr, dec_scr) = refs
        s_in = do_ref
    rows = bb * tt
    nseg = TILE // seglen
    n_tiles = rows // TILE
    t_id = pl.program_id(1)
    keep =slice(CARRY + tt - (CONV_W - 1), CARRY + tt)
    slot = slice(CARRY - (CONV_W - 1), CARRY)

    @pl.when(t_id > 0)
    def _():
        cbuf_ref[:, slot, :] = cbuf_ref[:, keep, :]

    @pl.when(t_id == 0)
    def _():
        if has_state:
            cbuf_ref[:, slot, :] = sc_ref[...]
        else:
            cbuf_ref[:, slot, :] = jnp.zeros((bb, CONV_W - 1, CONV_CH), F32)
            do_ref[...] = jnp.zeros_like(do_ref)

    _prenorm(x_ref, mod_ref, gpre_ref, h_ref, rows)
    for c0 in range(0, CONV_CH, 512):
        pr = jnp.dot(h_ref[...], win_ref[:, c0:c0 + 512], preferred_element_type=F32)
        cbuf_ref[:, CARRY:CARRY + tt, c0:c0 + 512] = pr.reshape(bb, tt, 512)
    for c0 in range(CONV_CH, N_C, 640):
        zab_ref[:, c0 - CONV_CH:c0 - CONV_CH + 640] = jnp.dot(h_ref[...], win_ref[:, c0:c0 + 640],
                                                             preferred_element_type=F32)
    co_ref[...] = cbuf_ref[:, keep, :]

    for c0 in range(0, CONV_CH, 512):
        acc = None
        for i in range(CONV_W):
            off = CARRY - (CONV_W - 1) + i
            term = cbuf_ref[:, off:off + tt, c0:c0 + 512] * convw_ref[i:i + 1, c0:c0 + 512]
            acc = term if acc is None else acc + term
        qkv_ref[:, c0:c0 + 512] = _silu(acc).reshape(rows, 512)

    nh = GDN_HEADS
    tb = min(PREP_TILES, n_tiles)
    causal, eye, cum = _masks(seglen, nh * tb)
    level_masks = _level_masks(seglen, nh * tb)
    eye_f = eye.astype(F32)
    neg_decay_rate = -jnp.exp(alog_ref[...])

    def prep_body(j, carry):
        q, k, v, beta, b_c, btot_c = [], [], [], [], [], []
        for t in range(tb):
            rs = pl.ds(pl.multiple_of((j * tb + t) * TILE, TILE), TILE)
            beta_all = _sigmoid(zab_ref[rs, 1024:1152])
            log_g = neg_decay_rate * _softplus(zab_ref[rs, 1152:1280] + dtb_ref[...])
            b_all, btot_all = _prefix_and_total(cum, log_g)
            for h in range(nh):
                q.append(qkv_ref[rs, h * DV:(h + 1) * DV])
                k.append(qkv_ref[rs, 1024 + h * DV:1024 + (h + 1) * DV])
                v.append(qkv_ref[rs, 2048 + h * DV:2048 + (h + 1) * DV])
                beta.append(beta_all[:, h:h + 1])
                b_c.append(b_all[:, h:h + 1])
                btot_c.append(btot_all[:, h:h + 1])
        q, k, v = jnp.stack(q), jnp.stack(k), jnp.stack(v)
        beta, b_c, btot_c = jnp.stack(beta), jnp.stack(b_c), jnp.stack(btot_c)
        qn = q * lax.rsqrt(jnp.sum(q * q, axis=-1, keepdims=True) + EPS) * (DV ** -0.5)
        kn = k * lax.rsqrt(jnp.sum(k * k, axis=-1, keepdims=True) + EPS)
        b_r = _col_to_row(eye, b_c)
        seg = jnp.where(causal, jnp.exp(jnp.where(causal, b_c - b_r, 0.0)), 0.0)
        k_beta = kn * beta
        v_beta = v * beta
        a_mat = _bmm_nt(k_beta, kn) * seg
        inv = eye_f - jnp.where(level_masks[0], a_mat, 0.0)
        for lm in level_masks[1:]:
            inv = inv - _bmm(_bmm(inv, jnp.where(lm, a_mat, 0.0)), inv)
        e_b = jnp.exp(b_c)
        sol = _bmm(inv, jnp.concatenate([v_beta, k_beta * e_b], axis=2))
        attn = _bmm_nt(qn, kn) * seg
        q_dec = qn * e_b
        k_out = kn * jnp.exp(btot_c - b_c)
        dec = jnp.broadcast_to(jnp.exp(btot_c), (nh * tb, TILE, DV))
        for t in range(tb):
            ti = j * tb + t
            hs = slice(t * nh, (t + 1) * nh)
            u_scr[ti] = sol[hs, :, :DV]
            w_scr[ti] = sol[hs, :, DV:]
            qd_scr[ti] = q_dec[hs]
            ko_scr[ti] = k_out[hs]
            at_scr[ti] = attn[hs]
            dec_scr[ti] = dec[hs]
        return carry

    assert n_tiles % tb == 0
    lax.fori_loop(0, n_tiles // tb, prep_body, 0)

    def tile_body(i, carry):
        rs = pl.ds(pl.multiple_of(i * TILE, TILE), TILE)
        u, w, q_dec, k_out, attn = u_scr[i], w_scr[i], qd_scr[i], ko_scr[i], at_scr[i]
        v_new, o_inter, states = [], [], []
        for s in range(nseg):
            sr = slice(s * seglen, (s + 1) * seglen)
            st = s_in[_seq_index(i, s, seglen, bb)]
            ws = _bmm(jnp.concatenate([w[:, sr], q_dec[:, sr]], axis=1), st)
            v_new.append(u[:, sr] - ws[:, :seglen])
            o_inter.append(ws[:, seglen:])
            states.append(st)
        v_new = v_new[0] if nseg == 1 else jnp.concatenate(v_new, axis=1)
        o_inter = o_inter[0] if nseg == 1 else jnp.concatenate(o_inter, axis=1)
        o = o_inter + _bmm(attn, v_new)
        for s in range(nseg):
            sr = slice(s * seglen, (s + 1) * seglen)
            dec = dec_scr[i, :, s * seglen:s * seglen + 1, :]
            do_ref[_seq_index(i, s, seglen, bb)] = states[s] * dec + _bmm_tn(k_out[:, sr], v_new[:, sr])
        on = o * lax.rsqrt(jnp.mean(o * o, axis=-1, keepdims=True) + EPS) * gn_ref[...]
        for h in range(nh):
            z = zab_ref[rs, h * DV:(h + 1) * DV]
            og_ref[rs, h * DV:(h + 1) * DV] = (on[h] * _silu(z)).astype(BF16)
        return carry

    lax.fori_loop(0, n_tiles, tile_body, 0)
    y = jnp.dot(og_ref[...], wout_ref[...], preferred_element_type=F32)
    _postnorm(x_ref, mod_ref, gpost_ref, y, y_ref, bb, tt)


def _c_call(x, mod, states, weights, *, bb, tt):
    nb_total, t_total, _ = x.shape
    has_state = states is not None
    seglen = min(TILE, tt)
    rows = bb * tt
    assert nb_total % bb == 0 and t_total % tt == 0 and rows % TILE == 0 and TILE % seglen == 0
    assert tt % 8 == 0 and tt >= CONV_W - 1 and (bb == 1 or tt == seglen)
    grid = (nb_total // bb, t_total // tt)
    assert not has_state or grid[1] == 1
    const2 = lambda b, t: (0, 0)
    sd_spec = pl.BlockSpec((bb, GDN_HEADS, DV, DV), lambda b, t: (b, 0, 0, 0))
    sc_spec = pl.BlockSpec((bb, CONV_W - 1, CONV_CH), lambda b, t: (b, 0, 0))
    in_specs = [
        pl.BlockSpec((bb, tt, D_MODEL), lambda b, t: (b, t, 0)),
        pl.BlockSpec((bb, 1, 3 * D_MODEL), lambda b, t: (b, 0, 0)),
    ]
    if has_state:
        in_specs += [sd_spec, sc_spec]
    in_specs += [pl.BlockSpec(w.shape, const2) for w in weights]
    args = (x, mod) + (tuple(states) if has_state else ()) + tuple(weights)
    return pl.pallas_call(
        functools.partial(_c_kernel, bb=bb, tt=tt, seglen=seglen, has_state=has_state),
        grid=grid,
        in_specs=in_specs,
        out_specs=[pl.BlockSpec((bb, tt, D_MODEL), lambda b, t: (b, t, 0)), sd_spec, sc_spec],
        out_shape=[jax.ShapeDtypeStruct(x.shape, F32),
                   jax.ShapeDtypeStruct((nb_total, GDN_HEADS, DV, DV), F32),
                   jax.ShapeDtypeStruct((nb_total, CONV_W - 1, CONV_CH), F32)],
        scratch_shapes=[pltpu.VMEM((rows, D_MODEL), BF16),
                        pltpu.VMEM((bb, CARRY + tt, CONV_CH), F32),
                        pltpu.VMEM((rows, CONV_CH), F32),
                        pltpu.VMEM((rows, N_C - CONV_CH), F32),
                        pltpu.VMEM((rows, D_MODEL), BF16)]
                       + [pltpu.VMEM((rows // TILE, GDN_HEADS, TILE, DV), F32)] * 4
                       + [pltpu.VMEM((rows // TILE, GDN_HEADS, TILE, TILE), F32),
                          pltpu.VMEM((rows // TILE, GDN_HEADS, TILE, DV), F32)],
        compiler_params=pltpu.CompilerParams(dimension_semantics=("arbitrary", "arbitrary"),
                                             vmem_limit_bytes=VMEM_LIMIT_BYTES),
        name="gdn_layer_" + ("sample" if has_state else "prompt"),
    )(*args)


def _rope_tables(pos):
    half = DK_AB // 2
    inv = ROPE_BASE ** (-jnp.arange(half, dtype=F32) / half)
    ang = pos.astype(F32)[:, None] * inv[None, :]
    cos, sin = jnp.cos(ang), jnp.sin(ang)
    cos_h = jnp.concatenate([cos, cos], axis=-1)
    sin_h = jnp.concatenate([-sin, sin], axis=-1)
    return jnp.tile(cos_h, (1, 4)), jnp.tile(sin_h, (1, 4))


def _row(a, width=None):
    a = a.reshape(1, -1).astype(F32)
    if width is not None and a.shape[1] < width:
        a = jnp.pad(a, ((0, 0), (0, width - a.shape[1])))
    return a


def kernel(x_prompt, x_sample, state_gla, state_ret, state_delta, state_conv, c_prompt, c_sample, w_ada, b_ada, g_pre, g_post, w_in_ab, w_gla_gate_up, b_gla_gate, g_norm_gla, g_norm_ret, w_out_ab, w_in_c, conv_c, a_log_c, dt_bias_c, g_norm_c, w_out_c):
    nbp, t_p, _ = x_prompt.shape
    nbs, t_s, _ = x_sample.shape
    assert w_ada.shape[0] == 2 and w_in_ab.shape[0] == 1

    mod = _ada_call(jnp.concatenate([c_prompt, c_sample], axis=0), w_ada, b_ada)
    mod_p = mod[:, :nbp, None, :]
    mod_s = mod[:, nbp:, None, :]

    w = w_in_ab[0]
    w_ab = jnp.concatenate([w[:, :1536], w[:, 1552:], w[:, 1536:1552],
                            jnp.zeros((D_MODEL, GLA_RANK_PAD - 16), F32)], axis=1).astype(BF16)
    wgu = jnp.pad(w_gla_gate_up[0], ((0, GLA_RANK_PAD - 16), (0, 0))).astype(BF16)
    lgam = jnp.repeat(jnp.log1p(-jnp.exp2(-5.0 - jnp.arange(4, dtype=F32))), DK_AB).reshape(1, 256)
    weights_ab = (_row(g_pre[0]), w_ab, wgu, _row(b_gla_gate[0]), lgam, _row(g_norm_gla[0]),
                  _row(g_norm_ret[0]), w_out_ab[0].astype(BF16), _row(g_post[0]))

    w = w_in_c[0]
    zpad = jnp.zeros((D_MODEL, 128 - GDN_HEADS), F32)
    w_c = jnp.concatenate([w[:, :4096], w[:, 4096:4104], zpad, w[:, 4104:4112], zpad], axis=1).astype(BF16)
    weights_c = (_row(g_pre[1]), w_c, conv_c[0], _row(a_log_c[0], 128), _row(dt_bias_c[0], 128),
                 _row(g_norm_c[0]), w_out_c[0].astype(BF16), _row(g_post[1]))

    cos_p, sin_p = _rope_tables(jnp.arange(t_p, dtype=jnp.int32))
    cos_s, sin_s = _rope_tables(PAST_LEN + jnp.arange(t_s, dtype=jnp.int32))
    bb_s = TILE // t_s
    cos_s, sin_s = jnp.tile(cos_s, (bb_s, 1)), jnp.tile(sin_s, (bb_s, 1))

    tt_p = 256
    xp, gla_p, ret_p = _ab_call(x_prompt, mod_p[0], cos_p, sin_p, None, weights_ab, bb=1, tt=tt_p)
    y_prompt, delta_p, conv_p = _c_call(xp, mod_p[1], None, weights_c, bb=1, tt=tt_p)

    xs, gla_s, ret_s = _ab_call(x_sample, mod_s[0], cos_s, sin_s,
                                (state_gla.reshape(state_gla.shape[1:]), state_ret.reshape(state_ret.shape[1:])),
                                weights_ab, bb=bb_s, tt=t_s)
    y_sample, delta_s, conv_s = _c_call(xs, mod_s[1],
                                        (state_delta.reshape(state_delta.shape[1:]),
                                         state_conv.reshape(state_conv.shape[1:])),
                                        weights_c, bb=bb_s, tt=t_s)
    return (y_prompt, y_sample, gla_p[None], ret_p[None], delta_p[None], conv_p[None],
            gla_s[None], ret_s[None], delta_s[None], conv_s[None])
```

```python
import functools
import math

import numpy as np
import jax
import jax.numpy as jnp
from jax import lax
from jax.experimental import pallas as pl
from jax.experimental.pallas import tpu as pltpu

F32 = jnp.float32
BF16 = jnp.bfloat16

D_MODEL = 1024
EPS = 1e-6
ROPE_BASE = 10000.0
PAST_LEN = 16384
GLA_TAU = 16.0
HEADS_AB = 8
DK_AB = 64
DV = 128
GLA_RANK_PAD = 128
N_AB = 3072 + GLA_RANK_PAD
GDN_HEADS = 8
CONV_W = 4
CONV_CH = 3072
N_C = 4096 + 128
TILE = 64
CARRY = 8
PREP_TILES_AB = 4
PREP_TILES_C = 2
VMEM_LIMIT_BYTES = 52 * 1024 * 1024


def _mm(a, b):
    return jnp.dot(a.astype(BF16), b.astype(BF16), preferred_element_type=F32)


def _mm_nt(a, b):
    return lax.dot_general(a.astype(BF16), b.astype(BF16), (((1,), (1,)), ((), ())),
                           preferred_element_type=F32)


def _mm_tn(a, b):
    return lax.dot_general(a.astype(BF16), b.astype(BF16), (((0,), (0,)), ((), ())),
                           preferred_element_type=F32)


def _bmm(a, b):
    return lax.dot_general(a.astype(BF16), b.astype(BF16), (((2,), (1,)), ((0,), (0,))),
                           preferred_element_type=F32)


def _bmm_nt(a, b):
    return lax.dot_general(a.astype(BF16), b.astype(BF16), (((2,), (2,)), ((0,), (0,))),
                           preferred_element_type=F32)


def _bmm_tn(a, b):
    return lax.dot_general(a.astype(BF16), b.astype(BF16), (((1,), (1,)), ((0,), (0,))),
                           preferred_element_type=F32)


def _split3(x):
    hi = x.astype(BF16)
    r = x - hi.astype(F32)
    mid = r.astype(BF16)
    lo = (r - mid.astype(F32)).astype(BF16)
    return hi, mid, lo


def _sigmoid(x):
    return 1.0 / (1.0 + jnp.exp(-x))


def _silu(x):
    return x * _sigmoid(x)


def _softplus(x):
    return jnp.maximum(x, 0.0) + jnp.log(1.0 + jnp.exp(-jnp.abs(x)))


def _masks(seglen, nh):
    r = lax.broadcasted_iota(jnp.int32, (nh, TILE, TILE), 1)
    c = lax.broadcasted_iota(jnp.int32, (nh, TILE, TILE), 2)
    sh = int(math.log2(seglen))
    same = (r >> sh) == (c >> sh)
    causal = same & (c <= r)
    eye = r == c
    cum = jnp.concatenate([causal[0].astype(F32), same[0].astype(F32)], axis=0).astype(BF16)
    return causal, eye, cum


def _level_masks(seglen, nh):
    r = lax.broadcasted_iota(jnp.int32, (nh, TILE, TILE), 1)
    c = lax.broadcasted_iota(jnp.int32, (nh, TILE, TILE), 2)
    masks = []
    for sh in range(int(math.log2(seglen))):
        masks.append(((r >> (sh + 1)) == (c >> (sh + 1))) & (((r >> sh) & 1) == 1) & (((c >> sh) & 1) == 0))
    return masks


def _seq_index(tile, seg, seglen, bb):
    return 0 if bb == 1 else tile * (TILE // seglen) + seg


def _prefix_and_total(cum, g):
    hi, mid, lo = _split3(g)
    r = (jnp.dot(cum, hi, preferred_element_type=F32) + jnp.dot(cum, mid, preferred_element_type=F32)
         + jnp.dot(cum, lo, preferred_element_type=F32))
    return r[:TILE], r[TILE:]


def _row_to_col(eye, row):
    return jnp.sum(jnp.where(eye, jnp.broadcast_to(row, eye.shape), 0.0), axis=2, keepdims=True)


def _col_to_row(eye, col):
    return jnp.sum(jnp.where(eye, jnp.broadcast_to(col, eye.shape), 0.0), axis=1, keepdims=True)


def _prenorm(x_ref, mod_ref, gpre_ref, h_ref, rows):
    x3 = x_ref[...]
    shift = mod_ref[:, :, 0:D_MODEL]
    scale = mod_ref[:, :, D_MODEL:2 * D_MODEL]
    xn = x3 * lax.rsqrt(jnp.mean(x3 * x3, axis=-1, keepdims=True) + EPS) * gpre_ref[...]
    h3 = xn * (1.0 + scale) + shift
    h_ref[...] = h3.reshape(rows, D_MODEL).astype(BF16)


def _postnorm(x_ref, mod_ref, gpost_ref, y, y_ref, bb, tt):
    gate = mod_ref[:, :, 2 * D_MODEL:3 * D_MODEL]
    y3 = y.reshape(bb, tt, D_MODEL)
    yn = y3 * lax.rsqrt(jnp.mean(y3 * y3, axis=-1, keepdims=True) + EPS) * gpost_ref[...]
    y_ref[...] = x_ref[...] + gate * yn


def _ada_kernel(c_ref, w_ref, b_ref, o_ref):
    o_ref[0] = _mm(_silu(c_ref[...]), w_ref[0]) + b_ref[0]


def _ada_call(c_all, w_ada, b_ada):
    n = c_all.shape[0]
    depth = w_ada.shape[0]
    nj = 3
    return pl.pallas_call(
        _ada_kernel,
        grid=(depth, nj),
        in_specs=[
            pl.BlockSpec((n, D_MODEL), lambda l, j: (0, 0)),
            pl.BlockSpec((1, D_MODEL, D_MODEL), lambda l, j: (l, 0, j)),
            pl.BlockSpec((1, 1, D_MODEL), lambda l, j: (l, 0, j)),
        ],
        out_specs=pl.BlockSpec((1, n, D_MODEL), lambda l, j: (l, 0, j)),
        out_shape=jax.ShapeDtypeStruct((depth, n, 3 * D_MODEL), F32),
        compiler_params=pltpu.CompilerParams(dimension_semantics=("arbitrary", "arbitrary")),
        name="adaln_mod",
    )(c_all, w_ada, b_ada.reshape(depth, 1, 3 * D_MODEL))


def _ab_kernel(*refs, bb, tt, seglen, has_state):
    if has_state:
        (x_ref, mod_ref, cos_ref, sin_ref, sg_ref, sr_ref, gpre_ref, win_ref, wgu_ref, bgate_ref, lgam_ref,
         ggla_ref, gret_ref, wout_ref, gpost_ref, y_ref, glao_ref, reto_ref, h_ref, proj_ref, og_ref,
         qd_scr, ko_scr, dec_scr, oi_scr) = refs
        s_in = (sg_ref, sr_ref)
    else:
        (x_ref, mod_ref, cos_ref, sin_ref, gpre_ref, win_ref, wgu_ref, bgate_ref, lgam_ref,
         ggla_ref, gret_ref, wout_ref, gpost_ref, y_ref, glao_ref, reto_ref, h_ref, proj_ref, og_ref,
         qd_scr, ko_scr, dec_scr, oi_scr) = refs
        s_in = (glao_ref, reto_ref)
    s_out = (glao_ref, reto_ref)
    rows = bb * tt
    nseg = TILE // seglen
    n_tiles = rows // TILE
    t_id = pl.program_id(1)

    if not has_state:
        @pl.when(t_id == 0)
        def _():
            glao_ref[...] = jnp.zeros_like(glao_ref)
            reto_ref[...] = jnp.zeros_like(reto_ref)

    _prenorm(x_ref, mod_ref, gpre_ref, h_ref, rows)
    for c0 in range(0, N_AB, 640):
        proj_ref[:, c0:c0 + 640] = jnp.dot(h_ref[...], win_ref[:, c0:c0 + 640], preferred_element_type=F32)

    nh = HEADS_AB
    tb = min(PREP_TILES_AB, n_tiles)
    causal, eye, cum = _masks(seglen, nh * tb)
    eye = eye[:nh]
    lane = lax.broadcasted_iota(jnp.int32, (TILE, 256), 1)
    first_half = (lane & 63) < 32
    gnorm = (ggla_ref[...], gret_ref[...])

    def heads(parts, width):
        return jnp.stack([p[:, h * width:(h + 1) * width] for p in parts for h in range(4)])

    def prep_body(j):
        qd, ki, v = [], [], []
        for t in range(tb):
            rs = pl.ds((j * tb + t) * TILE, TILE)

            def pj(a, b):
                return proj_ref[rs, a:b]

            def rot(x):
                swapped = jnp.where(first_half, pltpu.roll(x, 256 - 32, 1), pltpu.roll(x, 32, 1))
                return x * cos_ref[rs, :] + swapped * sin_ref[rs, :]

            log_a = _mm(pj(3072, 3200), wgu_ref[...]) + bgate_ref[...]
            log_a = (jnp.minimum(log_a, 0.0) - jnp.log(1.0 + jnp.exp(-jnp.abs(log_a)))) / GLA_TAU
            q_parts = (pj(0, 256) * (DK_AB ** -0.5), rot(pj(1536, 1792)) * (DK_AB ** -0.5))
            k_parts = (pj(256, 512), rot(pj(1792, 2048)))
            lg_parts = (log_a, jnp.broadcast_to(lgam_ref[...], (TILE, 256)))
            q_dec, k_inv = [], []
            for part in range(2):
                b, btot = _prefix_and_total(cum, lg_parts[part])
                q_dec.append(q_parts[part] * jnp.exp(b))
                k_inv.append(k_parts[part] * jnp.exp(-b))
                cs = slice(part * 256, (part + 1) * 256)
                qd_scr[rs, cs] = q_dec[part]
                ko_scr[rs, cs] = k_parts[part] * jnp.exp(btot - b)
                dec_scr[rs, cs] = jnp.exp(btot)
            qd.append(heads(q_dec, DK_AB))
            ki.append(heads(k_inv, DK_AB))
            v.append(heads((pj(512, 1024), pj(2048, 2560)), DV))
        qd, ki, v = (x[0] if tb == 1 else jnp.concatenate(x, axis=0) for x in (qd, ki, v))
        scores = jnp.where(causal, _bmm_nt(qd, ki), 0.0)
        o_intra = _bmm(scores, v)
        for t in range(tb):
            oi_scr[j * tb + t] = o_intra[t * nh:(t + 1) * nh]

    assert n_tiles % tb == 0
    for j in range(n_tiles // tb):
        prep_body(j)

    def tile_body(i):
        r0 = i * TILE
        rs = pl.ds(r0, TILE)
        qd = heads((qd_scr[rs, 0:256], qd_scr[rs, 256:512]), DK_AB)
        ko = heads((ko_scr[rs, 0:256], ko_scr[rs, 256:512]), DK_AB)
        v = heads((proj_ref[rs, 512:1024], proj_ref[rs, 2048:2560]), DV)
        z = heads((proj_ref[rs, 1024:1536], proj_ref[rs, 2560:3072]), DV)
        o_inter = []
        for s in range(nseg):
            sr = slice(s * seglen, (s + 1) * seglen)
            bidx = _seq_index(i, s, seglen, bb)
            st = jnp.concatenate([s_in[0][bidx], s_in[1][bidx]], axis=0)
            o_inter.append(_bmm(qd[:, sr], st))
            drow = pl.ds(r0 + s * seglen, 1)
            dec_row = heads((dec_scr[drow, 0:256], dec_scr[drow, 256:512]), DK_AB)
            st = st * _row_to_col(eye, dec_row) + _bmm_tn(ko[:, sr], v[:, sr])
            s_out[0][bidx] = st[:4]
            s_out[1][bidx] = st[4:]
        o = oi_scr[i] + (o_inter[0] if nseg == 1 else jnp.concatenate(o_inter, axis=1))
        o_g, o_r = o[:4], o[4:]
        on_g = o_g * lax.rsqrt(jnp.mean(o_g * o_g, axis=-1, keepdims=True) + EPS) * gnorm[0]
        dlt = o_r - jnp.mean(o_r, axis=-1, keepdims=True)
        on_r = dlt * lax.rsqrt(jnp.mean(dlt * dlt, axis=-1, keepdims=True) + EPS) * gnorm[1]
        og = (jnp.concatenate([on_g, on_r], axis=0) * _silu(z)).astype(BF16)
        for h in range(nh):
            og_ref[rs, h * DV:(h + 1) * DV] = og[h]

    for i in range(n_tiles):
        tile_body(i)
    y = jnp.dot(og_ref[...], wout_ref[...], preferred_element_type=F32)
    _postnorm(x_ref, mod_ref, gpost_ref, y, y_ref, bb, tt)


def _ab_call(x, mod, cos, sin, states, weights, *, bb, tt):
    nb_total, t_total, _ = x.shape
    has_state = states is not None
    seglen = min(TILE, tt)
    rows = bb * tt
    assert nb_total % bb == 0 and t_total % tt == 0 and rows % TILE == 0 and TILE % seglen == 0
    assert tt % 8 == 0 and (bb == 1 or tt == seglen)
    grid = (nb_total // bb, t_total // tt)
    assert not has_state or grid[1] == 1
    const2 = lambda b, t: (0, 0)
    st_spec = pl.BlockSpec((bb, 4, DK_AB, DV), lambda b, t: (b, 0, 0, 0))
    in_specs = [
        pl.BlockSpec((bb, tt, D_MODEL), lambda b, t: (b, t, 0)),
        pl.BlockSpec((bb, 1, 3 * D_MODEL), lambda b, t: (b, 0, 0)),
    ]
    if has_state:
        in_specs += [pl.BlockSpec((rows, 256), const2)] * 2 + [st_spec, st_spec]
    else:
        in_specs += [pl.BlockSpec((tt, 256), lambda b, t: (t, 0))] * 2
    in_specs += [pl.BlockSpec(w.shape, const2) for w in weights]
    args = (x, mod, cos, sin) + (tuple(states) if has_state else ()) + tuple(weights)
    st_shape = jax.ShapeDtypeStruct((nb_total, 4, DK_AB, DV), F32)
    return pl.pallas_call(
        functools.partial(_ab_kernel, bb=bb, tt=tt, seglen=seglen, has_state=has_state),
        grid=grid,
        in_specs=in_specs,
        out_specs=[pl.BlockSpec((bb, tt, D_MODEL), lambda b, t: (b, t, 0)), st_spec, st_spec],
        out_shape=[jax.ShapeDtypeStruct(x.shape, F32), st_shape, st_shape],
        scratch_shapes=[pltpu.VMEM((rows, D_MODEL), BF16), pltpu.VMEM((rows, N_AB), F32),
                        pltpu.VMEM((rows, D_MODEL), BF16)]
                       + [pltpu.VMEM((rows, HEADS_AB * DK_AB), F32)] * 3
                       + [pltpu.VMEM((rows // TILE, HEADS_AB, TILE, DV), F32)],
        compiler_params=pltpu.CompilerParams(dimension_semantics=("arbitrary", "arbitrary"),
                                             vmem_limit_bytes=VMEM_LIMIT_BYTES),
        name="gla_ret_layer_" + ("sample" if has_state else "prompt"),
    )(*args)


def _c_kernel(*refs, bb, tt, seglen, has_state):
    if has_state:
        (x_ref, mod_ref, sd_ref, sc_ref, gpre_ref, win_ref, convw_ref, alog_ref, dtb_ref, gn_ref, wout_ref,
         gpost_ref, y_ref, do_ref, co_ref, h_ref, cbuf_ref, qkv_ref, zab_ref, og_ref,
         u_scr, w_scr, qd_scr, ko_scr, at_scr, dec_scr) = refs
        s_in = sd_ref
    else:
        (x_ref, mod_ref, gpre_ref, win_ref, convw_ref, alog_ref, dtb_ref, gn_ref, wout_ref,
         gpost_ref, y_ref, do_ref, co_ref, h_ref, cbuf_ref, qkv_ref, zab_ref, og_ref,
         u_scr, w_scr, qd_scr, ko_scr, at_scr, dec_scr) = refs
        s_in = do_ref
    rows = bb * tt
    nseg = TILE // seglen
    n_tiles = rows // TILE
    t_id = pl.program_id(1)
    keep =slice(CARRY + tt - (CONV_W - 1), CARRY + tt)
    slot = slice(CARRY - (CONV_W - 1), CARRY)

    @pl.when(t_id > 0)
    def _():
        cbuf_ref[:, slot, :] = cbuf_ref[:, keep, :]

    @pl.when(t_id == 0)
    def _():
        cbuf_ref[:, 0:CARRY, :] = jnp.zeros((bb, CARRY, CONV_CH), F32)
        if has_state:
            cbuf_ref[:, slot, :] = sc_ref[...]
        else:
            do_ref[...] = jnp.zeros_like(do_ref)

    _prenorm(x_ref, mod_ref, gpre_ref, h_ref, rows)
    for c0 in range(0, CONV_CH, 512):
        pr = jnp.dot(h_ref[...], win_ref[:, c0:c0 + 512], preferred_element_type=F32)
        cbuf_ref[:, CARRY:CARRY + tt, c0:c0 + 512] = pr.reshape(bb, tt, 512)
    for c0, c1 in ((CONV_CH, CONV_CH + 640), (CONV_CH + 640, N_C)):
        zab_ref[:, c0 - CONV_CH:c1 - CONV_CH] = jnp.dot(h_ref[...], win_ref[:, c0:c1], preferred_element_type=F32)
    co_ref[...] = cbuf_ref[:, keep, :]

    for c0 in range(0, CONV_CH, 512):
        xs = cbuf_ref[:, :, c0:c0 + 512]
        acc = None
        for i in range(CONV_W):
            back = CONV_W - 1 - i
            shifted = xs if back == 0 else pltpu.roll(xs, back, 1)
            term = shifted[:, CARRY:, :] * convw_ref[i:i + 1, c0:c0 + 512]
            acc = term if acc is None else acc + term
        qkv_ref[:, c0:c0 + 512] = _silu(acc).reshape(rows, 512)

    nh = GDN_HEADS
    tb = min(PREP_TILES_C, n_tiles)
    causal, eye, cum = _masks(seglen, nh * tb)
    level_masks = _level_masks(seglen, nh * tb)
    eye_f = eye.astype(F32)
    neg_decay_rate = -jnp.exp(alog_ref[...])

    def prep_body(j):
        q, k, v, beta, b_c, btot_c = [], [], [], [], [], []
        for t in range(tb):
            rs = pl.ds((j * tb + t) * TILE, TILE)
            ba = zab_ref[rs, 1024:1152]
            beta_all = _sigmoid(ba)
            log_g = neg_decay_rate * _softplus(ba + dtb_ref[...])
            b_all, btot_all = _prefix_and_total(cum, log_g)
            for h in range(nh):
                q.append(qkv_ref[rs, h * DV:(h + 1) * DV])
                k.append(qkv_ref[rs, 1024 + h * DV:1024 + (h + 1) * DV])
                v.append(qkv_ref[rs, 2048 + h * DV:2048 + (h + 1) * DV])
                beta.append(beta_all[:, h:h + 1])
                b_c.append(b_all[:, nh + h:nh + h + 1])
                btot_c.append(btot_all[:, nh + h:nh + h + 1])
        q, k, v = jnp.stack(q), jnp.stack(k), jnp.stack(v)
        beta, b_c, btot_c = jnp.stack(beta), jnp.stack(b_c), jnp.stack(btot_c)
        qn = q * lax.rsqrt(jnp.sum(q * q, axis=-1, keepdims=True) + EPS) * (DV ** -0.5)
        kn = k * lax.rsqrt(jnp.sum(k * k, axis=-1, keepdims=True) + EPS)
        b_r = _col_to_row(eye, b_c)
        seg = jnp.where(causal, jnp.exp(jnp.where(causal, b_c - b_r, 0.0)), 0.0)
        k_beta = kn * beta
        v_beta = v * beta
        a_mat = _bmm_nt(k_beta, kn) * seg
        inv = eye_f - jnp.where(level_masks[0], a_mat, 0.0)
        for lm in level_masks[1:]:
            inv = inv - _bmm(_bmm(inv, jnp.where(lm, a_mat, 0.0)), inv)
        e_b = jnp.exp(b_c)
        sol = _bmm(inv, jnp.concatenate([v_beta, k_beta * e_b], axis=2))
        attn = _bmm_nt(qn, kn) * seg
        q_dec = qn * e_b
        k_out = kn * jnp.exp(btot_c - b_c)
        dec = [jnp.exp(btot_c[:, s * seglen:s * seglen + 1, :]) for s in range(nseg)]
        dec = jnp.broadcast_to(dec[0] if nseg == 1 else jnp.concatenate(dec, axis=1), (nh * tb, nseg, DV))
        for t in range(tb):
            ti = j * tb + t
            hs = slice(t * nh, (t + 1) * nh)
            u_scr[ti] = sol[hs, :, :DV]
            w_scr[ti] = sol[hs, :, DV:]
            qd_scr[ti] = q_dec[hs]
            ko_scr[ti] = k_out[hs]
            at_scr[ti] = attn[hs]
            dec_scr[ti] = dec[hs]

    assert n_tiles % tb == 0
    for j in range(n_tiles // tb):
        prep_body(j)

    def tile_body(i):
        rs = pl.ds(i * TILE, TILE)
        u, w, q_dec, k_out, attn = u_scr[i], w_scr[i], qd_scr[i], ko_scr[i], at_scr[i]
        v_new, o_inter, states = [], [], []
        for s in range(nseg):
            sr = slice(s * seglen, (s + 1) * seglen)
            st = s_in[_seq_index(i, s, seglen, bb)]
            ws = _bmm(jnp.concatenate([w[:, sr], q_dec[:, sr]], axis=1), st)
            v_new.append(u[:, sr] - ws[:, :seglen])
            o_inter.append(ws[:, seglen:])
            states.append(st)
        v_new = v_new[0] if nseg == 1 else jnp.concatenate(v_new, axis=1)
        o_inter = o_inter[0] if nseg == 1 else jnp.concatenate(o_inter, axis=1)
        o = o_inter + _bmm(attn, v_new)
        for s in range(nseg):
            sr = slice(s * seglen, (s + 1) * seglen)
            dec = dec_scr[i, :, s:s + 1, :]
            do_ref[_seq_index(i, s, seglen, bb)] = states[s] * dec + _bmm_tn(k_out[:, sr], v_new[:, sr])
        on = o * lax.rsqrt(jnp.mean(o * o, axis=-1, keepdims=True) + EPS) * gn_ref[...]
        for h in range(nh):
            z = zab_ref[rs, h * DV:(h + 1) * DV]
            og_ref[rs, h * DV:(h + 1) * DV] = (on[h] * _silu(z)).astype(BF16)

    for i in range(n_tiles):
        tile_body(i)
    y = jnp.dot(og_ref[...], wout_ref[...], preferred_element_type=F32)
    _postnorm(x_ref, mod_ref, gpost_ref, y, y_ref, bb, tt)


def _c_call(x, mod, states, weights, *, bb, tt):
    nb_total, t_total, _ = x.shape
    has_state = states is not None
    seglen = min(TILE, tt)
    rows = bb * tt
    assert nb_total % bb == 0 and t_total % tt == 0 and rows % TILE == 0 and TILE % seglen == 0
    assert tt % 8 == 0 and tt >= CONV_W - 1 and (bb == 1 or tt == seglen)
    grid = (nb_total // bb, t_total // tt)
    assert not has_state or grid[1] == 1
    const2 = lambda b, t: (0, 0)
    sd_spec = pl.BlockSpec((bb, GDN_HEADS, DV, DV), lambda b, t: (b, 0, 0, 0))
    sc_spec = pl.BlockSpec((bb, CONV_W - 1, CONV_CH), lambda b, t: (b, 0, 0))
    in_specs = [
        pl.BlockSpec((bb, tt, D_MODEL), lambda b, t: (b, t, 0)),
        pl.BlockSpec((bb, 1, 3 * D_MODEL), lambda b, t: (b, 0, 0)),
    ]
    if has_state:
        in_specs += [sd_spec, sc_spec]
    in_specs += [pl.BlockSpec(w.shape, const2) for w in weights]
    args = (x, mod) + (tuple(states) if has_state else ()) + tuple(weights)
    return pl.pallas_call(
        functools.partial(_c_kernel, bb=bb, tt=tt, seglen=seglen, has_state=has_state),
        grid=grid,
        in_specs=in_specs,
        out_specs=[pl.BlockSpec((bb, tt, D_MODEL), lambda b, t: (b, t, 0)), sd_spec, sc_spec],
        out_shape=[jax.ShapeDtypeStruct(x.shape, F32),
                   jax.ShapeDtypeStruct((nb_total, GDN_HEADS, DV, DV), F32),
                   jax.ShapeDtypeStruct((nb_total, CONV_W - 1, CONV_CH), F32)],
        scratch_shapes=[pltpu.VMEM((rows, D_MODEL), BF16),
                        pltpu.VMEM((bb, CARRY + tt, CONV_CH), F32),
                        pltpu.VMEM((rows, CONV_CH), F32),
                        pltpu.VMEM((rows, N_C - CONV_CH), F32),
                        pltpu.VMEM((rows, D_MODEL), BF16)]
                       + [pltpu.VMEM((rows // TILE, GDN_HEADS, TILE, DV), F32)] * 4
                       + [pltpu.VMEM((rows // TILE, GDN_HEADS, TILE, TILE), F32),
                          pltpu.VMEM((rows // TILE, GDN_HEADS, TILE // seglen, DV), F32)],
        compiler_params=pltpu.CompilerParams(dimension_semantics=("arbitrary", "arbitrary"),
                                             vmem_limit_bytes=VMEM_LIMIT_BYTES),
        name="gdn_layer_" + ("sample" if has_state else "prompt"),
    )(*args)


def _rope_tables(pos):
    half = DK_AB // 2
    inv = ROPE_BASE ** (-jnp.arange(half, dtype=F32) / half)
    ang = pos.astype(F32)[:, None] * inv[None, :]
    cos, sin = jnp.cos(ang), jnp.sin(ang)
    cos_h = jnp.concatenate([cos, cos], axis=-1)
    sin_h = jnp.concatenate([-sin, sin], axis=-1)
    return jnp.tile(cos_h, (1, 4)), jnp.tile(sin_h, (1, 4))


def _row(a):
    return a.reshape(1, -1).astype(F32)


def _prep_ab_kernel(w_ref, o_ref):
    w = w_ref[0]
    lane = lax.broadcasted_iota(jnp.int32, (w.shape[0], GLA_RANK_PAD), 1)
    o_ref[:, 0:1536] = w[:, 0:1536].astype(BF16)
    o_ref[:, 1536:3072] = w[:, 1552:3088].astype(BF16)
    o_ref[:, 3072:N_AB] = jnp.where(lane < 16, w[:, 1536:1536 + GLA_RANK_PAD], 0.0).astype(BF16)


def _prep_c_kernel(w_ref, o_ref):
    w = w_ref[0]
    o_ref[:, 0:4096] = w[:, 0:4096].astype(BF16)
    pad = jnp.zeros((w.shape[0], 128 - 2 * GDN_HEADS), F32)
    o_ref[:, 4096:N_C] = jnp.concatenate([w[:, 4096:4096 + 2 * GDN_HEADS], pad], axis=1).astype(BF16)


def _weight_prep_call(body, w, n_out):
    _, k, n_in = w.shape
    rb = 128
    return pl.pallas_call(
        body,
        grid=(k // rb,),
        in_specs=[pl.BlockSpec((1, rb, n_in), lambda i: (0, i, 0))],
        out_specs=pl.BlockSpec((rb, n_out), lambda i: (i, 0)),
        out_shape=jax.ShapeDtypeStruct((k, n_out), BF16),
        compiler_params=pltpu.CompilerParams(dimension_semantics=("arbitrary",)),
        name="weight_prep",
    )(w)


def kernel(x_prompt, x_sample, state_gla, state_ret, state_delta, state_conv, c_prompt, c_sample, w_ada, b_ada, g_pre, g_post, w_in_ab, w_gla_gate_up, b_gla_gate, g_norm_gla, g_norm_ret, w_out_ab, w_in_c, conv_c, a_log_c, dt_bias_c, g_norm_c, w_out_c):
    nbp, t_p, _ = x_prompt.shape
    nbs, t_s, _ = x_sample.shape
    assert w_ada.shape[0] == 2 and w_in_ab.shape[0] == 1

    mod = _ada_call(jnp.concatenate([c_prompt, c_sample], axis=0), w_ada, b_ada)
    mod_p = mod[:, :nbp, None, :]
    mod_s = mod[:, nbp:, None, :]

    w_ab = _weight_prep_call(_prep_ab_kernel, w_in_ab, N_AB)
    wgu = jnp.pad(w_gla_gate_up[0], ((0, GLA_RANK_PAD - 16), (0, 0))).astype(BF16)
    lgam = jnp.repeat(jnp.log1p(-jnp.exp2(-5.0 - jnp.arange(4, dtype=F32))), DK_AB).reshape(1, 256)
    weights_ab = (_row(g_pre[0]), w_ab, wgu, _row(b_gla_gate[0]), lgam, _row(g_norm_gla[0]),
                  _row(g_norm_ret[0]), w_out_ab[0].astype(BF16), _row(g_post[0]))

    w_c = _weight_prep_call(_prep_c_kernel, w_in_c, N_C)
    head_lanes = ((0, 0), (GDN_HEADS, 128 - 2 * GDN_HEADS))
    weights_c = (_row(g_pre[1]), w_c, conv_c[0], jnp.pad(_row(a_log_c[0]), head_lanes),
                 jnp.pad(_row(dt_bias_c[0]), head_lanes), _row(g_norm_c[0]), w_out_c[0].astype(BF16),
                 _row(g_post[1]))

    cos_p, sin_p = _rope_tables(jnp.arange(t_p, dtype=jnp.int32))
    cos_s, sin_s = _rope_tables(PAST_LEN + jnp.arange(t_s, dtype=jnp.int32))
    bb_s = TILE // t_s
    cos_s, sin_s = jnp.tile(cos_s, (bb_s, 1)), jnp.tile(sin_s, (bb_s, 1))

    tt_p = 256
    xp, gla_p, ret_p = _ab_call(x_prompt, mod_p[0], cos_p, sin_p, None, weights_ab, bb=1, tt=tt_p)
    y_prompt, delta_p, conv_p = _c_call(xp, mod_p[1], None, weights_c, bb=1, tt=tt_p)

    xs, gla_s, ret_s = _ab_call(x_sample, mod_s[0], cos_s, sin_s,
                                (state_gla.reshape(state_gla.shape[1:]), state_ret.reshape(state_ret.shape[1:])),
                                weights_ab, bb=bb_s, tt=t_s)
    y_sample, delta_s, conv_s = _c_call(xs, mod_s[1],
                                        (state_delta.reshape(state_delta.shape[1:]),
                                         state_conv.reshape(state_conv.shape[1:])),
                                        weights_c, bb=bb_s, tt=t_s)
    return (y_prompt, y_sample, gla_p[None], ret_p[None], delta_p[None], conv_p[None],
            gla_s[None], ret_s[None], delta_s[None], conv_s[None])
```

```python
import functools
import math

import numpy as np
import jax
import jax.numpy as jnp
from jax import lax
from jax.experimental import pallas as pl
from jax.experimental.pallas import tpu as pltpu

F32 = jnp.float32
BF16 = jnp.bfloat16

D_MODEL = 1024
EPS = 1e-6
ROPE_BASE = 10000.0
PAST_LEN = 16384
GLA_TAU = 16.0
HEADS_AB = 8
DK_AB = 64
DV = 128
GLA_RANK_PAD = 128
N_AB = 3072 + GLA_RANK_PAD
AB_GQ, AB_GK, AB_RQ, AB_RK, AB_GATE, AB_GV, AB_GZ, AB_RV, AB_RZ, _ = (0, 256, 512, 768, 1024, 1152, 1664, 2176, 2688, 3200)
AB_CHUNKS = ((1024, 1152), (0, 512), (512, 1024), (1152, 1664), (1664, 2176), (2176, 2688), (2688, 3200))
GDN_HEADS = 8
CONV_W = 4
CONV_CH = 3072
N_C = 4096 + 128
TILE = 64
CARRY = 8
PREP_TILES_AB = 4
PREP_TILES_C = 2
VMEM_LIMIT_BYTES = 52 * 1024 * 1024


def _mm(a, b):
    return jnp.dot(a.astype(BF16), b.astype(BF16), preferred_element_type=F32)


def _mm_nt(a, b):
    return lax.dot_general(a.astype(BF16), b.astype(BF16), (((1,), (1,)), ((), ())),
                           preferred_element_type=F32)


def _mm_tn(a, b):
    return lax.dot_general(a.astype(BF16), b.astype(BF16), (((0,), (0,)), ((), ())),
                           preferred_element_type=F32)


def _bmm(a, b):
    return lax.dot_general(a.astype(BF16), b.astype(BF16), (((2,), (1,)), ((0,), (0,))),
                           preferred_element_type=F32)


def _bmm_nt(a, b):
    return lax.dot_general(a.astype(BF16), b.astype(BF16), (((2,), (2,)), ((0,), (0,))),
                           preferred_element_type=F32)


def _bmm_tn(a, b):
    return lax.dot_general(a.astype(BF16), b.astype(BF16), (((1,), (1,)), ((0,), (0,))),
                           preferred_element_type=F32)


def _split3(x):
    hi = x.astype(BF16)
    r = x - hi.astype(F32)
    mid = r.astype(BF16)
    lo = (r - mid.astype(F32)).astype(BF16)
    return hi, mid, lo


def _sigmoid(x):
    return 1.0 / (1.0 + jnp.exp(-x))


def _silu(x):
    return x * _sigmoid(x)


def _softplus(x):
    return jnp.maximum(x, 0.0) + jnp.log(1.0 + jnp.exp(-jnp.abs(x)))


def _masks(seglen, nh):
    r = lax.broadcasted_iota(jnp.int32, (nh, TILE, TILE), 1)
    c = lax.broadcasted_iota(jnp.int32, (nh, TILE, TILE), 2)
    sh = int(math.log2(seglen))
    same = (r >> sh) == (c >> sh)
    causal = same & (c <= r)
    eye = r == c
    cum = jnp.concatenate([causal[0].astype(F32), same[0].astype(F32)], axis=0).astype(BF16)
    return causal, eye, cum


def _level_masks(seglen, nh):
    r = lax.broadcasted_iota(jnp.int32, (nh, TILE, TILE), 1)
    c = lax.broadcasted_iota(jnp.int32, (nh, TILE, TILE), 2)
    masks = []
    for sh in range(int(math.log2(seglen))):
        masks.append(((r >> (sh + 1)) == (c >> (sh + 1))) & (((r >> sh) & 1) == 1) & (((c >> sh) & 1) == 0))
    return masks


def _seq_index(tile, seg, seglen, bb):
    return 0 if bb == 1 else tile * (TILE // seglen) + seg


def _prefix_and_total(cum, pieces):
    hi, mid, lo = pieces
    r = (jnp.dot(cum, hi, preferred_element_type=F32) + jnp.dot(cum, mid, preferred_element_type=F32)
         + jnp.dot(cum, lo, preferred_element_type=F32))
    return r[:TILE], r[TILE:]


def _row_to_col(eye, row):
    return jnp.sum(jnp.where(eye, jnp.broadcast_to(row, eye.shape), 0.0), axis=2, keepdims=True)


def _col_to_row(eye, col):
    return jnp.sum(jnp.where(eye, jnp.broadcast_to(col, eye.shape), 0.0), axis=1, keepdims=True)


def _prenorm(x_ref, mod_ref, gpre_ref, h_ref, rows):
    x3 = x_ref[...]
    shift = mod_ref[:, :, 0:D_MODEL]
    scale = mod_ref[:, :, D_MODEL:2 * D_MODEL]
    xn = x3 * lax.rsqrt(jnp.mean(x3 * x3, axis=-1, keepdims=True) + EPS) * gpre_ref[...]
    h3 = xn * (1.0 + scale) + shift
    h_ref[...] = h3.reshape(rows, D_MODEL).astype(BF16)


def _postnorm(x_ref, mod_ref, gpost_ref, y, y_ref, bb, tt):
    gate = mod_ref[:, :, 2 * D_MODEL:3 * D_MODEL]
    y3 = y.reshape(bb, tt, D_MODEL)
    yn = y3 * lax.rsqrt(jnp.mean(y3 * y3, axis=-1, keepdims=True) + EPS) * gpost_ref[...]
    y_ref[...] = x_ref[...] + gate * yn


def _ada_kernel(c_ref, w_ref, b_ref, o_ref):
    o_ref[0] = _mm(_silu(c_ref[...]), w_ref[0]) + b_ref[0]


def _ada_call(c_all, w_ada, b_ada):
    n = c_all.shape[0]
    depth = w_ada.shape[0]
    nj = 3
    return pl.pallas_call(
        _ada_kernel,
        grid=(depth, nj),
        in_specs=[
            pl.BlockSpec((n, D_MODEL), lambda l, j: (0, 0)),
            pl.BlockSpec((1, D_MODEL, D_MODEL), lambda l, j: (l, 0, j)),
            pl.BlockSpec((1, 1, D_MODEL), lambda l, j: (l, 0, j)),
        ],
        out_specs=pl.BlockSpec((1, n, D_MODEL), lambda l, j: (l, 0, j)),
        out_shape=jax.ShapeDtypeStruct((depth, n, 3 * D_MODEL), F32),
        compiler_params=pltpu.CompilerParams(dimension_semantics=("arbitrary", "arbitrary")),
        name="adaln_mod",
    )(c_all, w_ada, b_ada.reshape(depth, 1, 3 * D_MODEL))


def _ab_kernel(*refs, bb, tt, seglen, has_state):
    if has_state:
        (x_ref, mod_ref, cos_ref, sin_ref, sg_ref, sr_ref, gpre_ref, win_ref, wgu_ref, bgate_ref, lgam_ref,
         ggla_ref, gret_ref, wout_ref, gpost_ref, y_ref, glao_ref, reto_ref, h_ref, og_ref,
         qd_scr, ko_scr, dec_scr, oi_scr, *proj_refs) = refs
        s_in = (sg_ref, sr_ref)
    else:
        (x_ref, mod_ref, cos_ref, sin_ref, gpre_ref, win_ref, wgu_ref, bgate_ref, lgam_ref,
         ggla_ref, gret_ref, wout_ref, gpost_ref, y_ref, glao_ref, reto_ref, h_ref, og_ref,
         qd_scr, ko_scr, dec_scr, oi_scr, *proj_refs) = refs
        s_in = (glao_ref, reto_ref)
    s_out = (glao_ref, reto_ref)
    rows = bb * tt
    nseg = TILE // seglen
    n_tiles = rows // TILE
    t_id = pl.program_id(1)

    if not has_state:
        @pl.when(t_id == 0)
        def _():
            glao_ref[...] = jnp.zeros_like(glao_ref)
            reto_ref[...] = jnp.zeros_like(reto_ref)

    _prenorm(x_ref, mod_ref, gpre_ref, h_ref, rows)

    def project(chunks):
        for c0, c1 in chunks:
            proj_refs[AB_CHUNKS.index((c0, c1))][...] = jnp.dot(h_ref[...], win_ref[:, c0:c1],
                                                                preferred_element_type=F32)

    def pcols(rs, a, b):
        i = [c0 <= a and b <= c1 for c0, c1 in AB_CHUNKS].index(True)
        return proj_refs[i][rs, a - AB_CHUNKS[i][0]:b - AB_CHUNKS[i][0]]

    project(AB_CHUNKS[:1])

    nh = HEADS_AB
    tb = min(PREP_TILES_AB, n_tiles)
    causal, eye, cum = _masks(seglen, nh * tb)
    eye = eye[:nh]
    lane = lax.broadcasted_iota(jnp.int32, (TILE, 256), 1)
    first_half = (lane & 63) < 32
    gnorm = (ggla_ref[...], gret_ref[...])

    def heads(parts, width):
        return jnp.stack([p[:, h * width:(h + 1) * width] for p in parts for h in range(4)])

    def prep_gate(t):
        rs = pl.ds(t * TILE, TILE)
        log_a = _mm(pcols(rs, AB_GATE, AB_GV), wgu_ref[...]) + bgate_ref[...]
        log_a = (jnp.minimum(log_a, 0.0) - jnp.log(1.0 + jnp.exp(-jnp.abs(log_a)))) / GLA_TAU
        return [_split3(lg) for lg in (log_a, jnp.broadcast_to(lgam_ref[...], (TILE, 256)))]

    def prep_decay(t, prefix):
        rs = pl.ds(t * TILE, TILE)

        def pj(a, b):
            return pcols(rs, a, b)

        def rot(x):
            swapped = jnp.where(first_half, pltpu.roll(x, 256 - 32, 1), pltpu.roll(x, 32, 1))
            return x * cos_ref[rs, :] + swapped * sin_ref[rs, :]

        q_parts = (pj(AB_GQ, AB_GK) * (DK_AB ** -0.5), rot(pj(AB_RQ, AB_RK)) * (DK_AB ** -0.5))
        k_parts = (pj(AB_GK, AB_RQ), rot(pj(AB_RK, AB_GATE)))
        q_dec, k_inv = [], []
        for part in range(2):
            b, btot = prefix[part]
            q_dec.append(q_parts[part] * jnp.exp(b))
            k_inv.append(k_parts[part] * jnp.exp(-b))
            cs = slice(part * 256, (part + 1) * 256)
            qd_scr[rs, cs] = q_dec[part]
            ko_scr[rs, cs] = k_parts[part] * jnp.exp(btot - b)
            dec_scr[rs, cs] = jnp.exp(btot)
        return heads(q_dec, DK_AB), heads(k_inv, DK_AB)

    def prep_scores(j, qk):
        qd, ki = ([x[i] for x in qk[j * tb:(j + 1) * tb]] for i in range(2))
        qd, ki = (x[0] if tb == 1 else jnp.concatenate(x, axis=0) for x in (qd, ki))
        return jnp.where(causal, _bmm_nt(qd, ki), 0.0)

    def prep_intra(j, scores):
        v = []
        for t in range(tb):
            rs = pl.ds((j * tb + t) * TILE, TILE)
            v.append(heads((pcols(rs, AB_GV, AB_GZ), pcols(rs, AB_RV, AB_RZ)), DV))
        o_intra = _bmm(scores, v[0] if tb == 1 else jnp.concatenate(v, axis=0))
        for t in range(tb):
            oi_scr[j * tb + t] = o_intra[t * nh:(t + 1) * nh]

    assert n_tiles % tb == 0
    pieces = [prep_gate(t) for t in range(n_tiles)]
    project(AB_CHUNKS[1:3])
    prefix = [[_prefix_and_total(cum, p) for p in pieces[t]] for t in range(n_tiles)]
    project(AB_CHUNKS[3:])
    qk = [prep_decay(t, prefix[t]) for t in range(n_tiles)]
    scores = [prep_scores(j, qk) for j in range(n_tiles // tb)]
    for j in range(n_tiles // tb):
        prep_intra(j, scores[j])

    def tile_body(i):
        r0 = i * TILE
        rs = pl.ds(r0, TILE)
        qd = heads((qd_scr[rs, 0:256], qd_scr[rs, 256:512]), DK_AB)
        ko = heads((ko_scr[rs, 0:256], ko_scr[rs, 256:512]), DK_AB)
        v = heads((pcols(rs, AB_GV, AB_GZ), pcols(rs, AB_RV, AB_RZ)), DV)
        z = heads((pcols(rs, AB_GZ, AB_RV), pcols(rs, AB_RZ, N_AB)), DV)
        o_inter = []
        for s in range(nseg):
            sr = slice(s * seglen, (s + 1) * seglen)
            bidx = _seq_index(i, s, seglen, bb)
            st = jnp.concatenate([s_in[0][bidx], s_in[1][bidx]], axis=0)
            o_inter.append(_bmm(qd[:, sr], st))
            drow = pl.ds(r0 + s * seglen, 1)
            dec_row = heads((dec_scr[drow, 0:256], dec_scr[drow, 256:512]), DK_AB)
            st = st * _row_to_col(eye, dec_row) + _bmm_tn(ko[:, sr], v[:, sr])
            s_out[0][bidx] = st[:4]
            s_out[1][bidx] = st[4:]
        o = oi_scr[i] + (o_inter[0] if nseg == 1 else jnp.concatenate(o_inter, axis=1))
        o_g, o_r = o[:4], o[4:]
        on_g = o_g * lax.rsqrt(jnp.mean(o_g * o_g, axis=-1, keepdims=True) + EPS) * gnorm[0]
        dlt = o_r - jnp.mean(o_r, axis=-1, keepdims=True)
        on_r = dlt * lax.rsqrt(jnp.mean(dlt * dlt, axis=-1, keepdims=True) + EPS) * gnorm[1]
        og = (jnp.concatenate([on_g, on_r], axis=0) * _silu(z)).astype(BF16)
        for h in range(nh):
            og_ref[rs, h * DV:(h + 1) * DV] = og[h]

    for i in range(n_tiles):
        tile_body(i)
    y = jnp.dot(og_ref[...], wout_ref[...], preferred_element_type=F32)
    _postnorm(x_ref, mod_ref, gpost_ref, y, y_ref, bb, tt)


def _ab_call(x, mod, cos, sin, states, weights, *, bb, tt):
    nb_total, t_total, _ = x.shape
    has_state = states is not None
    seglen = min(TILE, tt)
    rows = bb * tt
    assert nb_total % bb == 0 and t_total % tt == 0 and rows % TILE == 0 and TILE % seglen == 0
    assert tt % 8 == 0 and (bb == 1 or tt == seglen)
    grid = (nb_total // bb, t_total // tt)
    assert not has_state or grid[1] == 1
    const2 = lambda b, t: (0, 0)
    st_spec = pl.BlockSpec((bb, 4, DK_AB, DV), lambda b, t: (b, 0, 0, 0))
    in_specs = [
        pl.BlockSpec((bb, tt, D_MODEL), lambda b, t: (b, t, 0)),
        pl.BlockSpec((bb, 1, 3 * D_MODEL), lambda b, t: (b, 0, 0)),
    ]
    if has_state:
        in_specs += [pl.BlockSpec((rows, 256), const2)] * 2 + [st_spec, st_spec]
    else:
        in_specs += [pl.BlockSpec((tt, 256), lambda b, t: (t, 0))] * 2
    in_specs += [pl.BlockSpec(w.shape, const2, pipeline_mode=pl.Buffered(1)) for w in weights]
    args = (x, mod, cos, sin) + (tuple(states) if has_state else ()) + tuple(weights)
    st_shape = jax.ShapeDtypeStruct((nb_total, 4, DK_AB, DV), F32)
    return pl.pallas_call(
        functools.partial(_ab_kernel, bb=bb, tt=tt, seglen=seglen, has_state=has_state),
        grid=grid,
        in_specs=in_specs,
        out_specs=[pl.BlockSpec((bb, tt, D_MODEL), lambda b, t: (b, t, 0)), st_spec, st_spec],
        out_shape=[jax.ShapeDtypeStruct(x.shape, F32), st_shape, st_shape],
        scratch_shapes=[pltpu.VMEM((rows, D_MODEL), BF16), pltpu.VMEM((rows, D_MODEL), BF16)]
                       + [pltpu.VMEM((rows, HEADS_AB * DK_AB), F32)] * 3
                       + [pltpu.VMEM((rows // TILE, HEADS_AB, TILE, DV), F32)]
                       + [pltpu.VMEM((rows, c1 - c0), F32) for c0, c1 in AB_CHUNKS],
        compiler_params=pltpu.CompilerParams(dimension_semantics=("arbitrary", "arbitrary"),
                                             vmem_limit_bytes=VMEM_LIMIT_BYTES),
        name="gla_ret_layer_" + ("sample" if has_state else "prompt"),
    )(*args)


def _c_kernel(*refs, bb, tt, seglen, has_state):
    if has_state:
        (x_ref, mod_ref, sd_ref, sc_ref, gpre_ref, win_ref, convw_ref, alog_ref, dtb_ref, gn_ref, wout_ref,
         gpost_ref, y_ref, do_ref, co_ref, h_ref, cbuf_ref, qkv_ref, zab_ref, og_ref,
         u_scr, w_scr, qd_scr, ko_scr, at_scr, dec_scr) = refs
        s_in = sd_ref
    else:
        (x_ref, mod_ref, gpre_ref, win_ref, convw_ref, alog_ref, dtb_ref, gn_ref, wout_ref,
         gpost_ref, y_ref, do_ref, co_ref, h_ref, cbuf_ref, qkv_ref, zab_ref, og_ref,
         u_scr, w_scr, qd_scr, ko_scr, at_scr, dec_scr) = refs
        s_in = do_ref
    rows = bb * tt
    nseg = TILE // seglen
    n_tiles = rows // TILE
    t_id = pl.program_id(1)
    keep =slice(CARRY + tt - (CONV_W - 1), CARRY + tt)
    slot = slice(CARRY - (CONV_W - 1), CARRY)

    @pl.when(t_id > 0)
    def _():
        cbuf_ref[:, slot, :] = cbuf_ref[:, keep, :]

    @pl.when(t_id == 0)
    def _():
        cbuf_ref[:, 0:CARRY, :] = jnp.zeros((bb, CARRY, CONV_CH), F32)
        if has_state:
            cbuf_ref[:, slot, :] = sc_ref[...]
        else:
            do_ref[...] = jnp.zeros_like(do_ref)

    _prenorm(x_ref, mod_ref, gpre_ref, h_ref, rows)

    def project_z(c0, c1):
        zab_ref[:, c0 - CONV_CH:c1 - CONV_CH] = jnp.dot(h_ref[...], win_ref[:, c0:c1], preferred_element_type=F32)

    def project_qkv(c0):
        pr = jnp.dot(h_ref[...], win_ref[:, c0:c0 + 512], preferred_element_type=F32)
        cbuf_ref[:, CARRY:CARRY + tt, c0:c0 + 512] = pr.reshape(bb, tt, 512)
        co_ref[:, :, c0:c0 + 512] = cbuf_ref[:, keep, c0:c0 + 512]

    def conv_silu(c0):
        xs = cbuf_ref[:, :, c0:c0 + 512]
        acc = None
        for i in range(CONV_W):
            back = CONV_W - 1 - i
            shifted = xs if back == 0 else pltpu.roll(xs, back, 1)
            term = shifted[:, CARRY:, :] * convw_ref[i:i + 1, c0:c0 + 512]
            acc = term if acc is None else acc + term
        qkv_ref[:, c0:c0 + 512] = _silu(acc).reshape(rows, 512)

    nh = GDN_HEADS
    tb = min(PREP_TILES_C, n_tiles)
    causal, eye, cum = _masks(seglen, nh * tb)
    level_masks = _level_masks(seglen, nh * tb)
    eye_f = eye.astype(F32)
    neg_decay_rate = -jnp.exp(alog_ref[...])

    def gate(t):
        ba = zab_ref[pl.ds(t * TILE, TILE), 1024:1152]
        return _sigmoid(ba), _split3(neg_decay_rate * _softplus(ba + dtb_ref[...]))

    project_z(4096, N_C)
    gates = [gate(t) for t in range(n_tiles)]
    chunk_starts = list(range(0, CONV_CH, 512))
    project_qkv(chunk_starts[0])
    prefix = None
    for prev, c0 in zip(chunk_starts[:-1], chunk_starts[1:]):
        project_qkv(c0)
        if prefix is None:
            prefix = [_prefix_and_total(cum, g[1]) for g in gates]
        conv_silu(prev)
    project_z(CONV_CH, CONV_CH + 512)
    project_z(CONV_CH + 512, 4096)
    conv_silu(chunk_starts[-1])

    def prep_body(j):
        q, k, v, beta, b_c, btot_c = [], [], [], [], [], []
        for t in range(tb):
            rs = pl.ds((j * tb + t) * TILE, TILE)
            beta_all = gates[j * tb + t][0]
            b_all, btot_all = prefix[j * tb + t]
            for h in range(nh):
                q.append(qkv_ref[rs, h * DV:(h + 1) * DV])
                k.append(qkv_ref[rs, 1024 + h * DV:1024 + (h + 1) * DV])
                v.append(qkv_ref[rs, 2048 + h * DV:2048 + (h + 1) * DV])
                beta.append(beta_all[:, h:h + 1])
                b_c.append(b_all[:, nh + h:nh + h + 1])
                btot_c.append(btot_all[:, nh + h:nh + h + 1])
        q, k, v = jnp.stack(q), jnp.stack(k), jnp.stack(v)
        beta, b_c, btot_c = jnp.stack(beta), jnp.stack(b_c), jnp.stack(btot_c)
        qn = q * lax.rsqrt(jnp.sum(q * q, axis=-1, keepdims=True) + EPS) * (DV ** -0.5)
        kn = k * lax.rsqrt(jnp.sum(k * k, axis=-1, keepdims=True) + EPS)
        b_r = _col_to_row(eye, b_c)
        seg = jnp.where(causal, jnp.exp(jnp.where(causal, b_c - b_r, 0.0)), 0.0)
        k_beta = kn * beta
        v_beta = v * beta
        a_mat = _bmm_nt(k_beta, kn) * seg
        inv = eye_f - jnp.where(level_masks[0], a_mat, 0.0)
        for lm in level_masks[1:]:
            inv = inv - _bmm(_bmm(inv, jnp.where(lm, a_mat, 0.0)), inv)
        e_b = jnp.exp(b_c)
        sol = _bmm(inv, jnp.concatenate([v_beta, k_beta * e_b], axis=2))
        attn = _bmm_nt(qn, kn) * seg
        q_dec = qn * e_b
        k_out = kn * jnp.exp(btot_c - b_c)
        dec = [jnp.exp(btot_c[:, s * seglen:s * seglen + 1, :]) for s in range(nseg)]
        dec = jnp.broadcast_to(dec[0] if nseg == 1 else jnp.concatenate(dec, axis=1), (nh * tb, nseg, DV))
        for t in range(tb):
            ti = j * tb + t
            hs = slice(t * nh, (t + 1) * nh)
            u_scr[ti] = sol[hs, :, :DV]
            w_scr[ti] = sol[hs, :, DV:]
            qd_scr[ti] = q_dec[hs]
            ko_scr[ti] = k_out[hs]
            at_scr[ti] = attn[hs]
            dec_scr[ti] = dec[hs]

    assert n_tiles % tb == 0
    for j in range(n_tiles // tb):
        prep_body(j)

    def tile_body(i):
        rs = pl.ds(i * TILE, TILE)
        u, w, q_dec, k_out, attn = u_scr[i], w_scr[i], qd_scr[i], ko_scr[i], at_scr[i]
        v_new, o_inter, states = [], [], []
        for s in range(nseg):
            sr = slice(s * seglen, (s + 1) * seglen)
            st = s_in[_seq_index(i, s, seglen, bb)]
            ws = _bmm(jnp.concatenate([w[:, sr], q_dec[:, sr]], axis=1), st)
            v_new.append(u[:, sr] - ws[:, :seglen])
            o_inter.append(ws[:, seglen:])
            states.append(st)
        v_new = v_new[0] if nseg == 1 else jnp.concatenate(v_new, axis=1)
        o_inter = o_inter[0] if nseg == 1 else jnp.concatenate(o_inter, axis=1)
        o = o_inter + _bmm(attn, v_new)
        for s in range(nseg):
            sr = slice(s * seglen, (s + 1) * seglen)
            dec = dec_scr[i, :, s:s + 1, :]
            do_ref[_seq_index(i, s, seglen, bb)] = states[s] * dec + _bmm_tn(k_out[:, sr], v_new[:, sr])
        on = o * lax.rsqrt(jnp.mean(o * o, axis=-1, keepdims=True) + EPS) * gn_ref[...]
        for h in range(nh):
            z = zab_ref[rs, h * DV:(h + 1) * DV]
            og_ref[rs, h * DV:(h + 1) * DV] = (on[h] * _silu(z)).astype(BF16)

    for i in range(n_tiles):
        tile_body(i)
    y = jnp.dot(og_ref[...], wout_ref[...], preferred_element_type=F32)
    _postnorm(x_ref, mod_ref, gpost_ref, y, y_ref, bb, tt)


def _c_call(x, mod, states, weights, *, bb, tt):
    nb_total, t_total, _ = x.shape
    has_state = states is not None
    seglen = min(TILE, tt)
    rows = bb * tt
    assert nb_total % bb == 0 and t_total % tt == 0 and rows % TILE == 0 and TILE % seglen == 0
    assert tt % 8 == 0 and tt >= CONV_W - 1 and (bb == 1 or tt == seglen)
    grid = (nb_total // bb, t_total // tt)
    assert not has_state or grid[1] == 1
    const2 = lambda b, t: (0, 0)
    sd_spec = pl.BlockSpec((bb, GDN_HEADS, DV, DV), lambda b, t: (b, 0, 0, 0))
    sc_spec = pl.BlockSpec((bb, CONV_W - 1, CONV_CH), lambda b, t: (b, 0, 0))
    in_specs = [
        pl.BlockSpec((bb, tt, D_MODEL), lambda b, t: (b, t, 0)),
        pl.BlockSpec((bb, 1, 3 * D_MODEL), lambda b, t: (b, 0, 0)),
    ]
    if has_state:
        in_specs += [sd_spec, sc_spec]
    in_specs += [pl.BlockSpec(w.shape, const2, pipeline_mode=pl.Buffered(1)) for w in weights]
    args = (x, mod) + (tuple(states) if has_state else ()) + tuple(weights)
    return pl.pallas_call(
        functools.partial(_c_kernel, bb=bb, tt=tt, seglen=seglen, has_state=has_state),
        grid=grid,
        in_specs=in_specs,
        out_specs=[pl.BlockSpec((bb, tt, D_MODEL), lambda b, t: (b, t, 0)), sd_spec, sc_spec],
        out_shape=[jax.ShapeDtypeStruct(x.shape, F32),
                   jax.ShapeDtypeStruct((nb_total, GDN_HEADS, DV, DV), F32),
                   jax.ShapeDtypeStruct((nb_total, CONV_W - 1, CONV_CH), F32)],
        scratch_shapes=[pltpu.VMEM((rows, D_MODEL), BF16),
                        pltpu.VMEM((bb, CARRY + tt, CONV_CH), F32),
                        pltpu.VMEM((rows, CONV_CH), F32),
                        pltpu.VMEM((rows, N_C - CONV_CH), F32),
                        pltpu.VMEM((rows, D_MODEL), BF16)]
                       + [pltpu.VMEM((rows // TILE, GDN_HEADS, TILE, DV), F32)] * 4
                       + [pltpu.VMEM((rows // TILE, GDN_HEADS, TILE, TILE), F32),
                          pltpu.VMEM((rows // TILE, GDN_HEADS, TILE // seglen, DV), F32)],
        compiler_params=pltpu.CompilerParams(dimension_semantics=("arbitrary", "arbitrary"),
                                             vmem_limit_bytes=VMEM_LIMIT_BYTES),
        name="gdn_layer_" + ("sample" if has_state else "prompt"),
    )(*args)


def _rope_tables(pos):
    half = DK_AB // 2
    inv = ROPE_BASE ** (-jnp.arange(half, dtype=F32) / half)
    ang = pos.astype(F32)[:, None] * inv[None, :]
    cos, sin = jnp.cos(ang), jnp.sin(ang)
    cos_h = jnp.concatenate([cos, cos], axis=-1)
    sin_h = jnp.concatenate([-sin, sin], axis=-1)
    return jnp.tile(cos_h, (1, 4)), jnp.tile(sin_h, (1, 4))


def _row(a):
    return a.reshape(1, -1).astype(F32)


def _prep_ab_kernel(w_ref, o_ref):
    w = w_ref[0]
    lane = lax.broadcasted_iota(jnp.int32, (w.shape[0], GLA_RANK_PAD), 1)
    o_ref[:, AB_GQ:AB_RQ] = w[:, 0:512].astype(BF16)
    o_ref[:, AB_RQ:AB_GATE] = w[:, 1552:2064].astype(BF16)
    o_ref[:, AB_GATE:AB_GV] = jnp.where(lane < 16, w[:, 1536:1536 + GLA_RANK_PAD], 0.0).astype(BF16)
    o_ref[:, AB_GV:AB_RV] = w[:, 512:1536].astype(BF16)
    o_ref[:, AB_RV:N_AB] = w[:, 2064:3088].astype(BF16)


def _prep_c_kernel(w_ref, o_ref):
    w = w_ref[0]
    o_ref[:, 0:4096] = w[:, 0:4096].astype(BF16)
    pad = jnp.zeros((w.shape[0], 128 - 2 * GDN_HEADS), F32)
    o_ref[:, 4096:N_C] = jnp.concatenate([w[:, 4096:4096 + 2 * GDN_HEADS], pad], axis=1).astype(BF16)


def _weight_prep_call(body, w, n_out):
    _, k, n_in = w.shape
    rb = 128
    return pl.pallas_call(
        body,
        grid=(k // rb,),
        in_specs=[pl.BlockSpec((1, rb, n_in), lambda i: (0, i, 0))],
        out_specs=pl.BlockSpec((rb, n_out), lambda i: (i, 0)),
        out_shape=jax.ShapeDtypeStruct((k, n_out), BF16),
        compiler_params=pltpu.CompilerParams(dimension_semantics=("arbitrary",)),
        name="weight_prep",
    )(w)


def kernel(x_prompt, x_sample, state_gla, state_ret, state_delta, state_conv, c_prompt, c_sample, w_ada, b_ada, g_pre, g_post, w_in_ab, w_gla_gate_up, b_gla_gate, g_norm_gla, g_norm_ret, w_out_ab, w_in_c, conv_c, a_log_c, dt_bias_c, g_norm_c, w_out_c):
    nbp, t_p, _ = x_prompt.shape
    nbs, t_s, _ = x_sample.shape
    assert w_ada.shape[0] == 2 and w_in_ab.shape[0] == 1

    mod = _ada_call(jnp.concatenate([c_prompt, c_sample], axis=0), w_ada, b_ada)
    mod_p = mod[:, :nbp, None, :]
    mod_s = mod[:, nbp:, None, :]

    w_ab = _weight_prep_call(_prep_ab_kernel, w_in_ab, N_AB)
    wgu = jnp.pad(w_gla_gate_up[0], ((0, GLA_RANK_PAD - 16), (0, 0))).astype(BF16)
    lgam = jnp.repeat(jnp.log1p(-jnp.exp2(-5.0 - jnp.arange(4, dtype=F32))), DK_AB).reshape(1, 256)
    weights_ab = (_row(g_pre[0]), w_ab, wgu, _row(b_gla_gate[0]), lgam, _row(g_norm_gla[0]),
                  _row(g_norm_ret[0]), w_out_ab[0].astype(BF16), _row(g_post[0]))

    w_c = _weight_prep_call(_prep_c_kernel, w_in_c, N_C)
    head_lanes = ((0, 0), (GDN_HEADS, 128 - 2 * GDN_HEADS))
    weights_c = (_row(g_pre[1]), w_c, conv_c[0], jnp.pad(_row(a_log_c[0]), head_lanes),
                 jnp.pad(_row(dt_bias_c[0]), head_lanes), _row(g_norm_c[0]), w_out_c[0].astype(BF16),
                 _row(g_post[1]))

    cos_p, sin_p = _rope_tables(jnp.arange(t_p, dtype=jnp.int32))
    cos_s, sin_s = _rope_tables(PAST_LEN + jnp.arange(t_s, dtype=jnp.int32))
    bb_s = TILE // t_s
    cos_s, sin_s = jnp.tile(cos_s, (bb_s, 1)), jnp.tile(sin_s, (bb_s, 1))

    tt_p = 256
    xp, gla_p, ret_p = _ab_call(x_prompt, mod_p[0], cos_p, sin_p, None, weights_ab, bb=1, tt=tt_p)
    y_prompt, delta_p, conv_p = _c_call(xp, mod_p[1], None, weights_c, bb=1, tt=tt_p)

    xs, gla_s, ret_s = _ab_call(x_sample, mod_s[0], cos_s, sin_s,
                                (state_gla.reshape(state_gla.shape[1:]), state_ret.reshape(state_ret.shape[1:])),
                                weights_ab, bb=bb_s, tt=t_s)
    y_sample, delta_s, conv_s = _c_call(xs, mod_s[1],
                                        (state_delta.reshape(state_delta.shape[1:]),
                                         state_conv.reshape(state_conv.shape[1:])),
                                        weights_c, bb=bb_s, tt=t_s)
    return (y_prompt, y_sample, gla_p[None], ret_p[None], delta_p[None], conv_p[None],
            gla_s[None], ret_s[None], delta_s[None], conv_s[None])
```

```python
import functools
import math

import numpy as np
import jax
import jax.numpy as jnp
from jax import lax
from jax.experimental import pallas as pl
from jax.experimental.pallas import tpu as pltpu

F32 = jnp.float32
BF16 = jnp.bfloat16

D_MODEL = 1024
EPS = 1e-6
ROPE_BASE = 10000.0
PAST_LEN = 16384
GLA_TAU = 16.0
HEADS_AB = 8
DK_AB = 64
DV = 128
GLA_RANK_PAD = 128
N_AB = 3072 + GLA_RANK_PAD
AB_GQ, AB_GK, AB_RQ, AB_RK, AB_GATE, AB_GV, AB_GZ, AB_RV, AB_RZ, _ = (0, 256, 512, 768, 1024, 1152, 1664, 2176, 2688, 3200)
AB_CHUNKS = ((1024, 1152), (0, 512), (512, 1024), (1152, 1664), (1664, 2176), (2176, 2688), (2688, 3200))
GDN_HEADS = 8
CONV_W = 4
CONV_CH = 3072
N_C = 4096 + 128
TILE = 64
CARRY = 8
PREP_TILES_AB = 4
PREP_TILES_C = 2
VMEM_LIMIT_BYTES = 52 * 1024 * 1024


def _mm(a, b):
    return jnp.dot(a.astype(BF16), b.astype(BF16), preferred_element_type=F32)


def _mm_nt(a, b):
    return lax.dot_general(a.astype(BF16), b.astype(BF16), (((1,), (1,)), ((), ())),
                           preferred_element_type=F32)


def _mm_tn(a, b):
    return lax.dot_general(a.astype(BF16), b.astype(BF16), (((0,), (0,)), ((), ())),
                           preferred_element_type=F32)


def _bmm(a, b):
    return lax.dot_general(a.astype(BF16), b.astype(BF16), (((2,), (1,)), ((0,), (0,))),
                           preferred_element_type=F32)


def _bmm_nt(a, b):
    return lax.dot_general(a.astype(BF16), b.astype(BF16), (((2,), (2,)), ((0,), (0,))),
                           preferred_element_type=F32)


def _bmm_tn(a, b):
    return lax.dot_general(a.astype(BF16), b.astype(BF16), (((1,), (1,)), ((0,), (0,))),
                           preferred_element_type=F32)


def _split3(x):
    hi = x.astype(BF16)
    r = x - hi.astype(F32)
    mid = r.astype(BF16)
    lo = (r - mid.astype(F32)).astype(BF16)
    return hi, mid, lo


def _sigmoid(x):
    return 1.0 / (1.0 + jnp.exp(-x))


def _silu(x):
    return x * _sigmoid(x)


def _softplus(x):
    return jnp.maximum(x, 0.0) + jnp.log(1.0 + jnp.exp(-jnp.abs(x)))


def _masks(seglen, nh):
    r = lax.broadcasted_iota(jnp.int32, (nh, TILE, TILE), 1)
    c = lax.broadcasted_iota(jnp.int32, (nh, TILE, TILE), 2)
    sh = int(math.log2(seglen))
    same = (r >> sh) == (c >> sh)
    causal = same & (c <= r)
    eye = r == c
    cum = jnp.concatenate([causal[0].astype(F32), same[0].astype(F32)], axis=0).astype(BF16)
    return causal, eye, cum


def _level_masks(seglen, nh):
    r = lax.broadcasted_iota(jnp.int32, (nh, TILE, TILE), 1)
    c = lax.broadcasted_iota(jnp.int32, (nh, TILE, TILE), 2)
    masks = []
    for sh in range(int(math.log2(seglen))):
        masks.append(((r >> (sh + 1)) == (c >> (sh + 1))) & (((r >> sh) & 1) == 1) & (((c >> sh) & 1) == 0))
    return masks


def _seq_index(tile, seg, seglen, bb):
    return 0 if bb == 1 else tile * (TILE // seglen) + seg


def _prefix_and_total(cum, pieces):
    hi, mid, lo = pieces
    r = (jnp.dot(cum, hi, preferred_element_type=F32) + jnp.dot(cum, mid, preferred_element_type=F32)
         + jnp.dot(cum, lo, preferred_element_type=F32))
    return r[:TILE], r[TILE:]


def _row_to_col(eye, row):
    return jnp.sum(jnp.where(eye, jnp.broadcast_to(row, eye.shape), 0.0), axis=2, keepdims=True)


def _col_to_row(eye, col):
    return jnp.sum(jnp.where(eye, jnp.broadcast_to(col, eye.shape), 0.0), axis=1, keepdims=True)


def _prenorm(x_ref, mod_ref, gpre_ref, h_ref, rows):
    x3 = x_ref[...]
    shift = mod_ref[:, :, 0:D_MODEL]
    scale = mod_ref[:, :, D_MODEL:2 * D_MODEL]
    xn = x3 * lax.rsqrt(jnp.mean(x3 * x3, axis=-1, keepdims=True) + EPS) * gpre_ref[...]
    h3 = xn * (1.0 + scale) + shift
    h_ref[...] = h3.reshape(rows, D_MODEL).astype(BF16)


def _out_project(x_ref, mod_ref, gpost_ref, og_ref, wout_ref, y_ref, bb, tt):
    gate = mod_ref[:, :, 2 * D_MODEL:3 * D_MODEL]
    y3 = jnp.dot(og_ref[...], wout_ref[...], preferred_element_type=F32).reshape(bb, tt, D_MODEL)
    yn = y3 * lax.rsqrt(jnp.mean(y3 * y3, axis=-1, keepdims=True) + EPS) * gpost_ref[...]
    y_ref[...] = x_ref[...] + gate * yn


def _ada_kernel(c_ref, w_ref, b_ref, o_ref):
    o_ref[0] = _mm(_silu(c_ref[...]), w_ref[0]) + b_ref[0]


def _ada_call(c_all, w_ada, b_ada):
    n = c_all.shape[0]
    depth = w_ada.shape[0]
    nj = 3
    return pl.pallas_call(
        _ada_kernel,
        grid=(depth, nj),
        in_specs=[
            pl.BlockSpec((n, D_MODEL), lambda l, j: (0, 0)),
            pl.BlockSpec((1, D_MODEL, D_MODEL), lambda l, j: (l, 0, j)),
            pl.BlockSpec((1, 1, D_MODEL), lambda l, j: (l, 0, j)),
        ],
        out_specs=pl.BlockSpec((1, n, D_MODEL), lambda l, j: (l, 0, j)),
        out_shape=jax.ShapeDtypeStruct((depth, n, 3 * D_MODEL), F32),
        compiler_params=pltpu.CompilerParams(dimension_semantics=("arbitrary", "arbitrary")),
        name="adaln_mod",
    )(c_all, w_ada, b_ada.reshape(depth, 1, 3 * D_MODEL))


def _ab_kernel(*refs, bb, tt, seglen, has_state):
    if has_state:
        (x_ref, mod_ref, cos_ref, sin_ref, sg_ref, sr_ref, gpre_ref, win_ref, wgu_ref, bgate_ref, lgam_ref,
         ggla_ref, gret_ref, wout_ref, gpost_ref, y_ref, glao_ref, reto_ref, h_ref, og_ref,
         qd_scr, ko_scr, dec_scr, oi_scr, *proj_refs) = refs
        s_in = (sg_ref, sr_ref)
    else:
        (x_ref, mod_ref, cos_ref, sin_ref, gpre_ref, win_ref, wgu_ref, bgate_ref, lgam_ref,
         ggla_ref, gret_ref, wout_ref, gpost_ref, y_ref, glao_ref, reto_ref, h_ref, og_ref,
         qd_scr, ko_scr, dec_scr, oi_scr, *proj_refs) = refs
        s_in = (glao_ref, reto_ref)
    s_out = (glao_ref, reto_ref)
    rows = bb * tt
    nseg = TILE // seglen
    n_tiles = rows // TILE
    t_id = pl.program_id(1)

    if not has_state:
        @pl.when(t_id == 0)
        def _():
            glao_ref[...] = jnp.zeros_like(glao_ref)
            reto_ref[...] = jnp.zeros_like(reto_ref)

    _prenorm(x_ref, mod_ref, gpre_ref, h_ref, rows)

    def project(chunks):
        for c0, c1 in chunks:
            proj_refs[AB_CHUNKS.index((c0, c1))][...] = jnp.dot(h_ref[...], win_ref[:, c0:c1],
                                                                preferred_element_type=F32)

    def pcols(rs, a, b):
        i = [c0 <= a and b <= c1 for c0, c1 in AB_CHUNKS].index(True)
        return proj_refs[i][rs, a - AB_CHUNKS[i][0]:b - AB_CHUNKS[i][0]]

    project(AB_CHUNKS[:1])

    nh = HEADS_AB
    tb = min(PREP_TILES_AB, n_tiles)
    causal, eye, cum = _masks(seglen, nh * tb)
    eye = eye[:nh]
    lane = lax.broadcasted_iota(jnp.int32, (TILE, 256), 1)
    first_half = (lane & 63) < 32
    gnorm = (ggla_ref[...], gret_ref[...])

    def heads(parts, width):
        return jnp.stack([p[:, h * width:(h + 1) * width] for p in parts for h in range(4)])

    def prep_gate(t):
        rs = pl.ds(t * TILE, TILE)
        log_a = _mm(pcols(rs, AB_GATE, AB_GV), wgu_ref[...]) + bgate_ref[...]
        log_a = (jnp.minimum(log_a, 0.0) - jnp.log(1.0 + jnp.exp(-jnp.abs(log_a)))) / GLA_TAU
        return [_split3(lg) for lg in (log_a, jnp.broadcast_to(lgam_ref[...], (TILE, 256)))]

    def prep_decay(t, prefix):
        rs = pl.ds(t * TILE, TILE)

        def pj(a, b):
            return pcols(rs, a, b)

        def rot(x):
            swapped = jnp.where(first_half, pltpu.roll(x, 256 - 32, 1), pltpu.roll(x, 32, 1))
            return x * cos_ref[rs, :] + swapped * sin_ref[rs, :]

        q_parts = (pj(AB_GQ, AB_GK) * (DK_AB ** -0.5), rot(pj(AB_RQ, AB_RK)) * (DK_AB ** -0.5))
        k_parts = (pj(AB_GK, AB_RQ), rot(pj(AB_RK, AB_GATE)))
        q_dec, k_inv = [], []
        for part in range(2):
            b, btot = prefix[part]
            q_dec.append(q_parts[part] * jnp.exp(b))
            k_inv.append(k_parts[part] * jnp.exp(-b))
            cs = slice(part * 256, (part + 1) * 256)
            qd_scr[rs, cs] = q_dec[part]
            ko_scr[rs, cs] = k_parts[part] * jnp.exp(btot - b)
            dec_scr[rs, cs] = jnp.exp(btot)
        return heads(q_dec, DK_AB), heads(k_inv, DK_AB)

    def prep_scores(j, qk):
        qd, ki = ([x[i] for x in qk[j * tb:(j + 1) * tb]] for i in range(2))
        qd, ki = (x[0] if tb == 1 else jnp.concatenate(x, axis=0) for x in (qd, ki))
        return jnp.where(causal, _bmm_nt(qd, ki), 0.0)

    def prep_intra(j, scores):
        v = []
        for t in range(tb):
            rs = pl.ds((j * tb + t) * TILE, TILE)
            v.append(heads((pcols(rs, AB_GV, AB_GZ), pcols(rs, AB_RV, AB_RZ)), DV))
        o_intra = _bmm(scores, v[0] if tb == 1 else jnp.concatenate(v, axis=0))
        for t in range(tb):
            oi_scr[j * tb + t] = o_intra[t * nh:(t + 1) * nh]

    assert n_tiles % tb == 0
    pieces = [prep_gate(t) for t in range(n_tiles)]
    project(AB_CHUNKS[1:3])
    prefix = [[_prefix_and_total(cum, p) for p in pieces[t]] for t in range(n_tiles)]
    project(AB_CHUNKS[3:])
    qk = [prep_decay(t, prefix[t]) for t in range(n_tiles)]
    scores = [prep_scores(j, qk) for j in range(n_tiles // tb)]
    for j in range(n_tiles // tb):
        prep_intra(j, scores[j])

    def tile_body(i):
        r0 = i * TILE
        rs = pl.ds(r0, TILE)
        qd = heads((qd_scr[rs, 0:256], qd_scr[rs, 256:512]), DK_AB)
        ko = heads((ko_scr[rs, 0:256], ko_scr[rs, 256:512]), DK_AB)
        v = heads((pcols(rs, AB_GV, AB_GZ), pcols(rs, AB_RV, AB_RZ)), DV)
        z = heads((pcols(rs, AB_GZ, AB_RV), pcols(rs, AB_RZ, N_AB)), DV)
        o_inter = []
        for s in range(nseg):
            sr = slice(s * seglen, (s + 1) * seglen)
            bidx = _seq_index(i, s, seglen, bb)
            st = jnp.concatenate([s_in[0][bidx], s_in[1][bidx]], axis=0)
            o_inter.append(_bmm(qd[:, sr], st))
            drow = pl.ds(r0 + s * seglen, 1)
            dec_row = heads((dec_scr[drow, 0:256], dec_scr[drow, 256:512]), DK_AB)
            st = st * _row_to_col(eye, dec_row) + _bmm_tn(ko[:, sr], v[:, sr])
            s_out[0][bidx] = st[:4]
            s_out[1][bidx] = st[4:]
        o = oi_scr[i] + (o_inter[0] if nseg == 1 else jnp.concatenate(o_inter, axis=1))
        o_g, o_r = o[:4], o[4:]
        on_g = o_g * lax.rsqrt(jnp.mean(o_g * o_g, axis=-1, keepdims=True) + EPS) * gnorm[0]
        dlt = o_r - jnp.mean(o_r, axis=-1, keepdims=True)
        on_r = dlt * lax.rsqrt(jnp.mean(dlt * dlt, axis=-1, keepdims=True) + EPS) * gnorm[1]
        og = (jnp.concatenate([on_g, on_r], axis=0) * _silu(z)).astype(BF16)
        for h in range(nh):
            og_ref[rs, h * DV:(h + 1) * DV] = og[h]

    for i in range(n_tiles):
        tile_body(i)
    _out_project(x_ref, mod_ref, gpost_ref, og_ref, wout_ref, y_ref, bb, tt)


def _ab_call(x, mod, cos, sin, states, weights, *, bb, tt):
    nb_total, t_total, _ = x.shape
    has_state = states is not None
    seglen = min(TILE, tt)
    rows = bb * tt
    assert nb_total % bb == 0 and t_total % tt == 0 and rows % TILE == 0 and TILE % seglen == 0
    assert tt % 8 == 0 and (bb == 1 or tt == seglen)
    grid = (nb_total // bb, t_total // tt)
    assert not has_state or grid[1] == 1
    const2 = lambda b, t: (0, 0)
    st_spec = pl.BlockSpec((bb, 4, DK_AB, DV), lambda b, t: (b, 0, 0, 0))
    in_specs = [
        pl.BlockSpec((bb, tt, D_MODEL), lambda b, t: (b, t, 0)),
        pl.BlockSpec((bb, 1, 3 * D_MODEL), lambda b, t: (b, 0, 0)),
    ]
    if has_state:
        in_specs += [pl.BlockSpec((rows, 256), const2)] * 2 + [st_spec, st_spec]
    else:
        in_specs += [pl.BlockSpec((tt, 256), lambda b, t: (t, 0))] * 2
    in_specs += [pl.BlockSpec(w.shape, const2, pipeline_mode=pl.Buffered(1)) for w in weights]
    args = (x, mod, cos, sin) + (tuple(states) if has_state else ()) + tuple(weights)
    st_shape = jax.ShapeDtypeStruct((nb_total, 4, DK_AB, DV), F32)
    return pl.pallas_call(
        functools.partial(_ab_kernel, bb=bb, tt=tt, seglen=seglen, has_state=has_state),
        grid=grid,
        in_specs=in_specs,
        out_specs=[pl.BlockSpec((bb, tt, D_MODEL), lambda b, t: (b, t, 0)), st_spec, st_spec],
        out_shape=[jax.ShapeDtypeStruct(x.shape, F32), st_shape, st_shape],
        scratch_shapes=[pltpu.VMEM((rows, D_MODEL), BF16), pltpu.VMEM((rows, D_MODEL), BF16)]
                       + [pltpu.VMEM((rows, HEADS_AB * DK_AB), F32)] * 3
                       + [pltpu.VMEM((rows // TILE, HEADS_AB, TILE, DV), F32)]
                       + [pltpu.VMEM((rows, c1 - c0), F32) for c0, c1 in AB_CHUNKS],
        compiler_params=pltpu.CompilerParams(dimension_semantics=("arbitrary", "arbitrary"),
                                             vmem_limit_bytes=VMEM_LIMIT_BYTES),
        name="gla_ret_layer_" + ("sample" if has_state else "prompt"),
    )(*args)


def _c_kernel(*refs, bb, tt, seglen, has_state):
    if has_state:
        (x_ref, mod_ref, sd_ref, sc_ref, gpre_ref, win_ref, convw_ref, alog_ref, dtb_ref, gn_ref, wout_ref,
         gpost_ref, y_ref, do_ref, co_ref, h_ref, cbuf_ref, qkv_ref, zab_ref, og_ref,
         u_scr, w_scr, qd_scr, ko_scr, at_scr, dec_scr) = refs
        s_in = sd_ref
    else:
        (x_ref, mod_ref, gpre_ref, win_ref, convw_ref, alog_ref, dtb_ref, gn_ref, wout_ref,
         gpost_ref, y_ref, do_ref, co_ref, h_ref, cbuf_ref, qkv_ref, zab_ref, og_ref,
         u_scr, w_scr, qd_scr, ko_scr, at_scr, dec_scr) = refs
        s_in = do_ref
    rows = bb * tt
    nseg = TILE // seglen
    n_tiles = rows // TILE
    t_id = pl.program_id(1)
    keep =slice(CARRY + tt - (CONV_W - 1), CARRY + tt)
    slot = slice(CARRY - (CONV_W - 1), CARRY)

    @pl.when(t_id > 0)
    def _():
        cbuf_ref[:, slot, :] = cbuf_ref[:, keep, :]

    @pl.when(t_id == 0)
    def _():
        cbuf_ref[:, 0:CARRY, :] = jnp.zeros((bb, CARRY, CONV_CH), F32)
        if has_state:
            cbuf_ref[:, slot, :] = sc_ref[...]
        else:
            do_ref[...] = jnp.zeros_like(do_ref)

    _prenorm(x_ref, mod_ref, gpre_ref, h_ref, rows)

    def project_z(c0, c1):
        zab_ref[:, c0 - CONV_CH:c1 - CONV_CH] = jnp.dot(h_ref[...], win_ref[:, c0:c1], preferred_element_type=F32)

    def project_qkv(c0):
        pr = jnp.dot(h_ref[...], win_ref[:, c0:c0 + 512], preferred_element_type=F32)
        cbuf_ref[:, CARRY:CARRY + tt, c0:c0 + 512] = pr.reshape(bb, tt, 512)
        co_ref[:, :, c0:c0 + 512] = cbuf_ref[:, keep, c0:c0 + 512]

    def conv_silu(c0):
        xs = cbuf_ref[:, :, c0:c0 + 512]
        acc = None
        for i in range(CONV_W):
            back = CONV_W - 1 - i
            shifted = xs if back == 0 else pltpu.roll(xs, back, 1)
            term = shifted[:, CARRY:, :] * convw_ref[i:i + 1, c0:c0 + 512]
            acc = term if acc is None else acc + term
        qkv_ref[:, c0:c0 + 512] = _silu(acc).reshape(rows, 512)

    nh = GDN_HEADS
    tb = min(PREP_TILES_C, n_tiles)
    causal, eye, cum = _masks(seglen, nh * tb)
    level_masks = _level_masks(seglen, nh * tb)
    eye_f = eye.astype(F32)
    neg_decay_rate = -jnp.exp(alog_ref[...])

    def gate(t):
        ba = zab_ref[pl.ds(t * TILE, TILE), 1024:1152]
        return _sigmoid(ba), _split3(neg_decay_rate * _softplus(ba + dtb_ref[...]))

    project_z(4096, N_C)
    gates = [gate(t) for t in range(n_tiles)]
    chunk_starts = list(range(0, CONV_CH, 512))
    project_qkv(chunk_starts[0])
    prefix = None
    for prev, c0 in zip(chunk_starts[:-1], chunk_starts[1:]):
        project_qkv(c0)
        if prefix is None:
            prefix = [_prefix_and_total(cum, g[1]) for g in gates]
        conv_silu(prev)
    project_z(CONV_CH, CONV_CH + 512)
    project_z(CONV_CH + 512, 4096)
    conv_silu(chunk_starts[-1])

    def prep_body(j):
        q, k, v, beta, b_c, btot_c = [], [], [], [], [], []
        for t in range(tb):
            rs = pl.ds((j * tb + t) * TILE, TILE)
            beta_all = gates[j * tb + t][0]
            b_all, btot_all = prefix[j * tb + t]
            for h in range(nh):
                q.append(qkv_ref[rs, h * DV:(h + 1) * DV])
                k.append(qkv_ref[rs, 1024 + h * DV:1024 + (h + 1) * DV])
                v.append(qkv_ref[rs, 2048 + h * DV:2048 + (h + 1) * DV])
                beta.append(beta_all[:, h:h + 1])
                b_c.append(b_all[:, nh + h:nh + h + 1])
                btot_c.append(btot_all[:, nh + h:nh + h + 1])
        q, k, v = jnp.stack(q), jnp.stack(k), jnp.stack(v)
        beta, b_c, btot_c = jnp.stack(beta), jnp.stack(b_c), jnp.stack(btot_c)
        qn = q * lax.rsqrt(jnp.sum(q * q, axis=-1, keepdims=True) + EPS) * (DV ** -0.5)
        kn = k * lax.rsqrt(jnp.sum(k * k, axis=-1, keepdims=True) + EPS)
        b_r = _col_to_row(eye, b_c)
        seg = jnp.where(causal, jnp.exp(jnp.where(causal, b_c - b_r, 0.0)), 0.0)
        k_beta = kn * beta
        v_beta = v * beta
        a_mat = _bmm_nt(k_beta, kn) * seg
        a_op = a_mat.astype(BF16)
        inv = (eye_f - jnp.where(level_masks[0], a_mat, 0.0)).astype(BF16)
        zero = jnp.zeros((), BF16)
        for lm in level_masks[1:]:
            inv = inv - jnp.where(lm, _bmm(_bmm(inv, a_op), inv).astype(BF16), zero)
        e_b = jnp.exp(b_c)
        sol = _bmm(inv, jnp.concatenate([v_beta, k_beta * e_b], axis=2))
        attn = _bmm_nt(qn, kn) * seg
        q_dec = qn * e_b
        k_out = kn * jnp.exp(btot_c - b_c)
        dec = [jnp.exp(btot_c[:, s * seglen:s * seglen + 1, :]) for s in range(nseg)]
        dec = jnp.broadcast_to(dec[0] if nseg == 1 else jnp.concatenate(dec, axis=1), (nh * tb, nseg, DV))
        for t in range(tb):
            ti = j * tb + t
            hs = slice(t * nh, (t + 1) * nh)
            u_scr[ti] = sol[hs, :, :DV]
            w_scr[ti] = sol[hs, :, DV:]
            qd_scr[ti] = q_dec[hs]
            ko_scr[ti] = k_out[hs]
            at_scr[ti] = attn[hs]
            dec_scr[ti] = dec[hs]

    assert n_tiles % tb == 0
    for j in range(n_tiles // tb):
        prep_body(j)

    def tile_body(i):
        rs = pl.ds(i * TILE, TILE)
        u, w, q_dec, k_out, attn = u_scr[i], w_scr[i], qd_scr[i], ko_scr[i], at_scr[i]
        v_new, o_inter, states = [], [], []
        for s in range(nseg):
            sr = slice(s * seglen, (s + 1) * seglen)
            st = s_in[_seq_index(i, s, seglen, bb)]
            ws = _bmm(jnp.concatenate([w[:, sr], q_dec[:, sr]], axis=1), st)
            v_new.append(u[:, sr] - ws[:, :seglen])
            o_inter.append(ws[:, seglen:])
            states.append(st)
        v_new = v_new[0] if nseg == 1 else jnp.concatenate(v_new, axis=1)
        o_inter = o_inter[0] if nseg == 1 else jnp.concatenate(o_inter, axis=1)
        o = o_inter + _bmm(attn, v_new)
        for s in range(nseg):
            sr = slice(s * seglen, (s + 1) * seglen)
            dec = dec_scr[i, :, s:s + 1, :]
            do_ref[_seq_index(i, s, seglen, bb)] = states[s] * dec + _bmm_tn(k_out[:, sr], v_new[:, sr])
        on = o * lax.rsqrt(jnp.mean(o * o, axis=-1, keepdims=True) + EPS) * gn_ref[...]
        for h in range(nh):
            z = zab_ref[rs, h * DV:(h + 1) * DV]
            og_ref[rs, h * DV:(h + 1) * DV] = (on[h] * _silu(z)).astype(BF16)

    for i in range(n_tiles):
        tile_body(i)
    _out_project(x_ref, mod_ref, gpost_ref, og_ref, wout_ref, y_ref, bb, tt)


def _c_call(x, mod, states, weights, *, bb, tt):
    nb_total, t_total, _ = x.shape
    has_state = states is not None
    seglen = min(TILE, tt)
    rows = bb * tt
    assert nb_total % bb == 0 and t_total % tt == 0 and rows % TILE == 0 and TILE % seglen == 0
    assert tt % 8 == 0 and tt >= CONV_W - 1 and (bb == 1 or tt == seglen)
    grid = (nb_total // bb, t_total // tt)
    assert not has_state or grid[1] == 1
    const2 = lambda b, t: (0, 0)
    sd_spec = pl.BlockSpec((bb, GDN_HEADS, DV, DV), lambda b, t: (b, 0, 0, 0))
    sc_spec = pl.BlockSpec((bb, CONV_W - 1, CONV_CH), lambda b, t: (b, 0, 0))
    in_specs = [
        pl.BlockSpec((bb, tt, D_MODEL), lambda b, t: (b, t, 0)),
        pl.BlockSpec((bb, 1, 3 * D_MODEL), lambda b, t: (b, 0, 0)),
    ]
    if has_state:
        in_specs += [sd_spec, sc_spec]
    in_specs += [pl.BlockSpec(w.shape, const2, pipeline_mode=pl.Buffered(1)) for w in weights]
    args = (x, mod) + (tuple(states) if has_state else ()) + tuple(weights)
    return pl.pallas_call(
        functools.partial(_c_kernel, bb=bb, tt=tt, seglen=seglen, has_state=has_state),
        grid=grid,
        in_specs=in_specs,
        out_specs=[pl.BlockSpec((bb, tt, D_MODEL), lambda b, t: (b, t, 0)), sd_spec, sc_spec],
        out_shape=[jax.ShapeDtypeStruct(x.shape, F32),
                   jax.ShapeDtypeStruct((nb_total, GDN_HEADS, DV, DV), F32),
                   jax.ShapeDtypeStruct((nb_total, CONV_W - 1, CONV_CH), F32)],
        scratch_shapes=[pltpu.VMEM((rows, D_MODEL), BF16),
                        pltpu.VMEM((bb, CARRY + tt, CONV_CH), F32),
                        pltpu.VMEM((rows, CONV_CH), F32),
                        pltpu.VMEM((rows, N_C - CONV_CH), F32),
                        pltpu.VMEM((rows, D_MODEL), BF16)]
                       + [pltpu.VMEM((rows // TILE, GDN_HEADS, TILE, DV), F32)] * 4
                       + [pltpu.VMEM((rows // TILE, GDN_HEADS, TILE, TILE), F32),
                          pltpu.VMEM((rows // TILE, GDN_HEADS, TILE // seglen, DV), F32)],
        compiler_params=pltpu.CompilerParams(dimension_semantics=("arbitrary", "arbitrary"),
                                             vmem_limit_bytes=VMEM_LIMIT_BYTES),
        name="gdn_layer_" + ("sample" if has_state else "prompt"),
    )(*args)


def _rope_tables(pos):
    half = DK_AB // 2
    inv = ROPE_BASE ** (-jnp.arange(half, dtype=F32) / half)
    ang = pos.astype(F32)[:, None] * inv[None, :]
    cos, sin = jnp.cos(ang), jnp.sin(ang)
    cos_h = jnp.concatenate([cos, cos], axis=-1)
    sin_h = jnp.concatenate([-sin, sin], axis=-1)
    return jnp.tile(cos_h, (1, 4)), jnp.tile(sin_h, (1, 4))


def _row(a):
    return a.reshape(1, -1).astype(F32)


def _prep_ab_kernel(w_ref, o_ref):
    def cols(r0, r1):
        return w_ref[r0:r1, :].T.astype(BF16)

    lane = lax.broadcasted_iota(jnp.int32, (o_ref.shape[0], GLA_RANK_PAD), 1)
    o_ref[:, AB_GQ:AB_RQ] = cols(0, 512)
    o_ref[:, AB_RQ:AB_GATE] = cols(1552, 2064)
    o_ref[:, AB_GATE:AB_GV] = jnp.where(lane < 16, w_ref[1536:1536 + GLA_RANK_PAD, :].T, 0.0).astype(BF16)
    o_ref[:, AB_GV:AB_RV] = cols(512, 1536)
    o_ref[:, AB_RV:N_AB] = cols(2064, 3088)


def _prep_c_kernel(w_ref, o_ref):
    for r0 in range(0, 4096, 1024):
        o_ref[:, r0:r0 + 1024] = w_ref[r0:r0 + 1024, :].T.astype(BF16)
    pad = jnp.zeros((128 - 2 * GDN_HEADS, w_ref.shape[1]), F32)
    o_ref[:, 4096:N_C] = jnp.concatenate([w_ref[4096:4096 + 2 * GDN_HEADS, :], pad], axis=0).T.astype(BF16)


def _weight_prep_call(body, w, n_out):
    _, k, n_in = w.shape
    kb = 256
    return pl.pallas_call(
        body,
        grid=(k // kb,),
        in_specs=[pl.BlockSpec((n_in, kb), lambda i: (0, i))],
        out_specs=pl.BlockSpec((kb, n_out), lambda i: (i, 0)),
        out_shape=jax.ShapeDtypeStruct((k, n_out), BF16),
        compiler_params=pltpu.CompilerParams(dimension_semantics=("arbitrary",),
                                             vmem_limit_bytes=VMEM_LIMIT_BYTES),
        name="weight_prep",
    )(w[0].T)


def kernel(x_prompt, x_sample, state_gla, state_ret, state_delta, state_conv, c_prompt, c_sample, w_ada, b_ada, g_pre, g_post, w_in_ab, w_gla_gate_up, b_gla_gate, g_norm_gla, g_norm_ret, w_out_ab, w_in_c, conv_c, a_log_c, dt_bias_c, g_norm_c, w_out_c):
    nbp, t_p, _ = x_prompt.shape
    nbs, t_s, _ = x_sample.shape
    assert w_ada.shape[0] == 2 and w_in_ab.shape[0] == 1

    mod = _ada_call(jnp.concatenate([c_prompt, c_sample], axis=0), w_ada, b_ada)
    mod_p = mod[:, :nbp, None, :]
    mod_s = mod[:, nbp:, None, :]

    w_ab = _weight_prep_call(_prep_ab_kernel, w_in_ab, N_AB)
    wgu = jnp.pad(w_gla_gate_up[0], ((0, GLA_RANK_PAD - 16), (0, 0))).astype(BF16)
    lgam = jnp.repeat(jnp.log1p(-jnp.exp2(-5.0 - jnp.arange(4, dtype=F32))), DK_AB).reshape(1, 256)
    weights_ab = (_row(g_pre[0]), w_ab, wgu, _row(b_gla_gate[0]), lgam, _row(g_norm_gla[0]),
                  _row(g_norm_ret[0]), w_out_ab[0].astype(BF16), _row(g_post[0]))

    w_c = _weight_prep_call(_prep_c_kernel, w_in_c, N_C)
    head_lanes = ((0, 0), (GDN_HEADS, 128 - 2 * GDN_HEADS))
    weights_c = (_row(g_pre[1]), w_c, conv_c[0], jnp.pad(_row(a_log_c[0]), head_lanes),
                 jnp.pad(_row(dt_bias_c[0]), head_lanes), _row(g_norm_c[0]), w_out_c[0].astype(BF16),
                 _row(g_post[1]))

    cos_p, sin_p = _rope_tables(jnp.arange(t_p, dtype=jnp.int32))
    cos_s, sin_s = _rope_tables(PAST_LEN + jnp.arange(t_s, dtype=jnp.int32))
    bb_s = TILE // t_s
    bb_ab = 2 * bb_s
    cos_s, sin_s = jnp.tile(cos_s, (bb_ab, 1)), jnp.tile(sin_s, (bb_ab, 1))

    tt_p = 256
    xp, gla_p, ret_p = _ab_call(x_prompt, mod_p[0], cos_p, sin_p, None, weights_ab, bb=1, tt=tt_p)
    y_prompt, delta_p, conv_p = _c_call(xp, mod_p[1], None, weights_c, bb=1, tt=tt_p)

    xs, gla_s, ret_s = _ab_call(x_sample, mod_s[0], cos_s, sin_s,
                                (state_gla.reshape(state_gla.shape[1:]), state_ret.reshape(state_ret.shape[1:])),
                                weights_ab, bb=bb_ab, tt=t_s)
    y_sample, delta_s, conv_s = _c_call(xs, mod_s[1],
                                        (state_delta.reshape(state_delta.shape[1:]),
                                         state_conv.reshape(state_conv.shape[1:])),
                                        weights_c, bb=bb_s, tt=t_s)
    return (y_prompt, y_sample, gla_p[None], ret_p[None], delta_p[None], conv_p[None],
            gla_s[None], ret_s[None], delta_s[None], conv_s[None])
```

```python
import functools
import math

import numpy as np
import jax
import jax.numpy as jnp
from jax import lax
from jax.experimental import pallas as pl
from jax.experimental.pallas import tpu as pltpu

F32 = jnp.float32
BF16 = jnp.bfloat16

D_MODEL = 1024
EPS = 1e-6
ROPE_BASE = 10000.0
PAST_LEN = 16384
GLA_TAU = 16.0
HEADS_AB = 8
DK_AB = 64
DV = 128
GLA_RANK_PAD = 128
N_AB = 3072 + GLA_RANK_PAD
AB_GQ, AB_GK, AB_RQ, AB_RK, AB_GATE, AB_GV, AB_GZ, AB_RV, AB_RZ, _ = (0, 256, 512, 768, 1024, 1152, 1664, 2176, 2688, 3200)
AB_CHUNKS = ((1024, 1152), (0, 512), (512, 1024), (1152, 1664), (1664, 2176), (2176, 2688), (2688, 3200))
GDN_HEADS = 8
CONV_W = 4
CONV_CH = 3072
N_C = 4096 + 128
TILE = 64
CARRY = 8
PREP_TILES_AB = 4
PREP_TILES_C = 2
VMEM_LIMIT_BYTES = 52 * 1024 * 1024


def _mm(a, b):
    return jnp.dot(a.astype(BF16), b.astype(BF16), preferred_element_type=F32)


def _mm_nt(a, b):
    return lax.dot_general(a.astype(BF16), b.astype(BF16), (((1,), (1,)), ((), ())),
                           preferred_element_type=F32)


def _mm_tn(a, b):
    return lax.dot_general(a.astype(BF16), b.astype(BF16), (((0,), (0,)), ((), ())),
                           preferred_element_type=F32)


def _bmm(a, b):
    return lax.dot_general(a.astype(BF16), b.astype(BF16), (((2,), (1,)), ((0,), (0,))),
                           preferred_element_type=F32)


def _bmm_nt(a, b):
    return lax.dot_general(a.astype(BF16), b.astype(BF16), (((2,), (2,)), ((0,), (0,))),
                           preferred_element_type=F32)


def _bmm_tn(a, b):
    return lax.dot_general(a.astype(BF16), b.astype(BF16), (((1,), (1,)), ((0,), (0,))),
                           preferred_element_type=F32)


def _split3(x):
    hi = x.astype(BF16)
    r = x - hi.astype(F32)
    mid = r.astype(BF16)
    lo = (r - mid.astype(F32)).astype(BF16)
    return hi, mid, lo


def _sigmoid(x):
    return 1.0 / (1.0 + jnp.exp(-x))


def _silu(x):
    return x * _sigmoid(x)


def _softplus(x):
    return jnp.maximum(x, 0.0) + jnp.log(1.0 + jnp.exp(-jnp.abs(x)))


def _masks(seglen, nh):
    r = lax.broadcasted_iota(jnp.int32, (nh, TILE, TILE), 1)
    c = lax.broadcasted_iota(jnp.int32, (nh, TILE, TILE), 2)
    sh = int(math.log2(seglen))
    same = (r >> sh) == (c >> sh)
    causal = same & (c <= r)
    eye = r == c
    cum = jnp.concatenate([causal[0].astype(F32), same[0].astype(F32)], axis=0).astype(BF16)
    return causal, eye, cum


def _level_masks(seglen, nh):
    r = lax.broadcasted_iota(jnp.int32, (nh, TILE, TILE), 1)
    c = lax.broadcasted_iota(jnp.int32, (nh, TILE, TILE), 2)
    masks = []
    for sh in range(int(math.log2(seglen))):
        masks.append(((r >> (sh + 1)) == (c >> (sh + 1))) & (((r >> sh) & 1) == 1) & (((c >> sh) & 1) == 0))
    return masks


def _seq_index(tile, seg, seglen, bb):
    return 0 if bb == 1 else tile * (TILE // seglen) + seg


def _prefix_and_total(cum, pieces):
    hi, mid, lo = pieces
    r = (jnp.dot(cum, hi, preferred_element_type=F32) + jnp.dot(cum, mid, preferred_element_type=F32)
         + jnp.dot(cum, lo, preferred_element_type=F32))
    return r[:TILE], r[TILE:]


def _row_to_col(eye, row):
    return jnp.sum(jnp.where(eye, jnp.broadcast_to(row, eye.shape), 0.0), axis=2, keepdims=True)


def _col_to_row(eye, col):
    return jnp.sum(jnp.where(eye, jnp.broadcast_to(col, eye.shape), 0.0), axis=1, keepdims=True)


def _prenorm(x_ref, mod_ref, gpre_ref, h_ref, rows):
    x3 = x_ref[...]
    shift = mod_ref[:, :, 0:D_MODEL]
    scale = mod_ref[:, :, D_MODEL:2 * D_MODEL]
    xn = x3 * lax.rsqrt(jnp.mean(x3 * x3, axis=-1, keepdims=True) + EPS) * gpre_ref[...]
    h3 = xn * (1.0 + scale) + shift
    h_ref[...] = h3.reshape(rows, D_MODEL).astype(BF16)


def _out_project(x_ref, mod_ref, gpost_ref, og_ref, wout_ref, y_ref, bb, tt):
    gate = mod_ref[:, :, 2 * D_MODEL:3 * D_MODEL]
    y3 = jnp.dot(og_ref[...], wout_ref[...], preferred_element_type=F32).reshape(bb, tt, D_MODEL)
    yn = y3 * lax.rsqrt(jnp.mean(y3 * y3, axis=-1, keepdims=True) + EPS) * gpost_ref[...]
    y_ref[...] = x_ref[...] + gate * yn


def _ada_kernel(c_ref, w_ref, b_ref, o_ref):
    o_ref[0] = _mm(_silu(c_ref[...]), w_ref[0]) + b_ref[0]


def _ada_call(c_all, w_ada, b_ada):
    n = c_all.shape[0]
    depth = w_ada.shape[0]
    nj = 3
    return pl.pallas_call(
        _ada_kernel,
        grid=(depth, nj),
        in_specs=[
            pl.BlockSpec((n, D_MODEL), lambda l, j: (0, 0)),
            pl.BlockSpec((1, D_MODEL, D_MODEL), lambda l, j: (l, 0, j)),
            pl.BlockSpec((1, 1, D_MODEL), lambda l, j: (l, 0, j)),
        ],
        out_specs=pl.BlockSpec((1, n, D_MODEL), lambda l, j: (l, 0, j)),
        out_shape=jax.ShapeDtypeStruct((depth, n, 3 * D_MODEL), F32),
        compiler_params=pltpu.CompilerParams(dimension_semantics=("arbitrary", "arbitrary")),
        name="adaln_mod",
    )(c_all, w_ada, b_ada.reshape(depth, 1, 3 * D_MODEL))


def _ab_kernel(*refs, bb, tt, seglen, has_state):
    if has_state:
        (x_ref, mod_ref, cos_ref, sin_ref, sg_ref, sr_ref, gpre_ref, win_ref, wgu_ref, bgate_ref, lgam_ref,
         ggla_ref, gret_ref, wout_ref, gpost_ref, y_ref, glao_ref, reto_ref, h_ref, og_ref,
         qd_scr, ko_scr, dec_scr, oi_scr, *proj_refs) = refs
        s_in = (sg_ref, sr_ref)
    else:
        (x_ref, mod_ref, cos_ref, sin_ref, gpre_ref, win_ref, wgu_ref, bgate_ref, lgam_ref,
         ggla_ref, gret_ref, wout_ref, gpost_ref, y_ref, glao_ref, reto_ref, h_ref, og_ref,
         qd_scr, ko_scr, dec_scr, oi_scr, *proj_refs) = refs
        s_in = (glao_ref, reto_ref)
    s_out = (glao_ref, reto_ref)
    rows = bb * tt
    nseg = TILE // seglen
    n_tiles = rows // TILE
    t_id = pl.program_id(1)

    if not has_state:
        @pl.when(t_id == 0)
        def _():
            glao_ref[...] = jnp.zeros_like(glao_ref)
            reto_ref[...] = jnp.zeros_like(reto_ref)

    _prenorm(x_ref, mod_ref, gpre_ref, h_ref, rows)

    def project(chunks):
        for c0, c1 in chunks:
            proj_refs[AB_CHUNKS.index((c0, c1))][...] = jnp.dot(h_ref[...], win_ref[:, c0:c1],
                                                                preferred_element_type=F32)

    def pcols(rs, a, b):
        i = [c0 <= a and b <= c1 for c0, c1 in AB_CHUNKS].index(True)
        return proj_refs[i][rs, a - AB_CHUNKS[i][0]:b - AB_CHUNKS[i][0]]

    project(AB_CHUNKS[:1])

    nh = HEADS_AB
    tb = min(PREP_TILES_AB, n_tiles)
    causal, eye, cum = _masks(seglen, nh * tb)
    eye = eye[:nh]
    lane = lax.broadcasted_iota(jnp.int32, (TILE, 256), 1)
    first_half = (lane & 63) < 32
    gnorm = (ggla_ref[...], gret_ref[...])

    def heads(parts, width):
        return jnp.stack([p[:, h * width:(h + 1) * width] for p in parts for h in range(4)])

    def prep_gate(t):
        rs = pl.ds(t * TILE, TILE)
        log_a = _mm(pcols(rs, AB_GATE, AB_GV), wgu_ref[...]) + bgate_ref[...]
        log_a = (jnp.minimum(log_a, 0.0) - jnp.log(1.0 + jnp.exp(-jnp.abs(log_a)))) / GLA_TAU
        return [_split3(lg) for lg in (log_a, jnp.broadcast_to(lgam_ref[...], (TILE, 256)))]

    def prep_decay(t, prefix):
        rs = pl.ds(t * TILE, TILE)

        def pj(a, b):
            return pcols(rs, a, b)

        def rot(x):
            swapped = jnp.where(first_half, pltpu.roll(x, 256 - 32, 1), pltpu.roll(x, 32, 1))
            return x * cos_ref[rs, :] + swapped * sin_ref[rs, :]

        q_parts = (pj(AB_GQ, AB_GK) * (DK_AB ** -0.5), rot(pj(AB_RQ, AB_RK)) * (DK_AB ** -0.5))
        k_parts = (pj(AB_GK, AB_RQ), rot(pj(AB_RK, AB_GATE)))
        q_dec, k_inv = [], []
        for part in range(2):
            b, btot = prefix[part]
            q_dec.append(q_parts[part] * jnp.exp(b))
            k_inv.append(k_parts[part] * jnp.exp(-b))
            cs = slice(part * 256, (part + 1) * 256)
            qd_scr[rs, cs] = q_dec[part]
            ko_scr[rs, cs] = k_parts[part] * jnp.exp(btot - b)
            dec_scr[rs, cs] = jnp.exp(btot)
        return heads(q_dec, DK_AB), heads(k_inv, DK_AB)

    def prep_scores(j, qk):
        qd, ki = ([x[i] for x in qk[j * tb:(j + 1) * tb]] for i in range(2))
        qd, ki = (x[0] if tb == 1 else jnp.concatenate(x, axis=0) for x in (qd, ki))
        return jnp.where(causal, _bmm_nt(qd, ki), 0.0)

    def prep_intra(j, scores):
        v = []
        for t in range(tb):
            rs = pl.ds((j * tb + t) * TILE, TILE)
            v.append(heads((pcols(rs, AB_GV, AB_GZ), pcols(rs, AB_RV, AB_RZ)), DV))
        o_intra = _bmm(scores, v[0] if tb == 1 else jnp.concatenate(v, axis=0))
        for t in range(tb):
            oi_scr[j * tb + t] = o_intra[t * nh:(t + 1) * nh]

    assert n_tiles % tb == 0
    pieces = [prep_gate(t) for t in range(n_tiles)]
    project(AB_CHUNKS[1:3])
    prefix = [[_prefix_and_total(cum, p) for p in pieces[t]] for t in range(n_tiles)]
    project(AB_CHUNKS[3:])
    qk = [prep_decay(t, prefix[t]) for t in range(n_tiles)]
    scores = [prep_scores(j, qk) for j in range(n_tiles // tb)]
    for j in range(n_tiles // tb):
        prep_intra(j, scores[j])

    def tile_body(i):
        r0 = i * TILE
        rs = pl.ds(r0, TILE)
        qd = heads((qd_scr[rs, 0:256], qd_scr[rs, 256:512]), DK_AB)
        ko = heads((ko_scr[rs, 0:256], ko_scr[rs, 256:512]), DK_AB)
        v = heads((pcols(rs, AB_GV, AB_GZ), pcols(rs, AB_RV, AB_RZ)), DV)
        z = heads((pcols(rs, AB_GZ, AB_RV), pcols(rs, AB_RZ, N_AB)), DV)
        o_inter = []
        for s in range(nseg):
            sr = slice(s * seglen, (s + 1) * seglen)
            bidx = _seq_index(i, s, seglen, bb)
            st = jnp.concatenate([s_in[0][bidx], s_in[1][bidx]], axis=0)
            o_inter.append(_bmm(qd[:, sr], st))
            drow = pl.ds(r0 + s * seglen, 1)
            dec_row = heads((dec_scr[drow, 0:256], dec_scr[drow, 256:512]), DK_AB)
            st = st * _row_to_col(eye, dec_row) + _bmm_tn(ko[:, sr], v[:, sr])
            s_out[0][bidx] = st[:4]
            s_out[1][bidx] = st[4:]
        o = oi_scr[i] + (o_inter[0] if nseg == 1 else jnp.concatenate(o_inter, axis=1))
        o_g, o_r = o[:4], o[4:]
        on_g = o_g * lax.rsqrt(jnp.mean(o_g * o_g, axis=-1, keepdims=True) + EPS) * gnorm[0]
        dlt = o_r - jnp.mean(o_r, axis=-1, keepdims=True)
        on_r = dlt * lax.rsqrt(jnp.mean(dlt * dlt, axis=-1, keepdims=True) + EPS) * gnorm[1]
        og = (jnp.concatenate([on_g, on_r], axis=0) * _silu(z)).astype(BF16)
        for h in range(nh):
            og_ref[rs, h * DV:(h + 1) * DV] = og[h]

    for i in range(n_tiles):
        tile_body(i)
    _out_project(x_ref, mod_ref, gpost_ref, og_ref, wout_ref, y_ref, bb, tt)


def _ab_call(x, mod, cos, sin, states, weights, *, bb, tt):
    nb_total, t_total, _ = x.shape
    has_state = states is not None
    seglen = min(TILE, tt)
    rows = bb * tt
    assert nb_total % bb == 0 and t_total % tt == 0 and rows % TILE == 0 and TILE % seglen == 0
    assert tt % 8 == 0 and (bb == 1 or tt == seglen)
    grid = (nb_total // bb, t_total // tt)
    assert not has_state or grid[1] == 1
    const2 = lambda b, t: (0, 0)
    st_spec = pl.BlockSpec((bb, 4, DK_AB, DV), lambda b, t: (b, 0, 0, 0))
    in_specs = [
        pl.BlockSpec((bb, tt, D_MODEL), lambda b, t: (b, t, 0)),
        pl.BlockSpec((bb, 1, 3 * D_MODEL), lambda b, t: (b, 0, 0)),
    ]
    if has_state:
        in_specs += [pl.BlockSpec((rows, 256), const2)] * 2 + [st_spec, st_spec]
    else:
        in_specs += [pl.BlockSpec((tt, 256), lambda b, t: (t, 0))] * 2
    in_specs += [pl.BlockSpec(w.shape, const2, pipeline_mode=pl.Buffered(1)) for w in weights]
    args = (x, mod, cos, sin) + (tuple(states) if has_state else ()) + tuple(weights)
    st_shape = jax.ShapeDtypeStruct((nb_total, 4, DK_AB, DV), F32)
    return pl.pallas_call(
        functools.partial(_ab_kernel, bb=bb, tt=tt, seglen=seglen, has_state=has_state),
        grid=grid,
        in_specs=in_specs,
        out_specs=[pl.BlockSpec((bb, tt, D_MODEL), lambda b, t: (b, t, 0)), st_spec, st_spec],
        out_shape=[jax.ShapeDtypeStruct(x.shape, F32), st_shape, st_shape],
        scratch_shapes=[pltpu.VMEM((rows, D_MODEL), BF16), pltpu.VMEM((rows, D_MODEL), BF16)]
                       + [pltpu.VMEM((rows, HEADS_AB * DK_AB), F32)] * 3
                       + [pltpu.VMEM((rows // TILE, HEADS_AB, TILE, DV), F32)]
                       + [pltpu.VMEM((rows, c1 - c0), F32) for c0, c1 in AB_CHUNKS],
        compiler_params=pltpu.CompilerParams(dimension_semantics=("arbitrary", "arbitrary"),
                                             vmem_limit_bytes=VMEM_LIMIT_BYTES),
        name="gla_ret_layer_" + ("sample" if has_state else "prompt"),
    )(*args)


def _c_kernel(*refs, bb, tt, seglen, has_state):
    if has_state:
        (x_ref, mod_ref, sd_ref, sc_ref, gpre_ref, win_ref, convw_ref, alog_ref, dtb_ref, gn_ref, wout_ref,
         gpost_ref, y_ref, do_ref, co_ref, h_ref, cbuf_ref, qkv_ref, zab_ref, og_ref,
         u_scr, w_scr, qd_scr, ko_scr, at_scr, dec_scr) = refs
        s_in = sd_ref
    else:
        (x_ref, mod_ref, gpre_ref, win_ref, convw_ref, alog_ref, dtb_ref, gn_ref, wout_ref,
         gpost_ref, y_ref, do_ref, co_ref, h_ref, cbuf_ref, qkv_ref, zab_ref, og_ref,
         u_scr, w_scr, qd_scr, ko_scr, at_scr, dec_scr) = refs
        s_in = do_ref
    rows = bb * tt
    nseg = TILE // seglen
    n_tiles = rows // TILE
    t_id = pl.program_id(1)
    keep =slice(CARRY + tt - (CONV_W - 1), CARRY + tt)
    slot = slice(CARRY - (CONV_W - 1), CARRY)

    @pl.when(t_id > 0)
    def _():
        cbuf_ref[:, slot, :] = cbuf_ref[:, keep, :]

    @pl.when(t_id == 0)
    def _():
        cbuf_ref[:, 0:CARRY, :] = jnp.zeros((bb, CARRY, CONV_CH), F32)
        if has_state:
            cbuf_ref[:, slot, :] = sc_ref[...]
        else:
            do_ref[...] = jnp.zeros_like(do_ref)

    _prenorm(x_ref, mod_ref, gpre_ref, h_ref, rows)

    def project_z(c0, c1):
        zab_ref[:, c0 - CONV_CH:c1 - CONV_CH] = jnp.dot(h_ref[...], win_ref[:, c0:c1], preferred_element_type=F32)

    def project_qkv(c0):
        pr = jnp.dot(h_ref[...], win_ref[:, c0:c0 + 512], preferred_element_type=F32)
        cbuf_ref[:, CARRY:CARRY + tt, c0:c0 + 512] = pr.reshape(bb, tt, 512)
        co_ref[:, :, c0:c0 + 512] = cbuf_ref[:, keep, c0:c0 + 512]

    def conv_silu(c0):
        xs = cbuf_ref[:, :, c0:c0 + 512].reshape(bb, (CARRY + tt) // 8, 8, 512)
        sub = lax.broadcasted_iota(jnp.int32, (bb, tt // 8, 8, 512), 2)
        acc = None
        for i in range(CONV_W):
            back = CONV_W - 1 - i
            if back == 0:
                shifted = xs[:, 1:]
            else:
                rot = pltpu.roll(xs, back, 2)
                shifted = jnp.where(sub < back, rot[:, :-1], rot[:, 1:])
            term = shifted * convw_ref[i:i + 1, c0:c0 + 512]
            acc = term if acc is None else acc + term
        qkv_ref[:, c0:c0 + 512] = _silu(acc).reshape(rows, 512)

    nh = GDN_HEADS
    tb = min(PREP_TILES_C, n_tiles)
    causal, eye, cum = _masks(seglen, nh * tb)
    level_masks = _level_masks(seglen, nh * n_tiles)
    eye_f = eye.astype(F32)
    neg_decay_rate = -jnp.exp(alog_ref[...])

    def gate(t):
        ba = zab_ref[pl.ds(t * TILE, TILE), 1024:1152]
        return _sigmoid(ba), _split3(neg_decay_rate * _softplus(ba + dtb_ref[...]))

    project_z(4096, N_C)
    gates = [gate(t) for t in range(n_tiles)]
    chunk_starts = list(range(0, CONV_CH, 512))
    project_qkv(chunk_starts[0])
    prefix = None
    for prev, c0 in zip(chunk_starts[:-1], chunk_starts[1:]):
        project_qkv(c0)
        if prefix is None:
            prefix = [_prefix_and_total(cum, g[1]) for g in gates]
        conv_silu(prev)
    conv_silu(chunk_starts[-1])
    z_pieces = [functools.partial(project_z, c, c + 256) for c in range(CONV_CH, 4096, 256)]

    def prep_factors(j):
        q, k, v, beta, b_c, btot_c = [], [], [], [], [], []
        for t in range(tb):
            rs = pl.ds((j * tb + t) * TILE, TILE)
            beta_all = gates[j * tb + t][0]
            b_all, btot_all = prefix[j * tb + t]
            for h in range(nh):
                q.append(qkv_ref[rs, h * DV:(h + 1) * DV])
                k.append(qkv_ref[rs, 1024 + h * DV:1024 + (h + 1) * DV])
                v.append(qkv_ref[rs, 2048 + h * DV:2048 + (h + 1) * DV])
                beta.append(beta_all[:, h:h + 1])
                b_c.append(b_all[:, nh + h:nh + h + 1])
                btot_c.append(btot_all[:, nh + h:nh + h + 1])
        q, k, v = jnp.stack(q), jnp.stack(k), jnp.stack(v)
        beta, b_c, btot_c = jnp.stack(beta), jnp.stack(b_c), jnp.stack(btot_c)
        qn = q * lax.rsqrt(jnp.sum(q * q, axis=-1, keepdims=True) + EPS) * (DV ** -0.5)
        kn = k * lax.rsqrt(jnp.sum(k * k, axis=-1, keepdims=True) + EPS)
        b_r = _col_to_row(eye, b_c)
        seg = jnp.where(causal, jnp.exp(jnp.where(causal, b_c - b_r, 0.0)), 0.0)
        k_beta = kn * beta
        v_beta = v * beta
        a_mat = _bmm_nt(k_beta, kn) * seg
        inv = (eye_f - jnp.where(level_masks[0][:nh * tb], a_mat, 0.0)).astype(BF16)
        e_b = jnp.exp(b_c)
        rhs = jnp.concatenate([v_beta, k_beta * e_b], axis=2).astype(BF16)
        attn = _bmm_nt(qn, kn) * seg
        q_dec = qn * e_b
        k_out = kn * jnp.exp(btot_c - b_c)
        dec = [jnp.exp(btot_c[:, s * seglen:s * seglen + 1, :]) for s in range(nseg)]
        dec = jnp.broadcast_to(dec[0] if nseg == 1 else jnp.concatenate(dec, axis=1), (nh * tb, nseg, DV))
        for t in range(tb):
            ti = j * tb + t
            hs = slice(t * nh, (t + 1) * nh)
            qd_scr[ti] = q_dec[hs]
            ko_scr[ti] = k_out[hs]
            at_scr[ti] = attn[hs]
            dec_scr[ti] = dec[hs]
        return a_mat.astype(BF16), inv, rhs

    def solve(parts):
        a_op, inv = (jnp.concatenate([p[i] for p in parts], axis=0) if len(parts) > 1 else parts[0][i]
                     for i in range(2))
        zero = jnp.zeros((), BF16)
        for lm in level_masks[1:]:
            half = _bmm(inv, a_op)
            if z_pieces:
                z_pieces.pop(0)()
            full = _bmm(half, inv)
            if z_pieces:
                z_pieces.pop(0)()
            inv = inv - jnp.where(lm, full.astype(BF16), zero)
        while z_pieces:
            z_pieces.pop(0)()
        for j, p in enumerate(parts):
            sol = _bmm(inv[j * nh * tb:(j + 1) * nh * tb], p[2])
            for t in range(tb):
                hs = slice(t * nh, (t + 1) * nh)
                u_scr[j * tb + t] = sol[hs, :, :DV]
                w_scr[j * tb + t] = sol[hs, :, DV:]

    def tile_body(i):
        rs = pl.ds(i * TILE, TILE)
        u, w, q_dec, k_out, attn = u_scr[i], w_scr[i], qd_scr[i], ko_scr[i], at_scr[i]
        v_new, o_inter, states = [], [], []
        for s in range(nseg):
            sr = slice(s * seglen, (s + 1) * seglen)
            st = s_in[_seq_index(i, s, seglen, bb)]
            ws = _bmm(jnp.concatenate([w[:, sr], q_dec[:, sr]], axis=1), st)
            v_new.append(u[:, sr] - ws[:, :seglen])
            o_inter.append(ws[:, seglen:])
            states.append(st)
        v_new = v_new[0] if nseg == 1 else jnp.concatenate(v_new, axis=1)
        o_inter = o_inter[0] if nseg == 1 else jnp.concatenate(o_inter, axis=1)
        o = o_inter + _bmm(attn, v_new)
        for s in range(nseg):
            sr = slice(s * seglen, (s + 1) * seglen)
            dec = dec_scr[i, :, s:s + 1, :]
            do_ref[_seq_index(i, s, seglen, bb)] = states[s] * dec + _bmm_tn(k_out[:, sr], v_new[:, sr])
        on = o * lax.rsqrt(jnp.mean(o * o, axis=-1, keepdims=True) + EPS) * gn_ref[...]
        for h in range(nh):
            z = zab_ref[rs, h * DV:(h + 1) * DV]
            og_ref[rs, h * DV:(h + 1) * DV] = (on[h] * _silu(z)).astype(BF16)

    assert n_tiles % tb == 0
    solve([prep_factors(j) for j in range(n_tiles // tb)])
    for i in range(n_tiles):
        tile_body(i)
    _out_project(x_ref, mod_ref, gpost_ref, og_ref, wout_ref, y_ref, bb, tt)


def _c_call(x, mod, states, weights, *, bb, tt):
    nb_total, t_total, _ = x.shape
    has_state = states is not None
    seglen = min(TILE, tt)
    rows = bb * tt
    assert nb_total % bb == 0 and t_total % tt == 0 and rows % TILE == 0 and TILE % seglen == 0
    assert tt % 8 == 0 and tt >= CONV_W - 1 and (bb == 1 or tt == seglen)
    grid = (nb_total // bb, t_total // tt)
    assert not has_state or grid[1] == 1
    const2 = lambda b, t: (0, 0)
    sd_spec = pl.BlockSpec((bb, GDN_HEADS, DV, DV), lambda b, t: (b, 0, 0, 0))
    sc_spec = pl.BlockSpec((bb, CONV_W - 1, CONV_CH), lambda b, t: (b, 0, 0))
    in_specs = [
        pl.BlockSpec((bb, tt, D_MODEL), lambda b, t: (b, t, 0)),
        pl.BlockSpec((bb, 1, 3 * D_MODEL), lambda b, t: (b, 0, 0)),
    ]
    if has_state:
        in_specs += [sd_spec, sc_spec]
    in_specs += [pl.BlockSpec(w.shape, const2, pipeline_mode=pl.Buffered(1)) for w in weights]
    args = (x, mod) + (tuple(states) if has_state else ()) + tuple(weights)
    return pl.pallas_call(
        functools.partial(_c_kernel, bb=bb, tt=tt, seglen=seglen, has_state=has_state),
        grid=grid,
        in_specs=in_specs,
        out_specs=[pl.BlockSpec((bb, tt, D_MODEL), lambda b, t: (b, t, 0)), sd_spec, sc_spec],
        out_shape=[jax.ShapeDtypeStruct(x.shape, F32),
                   jax.ShapeDtypeStruct((nb_total, GDN_HEADS, DV, DV), F32),
                   jax.ShapeDtypeStruct((nb_total, CONV_W - 1, CONV_CH), F32)],
        scratch_shapes=[pltpu.VMEM((rows, D_MODEL), BF16),
                        pltpu.VMEM((bb, CARRY + tt, CONV_CH), F32),
                        pltpu.VMEM((rows, CONV_CH), F32),
                        pltpu.VMEM((rows, N_C - CONV_CH), F32),
                        pltpu.VMEM((rows, D_MODEL), BF16)]
                       + [pltpu.VMEM((rows // TILE, GDN_HEADS, TILE, DV), F32)] * 4
                       + [pltpu.VMEM((rows // TILE, GDN_HEADS, TILE, TILE), F32),
                          pltpu.VMEM((rows // TILE, GDN_HEADS, TILE // seglen, DV), F32)],
        compiler_params=pltpu.CompilerParams(dimension_semantics=("arbitrary", "arbitrary"),
                                             vmem_limit_bytes=VMEM_LIMIT_BYTES),
        name="gdn_layer_" + ("sample" if has_state else "prompt"),
    )(*args)


def _rope_tables(pos):
    half = DK_AB // 2
    inv = ROPE_BASE ** (-jnp.arange(half, dtype=F32) / half)
    ang = pos.astype(F32)[:, None] * inv[None, :]
    cos, sin = jnp.cos(ang), jnp.sin(ang)
    cos_h = jnp.concatenate([cos, cos], axis=-1)
    sin_h = jnp.concatenate([-sin, sin], axis=-1)
    return jnp.tile(cos_h, (1, 4)), jnp.tile(sin_h, (1, 4))


def _row(a):
    return a.reshape(1, -1).astype(F32)


def _prep_ab_kernel(w_ref, o_ref):
    def cols(r0, r1):
        return w_ref[r0:r1, :].T.astype(BF16)

    lane = lax.broadcasted_iota(jnp.int32, (o_ref.shape[0], GLA_RANK_PAD), 1)
    o_ref[:, AB_GQ:AB_RQ] = cols(0, 512)
    o_ref[:, AB_RQ:AB_GATE] = cols(1552, 2064)
    o_ref[:, AB_GATE:AB_GV] = jnp.where(lane < 16, w_ref[1536:1536 + GLA_RANK_PAD, :].T, 0.0).astype(BF16)
    o_ref[:, AB_GV:AB_RV] = cols(512, 1536)
    o_ref[:, AB_RV:N_AB] = cols(2064, 3088)


def _prep_c_kernel(w_ref, o_ref):
    for r0 in range(0, 4096, 1024):
        o_ref[:, r0:r0 + 1024] = w_ref[r0:r0 + 1024, :].T.astype(BF16)
    pad = jnp.zeros((128 - 2 * GDN_HEADS, w_ref.shape[1]), F32)
    o_ref[:, 4096:N_C] = jnp.concatenate([w_ref[4096:4096 + 2 * GDN_HEADS, :], pad], axis=0).T.astype(BF16)


def _weight_prep_call(body, w, n_out):
    _, k, n_in = w.shape
    kb = 256
    return pl.pallas_call(
        body,
        grid=(k // kb,),
        in_specs=[pl.BlockSpec((n_in, kb), lambda i: (0, i))],
        out_specs=pl.BlockSpec((kb, n_out), lambda i: (i, 0)),
        out_shape=jax.ShapeDtypeStruct((k, n_out), BF16),
        compiler_params=pltpu.CompilerParams(dimension_semantics=("arbitrary",),
                                             vmem_limit_bytes=VMEM_LIMIT_BYTES),
        name="weight_prep",
    )(w[0].T)


def kernel(x_prompt, x_sample, state_gla, state_ret, state_delta, state_conv, c_prompt, c_sample, w_ada, b_ada, g_pre, g_post, w_in_ab, w_gla_gate_up, b_gla_gate, g_norm_gla, g_norm_ret, w_out_ab, w_in_c, conv_c, a_log_c, dt_bias_c, g_norm_c, w_out_c):
    nbp, t_p, _ = x_prompt.shape
    nbs, t_s, _ = x_sample.shape
    assert w_ada.shape[0] == 2 and w_in_ab.shape[0] == 1

    mod = _ada_call(jnp.concatenate([c_prompt, c_sample], axis=0), w_ada, b_ada)
    mod_p = mod[:, :nbp, None, :]
    mod_s = mod[:, nbp:, None, :]

    w_ab = _weight_prep_call(_prep_ab_kernel, w_in_ab, N_AB)
    wgu = jnp.pad(w_gla_gate_up[0], ((0, GLA_RANK_PAD - 16), (0, 0))).astype(BF16)
    lgam = jnp.repeat(jnp.log1p(-jnp.exp2(-5.0 - jnp.arange(4, dtype=F32))), DK_AB).reshape(1, 256)
    weights_ab = (_row(g_pre[0]), w_ab, wgu, _row(b_gla_gate[0]), lgam, _row(g_norm_gla[0]),
                  _row(g_norm_ret[0]), w_out_ab[0].astype(BF16), _row(g_post[0]))

    w_c = _weight_prep_call(_prep_c_kernel, w_in_c, N_C)
    head_lanes = ((0, 0), (GDN_HEADS, 128 - 2 * GDN_HEADS))
    weights_c = (_row(g_pre[1]), w_c, conv_c[0], jnp.pad(_row(a_log_c[0]), head_lanes),
                 jnp.pad(_row(dt_bias_c[0]), head_lanes), _row(g_norm_c[0]), w_out_c[0].astype(BF16),
                 _row(g_post[1]))

    cos_p, sin_p = _rope_tables(jnp.arange(t_p, dtype=jnp.int32))
    cos_s, sin_s = _rope_tables(PAST_LEN + jnp.arange(t_s, dtype=jnp.int32))
    bb_s = TILE // t_s
    bb_ab = 2 * bb_s
    cos_s, sin_s = jnp.tile(cos_s, (bb_ab, 1)), jnp.tile(sin_s, (bb_ab, 1))

    tt_p = 256
    xp, gla_p, ret_p = _ab_call(x_prompt, mod_p[0], cos_p, sin_p, None, weights_ab, bb=1, tt=tt_p)
    y_prompt, delta_p, conv_p = _c_call(xp, mod_p[1], None, weights_c, bb=1, tt=tt_p)

    xs, gla_s, ret_s = _ab_call(x_sample, mod_s[0], cos_s, sin_s,
                                (state_gla.reshape(state_gla.shape[1:]), state_ret.reshape(state_ret.shape[1:])),
                                weights_ab, bb=bb_ab, tt=t_s)
    y_sample, delta_s, conv_s = _c_call(xs, mod_s[1],
                                        (state_delta.reshape(state_delta.shape[1:]),
                                         state_conv.reshape(state_conv.shape[1:])),
                                        weights_c, bb=bb_s, tt=t_s)
    return (y_prompt, y_sample, gla_p[None], ret_p[None], delta_p[None], conv_p[None],
            gla_s[None], ret_s[None], delta_s[None], conv_s[None])
```

```python
import functools
import math

import numpy as np
import jax
import jax.numpy as jnp
from jax import lax
from jax.experimental import pallas as pl
from jax.experimental.pallas import tpu as pltpu

F32 = jnp.float32
BF16 = jnp.bfloat16

D_MODEL = 1024
EPS = 1e-6
ROPE_BASE = 10000.0
PAST_LEN = 16384
GLA_TAU = 16.0
HEADS_AB = 8
DK_AB = 64
DV = 128
GLA_RANK_PAD = 128
N_AB = 3072 + GLA_RANK_PAD
AB_GQ, AB_GK, AB_RQ, AB_RK, AB_GATE, AB_GV, AB_GZ, AB_RV, AB_RZ, _ = (0, 256, 512, 768, 1024, 1152, 1664, 2176, 2688, 3200)
AB_CHUNKS = ((1024, 1152), (0, 512), (512, 1024), (1152, 1664), (1664, 2176), (2176, 2688), (2688, 3200))
GDN_HEADS = 8
CONV_W = 4
CONV_CH = 3072
N_C = 4096 + 128
TILE = 64
CARRY = 8
PREP_TILES_AB = 4
PREP_TILES_C = 2
VMEM_LIMIT_BYTES = 52 * 1024 * 1024


def _mm(a, b):
    return jnp.dot(a.astype(BF16), b.astype(BF16), preferred_element_type=F32)


def _mm_nt(a, b):
    return lax.dot_general(a.astype(BF16), b.astype(BF16), (((1,), (1,)), ((), ())),
                           preferred_element_type=F32)


def _mm_tn(a, b):
    return lax.dot_general(a.astype(BF16), b.astype(BF16), (((0,), (0,)), ((), ())),
                           preferred_element_type=F32)


def _bmm(a, b):
    return lax.dot_general(a.astype(BF16), b.astype(BF16), (((2,), (1,)), ((0,), (0,))),
                           preferred_element_type=F32)


def _bmm_nt(a, b):
    return lax.dot_general(a.astype(BF16), b.astype(BF16), (((2,), (2,)), ((0,), (0,))),
                           preferred_element_type=F32)


def _bmm_tn(a, b):
    return lax.dot_general(a.astype(BF16), b.astype(BF16), (((1,), (1,)), ((0,), (0,))),
                           preferred_element_type=F32)


def _split3(x):
    hi = x.astype(BF16)
    r = x - hi.astype(F32)
    mid = r.astype(BF16)
    lo = (r - mid.astype(F32)).astype(BF16)
    return hi, mid, lo


def _sigmoid(x):
    return 1.0 / (1.0 + jnp.exp(-x))


def _silu(x):
    return x * _sigmoid(x)


def _softplus(x):
    return jnp.maximum(x, 0.0) + jnp.log(1.0 + jnp.exp(-jnp.abs(x)))


def _masks(seglen, nh):
    r = lax.broadcasted_iota(jnp.int32, (nh, TILE, TILE), 1)
    c = lax.broadcasted_iota(jnp.int32, (nh, TILE, TILE), 2)
    sh = int(math.log2(seglen))
    same = (r >> sh) == (c >> sh)
    causal = same & (c <= r)
    eye = r == c
    cum = jnp.concatenate([causal[0].astype(F32), same[0].astype(F32)], axis=0).astype(BF16)
    return causal, eye, cum


def _level_masks(seglen, nh):
    r = lax.broadcasted_iota(jnp.int32, (nh, TILE, TILE), 1)
    c = lax.broadcasted_iota(jnp.int32, (nh, TILE, TILE), 2)
    masks = []
    for sh in range(int(math.log2(seglen))):
        masks.append(((r >> (sh + 1)) == (c >> (sh + 1))) & (((r >> sh) & 1) == 1) & (((c >> sh) & 1) == 0))
    return masks


def _seq_index(tile, seg, seglen, tt):
    return (tile * TILE + seg * seglen) // tt


def _lockstep_tiles(bb, tt, n_tiles):
    if tt < TILE:
        return [[i] for i in range(n_tiles)]
    per_seq = tt // TILE
    return [[p * per_seq + step for p in range(bb)] for step in range(per_seq)]


def _cat(xs, axis=0):
    return xs[0] if len(xs) == 1 else jnp.concatenate(xs, axis=axis)


def _prefix_and_total(cum, pieces):
    hi, mid, lo = pieces
    r = (jnp.dot(cum, hi, preferred_element_type=F32) + jnp.dot(cum, mid, preferred_element_type=F32)
         + jnp.dot(cum, lo, preferred_element_type=F32))
    return r[:TILE], r[TILE:]


def _row_to_col(eye, row):
    return jnp.sum(jnp.where(eye, jnp.broadcast_to(row, eye.shape), 0.0), axis=2, keepdims=True)


def _col_to_row(eye, col):
    return jnp.sum(jnp.where(eye, jnp.broadcast_to(col, eye.shape), 0.0), axis=1, keepdims=True)


def _prenorm(x_ref, mod_ref, gpre_ref, h_ref, rows):
    x3 = x_ref[...]
    shift = mod_ref[:, :, 0:D_MODEL]
    scale = mod_ref[:, :, D_MODEL:2 * D_MODEL]
    xn = x3 * lax.rsqrt(jnp.mean(x3 * x3, axis=-1, keepdims=True) + EPS) * gpre_ref[...]
    h3 = xn * (1.0 + scale) + shift
    h_ref[...] = h3.reshape(rows, D_MODEL).astype(BF16)


def _out_project(x_ref, mod_ref, gpost_ref, og_ref, wout_ref, y_ref, bb, tt):
    gate = mod_ref[:, :, 2 * D_MODEL:3 * D_MODEL]
    y3 = jnp.dot(og_ref[...], wout_ref[...], preferred_element_type=F32).reshape(bb, tt, D_MODEL)
    yn = y3 * lax.rsqrt(jnp.mean(y3 * y3, axis=-1, keepdims=True) + EPS) * gpost_ref[...]
    y_ref[...] = x_ref[...] + gate * yn


def _ada_kernel(c_ref, w_ref, b_ref, o_ref):
    o_ref[0] = _mm(_silu(c_ref[...]), w_ref[0]) + b_ref[0]


def _ada_call(c_all, w_ada, b_ada):
    n = c_all.shape[0]
    depth = w_ada.shape[0]
    nj = 3
    return pl.pallas_call(
        _ada_kernel,
        grid=(depth, nj),
        in_specs=[
            pl.BlockSpec((n, D_MODEL), lambda l, j: (0, 0)),
            pl.BlockSpec((1, D_MODEL, D_MODEL), lambda l, j: (l, 0, j)),
            pl.BlockSpec((1, 1, D_MODEL), lambda l, j: (l, 0, j)),
        ],
        out_specs=pl.BlockSpec((1, n, D_MODEL), lambda l, j: (l, 0, j)),
        out_shape=jax.ShapeDtypeStruct((depth, n, 3 * D_MODEL), F32),
        compiler_params=pltpu.CompilerParams(dimension_semantics=("arbitrary", "arbitrary")),
        name="adaln_mod",
    )(c_all, w_ada, b_ada.reshape(depth, 1, 3 * D_MODEL))


def _ab_kernel(*refs, bb, tt, seglen, has_state):
    if has_state:
        (x_ref, mod_ref, cos_ref, sin_ref, rdec_ref, sg_ref, sr_ref, gpre_ref, win_ref, wgu_ref, bgate_ref,
         ggla_ref, gret_ref, wout_ref, gpost_ref, y_ref, glao_ref, reto_ref, h_ref, og_ref,
         qd_scr, ko_scr, dec_scr, oi_scr, *proj_refs) = refs
        s_in = (sg_ref, sr_ref)
    else:
        (x_ref, mod_ref, cos_ref, sin_ref, rdec_ref, gpre_ref, win_ref, wgu_ref, bgate_ref,
         ggla_ref, gret_ref, wout_ref, gpost_ref, y_ref, glao_ref, reto_ref, h_ref, og_ref,
         qd_scr, ko_scr, dec_scr, oi_scr, *proj_refs) = refs
        s_in = (glao_ref, reto_ref)
    s_out = (glao_ref, reto_ref)
    rows = bb * tt
    nseg = TILE // seglen
    n_tiles = rows // TILE
    t_id = pl.program_id(1)

    if not has_state:
        @pl.when(t_id == 0)
        def _():
            glao_ref[...] = jnp.zeros_like(glao_ref)
            reto_ref[...] = jnp.zeros_like(reto_ref)

    _prenorm(x_ref, mod_ref, gpre_ref, h_ref, rows)

    def project(chunks):
        for c0, c1 in chunks:
            proj_refs[AB_CHUNKS.index((c0, c1))][...] = jnp.dot(h_ref[...], win_ref[:, c0:c1],
                                                                preferred_element_type=F32)

    def pcols(rs, a, b):
        i = [c0 <= a and b <= c1 for c0, c1 in AB_CHUNKS].index(True)
        return proj_refs[i][rs, a - AB_CHUNKS[i][0]:b - AB_CHUNKS[i][0]]

    project(AB_CHUNKS[:1])

    nh = HEADS_AB
    tb = min(PREP_TILES_AB, n_tiles)
    causal, eye, cum = _masks(seglen, nh * max(tb, bb if tt >= TILE else 1))
    causal = causal[:nh * tb]
    eye = eye[:nh * (bb if tt >= TILE else 1)]
    lane = lax.broadcasted_iota(jnp.int32, (TILE, 256), 1)
    first_half = (lane & 63) < 32
    gnorm = (ggla_ref[...], gret_ref[...])

    def heads(parts, width):
        return jnp.stack([p[:, h * width:(h + 1) * width] for p in parts for h in range(4)])

    def prep_gate(t):
        rs = pl.ds(t * TILE, TILE)
        log_a = _mm(pcols(rs, AB_GATE, AB_GV), wgu_ref[...]) + bgate_ref[...]
        log_a = (jnp.minimum(log_a, 0.0) - jnp.log(1.0 + jnp.exp(-jnp.abs(log_a)))) / GLA_TAU
        return _split3(log_a)

    def prep_decay(t, prefix):
        rs = pl.ds(t * TILE, TILE)

        def pj(a, b):
            return pcols(rs, a, b)

        ts = rs if tt < TILE else pl.ds(t % (tt // TILE) * TILE, TILE)

        def rot(x):
            swapped = jnp.where(first_half, pltpu.roll(x, 256 - 32, 1), pltpu.roll(x, 32, 1))
            return x * cos_ref[ts, :] + swapped * sin_ref[ts, :]

        q_parts = (pj(AB_GQ, AB_GK) * (DK_AB ** -0.5), rot(pj(AB_RQ, AB_RK)) * (DK_AB ** -0.5))
        k_parts = (pj(AB_GK, AB_RQ), rot(pj(AB_RK, AB_GATE)))
        q_dec, k_inv = [], []
        b, btot = prefix
        factors = ((jnp.exp(b), jnp.exp(-b), jnp.exp(btot - b), jnp.exp(btot)), [rdec_ref[i] for i in range(4)])
        for part in range(2):
            e_b, e_nb, e_out, e_tot = factors[part]
            q_dec.append(q_parts[part] * e_b)
            k_inv.append(k_parts[part] * e_nb)
            cs = slice(part * 256, (part + 1) * 256)
            qd_scr[rs, cs] = q_dec[part]
            ko_scr[rs, cs] = k_parts[part] * e_out
            dec_scr[rs, cs] = e_tot
        return heads(q_dec, DK_AB), heads(k_inv, DK_AB)

    def prep_scores(j, qk):
        qd, ki = ([x[i] for x in qk[j * tb:(j + 1) * tb]] for i in range(2))
        qd, ki = (x[0] if tb == 1 else jnp.concatenate(x, axis=0) for x in (qd, ki))
        return jnp.where(causal, _bmm_nt(qd, ki), 0.0)

    def prep_intra(j, scores):
        v = []
        for t in range(tb):
            rs = pl.ds((j * tb + t) * TILE, TILE)
            v.append(heads((pcols(rs, AB_GV, AB_GZ), pcols(rs, AB_RV, AB_RZ)), DV))
        o_intra = _bmm(scores, v[0] if tb == 1 else jnp.concatenate(v, axis=0))
        for t in range(tb):
            oi_scr[j * tb + t] = o_intra[t * nh:(t + 1) * nh]

    assert n_tiles % tb == 0
    pieces = [prep_gate(t) for t in range(n_tiles)]
    project(AB_CHUNKS[1:3])
    prefix = [_prefix_and_total(cum, pieces[t]) for t in range(n_tiles)]
    project(AB_CHUNKS[3:])
    qk = [prep_decay(t, prefix[t]) for t in range(n_tiles)]
    scores = [prep_scores(j, qk) for j in range(n_tiles // tb)]
    for j in range(n_tiles // tb):
        prep_intra(j, scores[j])

    def tile_body(tiles):
        rss = [pl.ds(i * TILE, TILE) for i in tiles]
        qd = _cat([heads((qd_scr[rs, 0:256], qd_scr[rs, 256:512]), DK_AB) for rs in rss])
        ko = _cat([heads((ko_scr[rs, 0:256], ko_scr[rs, 256:512]), DK_AB) for rs in rss])
        v = _cat([heads((pcols(rs, AB_GV, AB_GZ), pcols(rs, AB_RV, AB_RZ)), DV) for rs in rss])
        z = _cat([heads((pcols(rs, AB_GZ, AB_RV), pcols(rs, AB_RZ, N_AB)), DV) for rs in rss])
        o_inter = []
        for s in range(nseg):
            sr = slice(s * seglen, (s + 1) * seglen)
            seqs = [_seq_index(i, s, seglen, tt) for i in tiles]
            st = _cat([x for b in seqs for x in (s_in[0][b], s_in[1][b])])
            o_inter.append(_bmm(qd[:, sr], st))
            drows = [pl.ds(i * TILE + s * seglen, 1) for i in tiles]
            dec_row = _cat([heads((dec_scr[dr, 0:256], dec_scr[dr, 256:512]), DK_AB) for dr in drows])
            st = st * _row_to_col(eye, dec_row) + _bmm_tn(ko[:, sr], v[:, sr])
            for p, b in enumerate(seqs):
                s_out[0][b] = st[p * nh:p * nh + 4]
                s_out[1][b] = st[p * nh + 4:(p + 1) * nh]
        o = _cat([oi_scr[i] for i in tiles]) + _cat(o_inter, axis=1)
        for p, rs in enumerate(rss):
            o_g, o_r = o[p * nh:p * nh + 4], o[p * nh + 4:(p + 1) * nh]
            on_g = o_g * lax.rsqrt(jnp.mean(o_g * o_g, axis=-1, keepdims=True) + EPS) * gnorm[0]
            dlt = o_r - jnp.mean(o_r, axis=-1, keepdims=True)
            on_r = dlt * lax.rsqrt(jnp.mean(dlt * dlt, axis=-1, keepdims=True) + EPS) * gnorm[1]
            og = (jnp.concatenate([on_g, on_r], axis=0) * _silu(z[p * nh:(p + 1) * nh])).astype(BF16)
            for h in range(nh):
                og_ref[rs, h * DV:(h + 1) * DV] = og[h]

    for tiles in _lockstep_tiles(bb, tt, n_tiles):
        tile_body(tiles)
    _out_project(x_ref, mod_ref, gpost_ref, og_ref, wout_ref, y_ref, bb, tt)


def _ab_call(x, mod, cos, sin, states, weights, *, bb, tt):
    nb_total, t_total, _ = x.shape
    has_state = states is not None
    seglen = min(TILE, tt)
    rows = bb * tt
    assert nb_total % bb == 0 and t_total % tt == 0 and rows % TILE == 0 and TILE % seglen == 0
    assert tt % 8 == 0 and (tt == seglen or tt % TILE == 0)
    grid = (nb_total // bb, t_total // tt)
    assert not has_state or grid[1] == 1
    const2 = lambda b, t: (0, 0)
    st_spec = pl.BlockSpec((bb, 4, DK_AB, DV), lambda b, t: (b, 0, 0, 0))
    in_specs = [
        pl.BlockSpec((bb, tt, D_MODEL), lambda b, t: (b, t, 0)),
        pl.BlockSpec((bb, 1, 3 * D_MODEL), lambda b, t: (b, 0, 0)),
    ]
    in_specs += [pl.BlockSpec((rows, 256), const2) if has_state else pl.BlockSpec((tt, 256), lambda b, t: (t, 0))] * 2
    in_specs += [pl.BlockSpec((4, TILE, 256), lambda b, t: (0, 0, 0), pipeline_mode=pl.Buffered(1))]
    if has_state:
        in_specs += [st_spec, st_spec]
    in_specs += [pl.BlockSpec(w.shape, const2, pipeline_mode=pl.Buffered(1)) for w in weights]
    args = (x, mod, cos, sin, _retnet_decay_factors(seglen)) + (tuple(states) if has_state else ()) + tuple(weights)
    st_shape = jax.ShapeDtypeStruct((nb_total, 4, DK_AB, DV), F32)
    return pl.pallas_call(
        functools.partial(_ab_kernel, bb=bb, tt=tt, seglen=seglen, has_state=has_state),
        grid=grid,
        in_specs=in_specs,
        out_specs=[pl.BlockSpec((bb, tt, D_MODEL), lambda b, t: (b, t, 0)), st_spec, st_spec],
        out_shape=[jax.ShapeDtypeStruct(x.shape, F32), st_shape, st_shape],
        scratch_shapes=[pltpu.VMEM((rows, D_MODEL), BF16), pltpu.VMEM((rows, D_MODEL), BF16)]
                       + [pltpu.VMEM((rows, HEADS_AB * DK_AB), F32)] * 3
                       + [pltpu.VMEM((rows // TILE, HEADS_AB, TILE, DV), F32)]
                       + [pltpu.VMEM((rows, c1 - c0), F32) for c0, c1 in AB_CHUNKS],
        compiler_params=pltpu.CompilerParams(dimension_semantics=("arbitrary", "arbitrary"),
                                             vmem_limit_bytes=VMEM_LIMIT_BYTES),
        name="gla_ret_layer_" + ("sample" if has_state else "prompt"),
    )(*args)


def _c_kernel(*refs, bb, tt, seglen, has_state):
    if has_state:
        (x_ref, mod_ref, sd_ref, sc_ref, gpre_ref, win_ref, convw_ref, alog_ref, dtb_ref, gn_ref, wout_ref,
         gpost_ref, y_ref, do_ref, co_ref, h_ref, cbuf_ref, qkv_ref, zab_ref, og_ref,
         u_scr, w_scr, qd_scr, ko_scr, at_scr, dec_scr) = refs
        s_in = sd_ref
    else:
        (x_ref, mod_ref, gpre_ref, win_ref, convw_ref, alog_ref, dtb_ref, gn_ref, wout_ref,
         gpost_ref, y_ref, do_ref, co_ref, h_ref, cbuf_ref, qkv_ref, zab_ref, og_ref,
         u_scr, w_scr, qd_scr, ko_scr, at_scr, dec_scr) = refs
        s_in = do_ref
    rows = bb * tt
    nseg = TILE // seglen
    n_tiles = rows // TILE
    t_id = pl.program_id(1)
    keep =slice(CARRY + tt - (CONV_W - 1), CARRY + tt)
    slot = slice(CARRY - (CONV_W - 1), CARRY)

    @pl.when(t_id > 0)
    def _():
        cbuf_ref[:, slot, :] = cbuf_ref[:, keep, :]

    @pl.when(t_id == 0)
    def _():
        cbuf_ref[:, 0:CARRY, :] = jnp.zeros((bb, CARRY, CONV_CH), F32)
        if has_state:
            cbuf_ref[:, slot, :] = sc_ref[...]
        else:
            do_ref[...] = jnp.zeros_like(do_ref)

    _prenorm(x_ref, mod_ref, gpre_ref, h_ref, rows)

    def project_z(c0, c1):
        zab_ref[:, c0 - CONV_CH:c1 - CONV_CH] = jnp.dot(h_ref[...], win_ref[:, c0:c1], preferred_element_type=F32)

    def project_qkv(c0):
        pr = jnp.dot(h_ref[...], win_ref[:, c0:c0 + 512], preferred_element_type=F32)
        cbuf_ref[:, CARRY:CARRY + tt, c0:c0 + 512] = pr.reshape(bb, tt, 512)
        co_ref[:, :, c0:c0 + 512] = cbuf_ref[:, keep, c0:c0 + 512]

    def conv_silu(c0):
        xs = cbuf_ref[:, :, c0:c0 + 512].reshape(bb, (CARRY + tt) // 8, 8, 512)
        sub = lax.broadcasted_iota(jnp.int32, (bb, tt // 8, 8, 512), 2)
        acc = None
        for i in range(CONV_W):
            back = CONV_W - 1 - i
            if back == 0:
                shifted = xs[:, 1:]
            else:
                rot = pltpu.roll(xs, back, 2)
                shifted = jnp.where(sub < back, rot[:, :-1], rot[:, 1:])
            term = shifted * convw_ref[i:i + 1, c0:c0 + 512]
            acc = term if acc is None else acc + term
        qkv_ref[:, c0:c0 + 512] = _silu(acc).reshape(rows, 512)

    nh = GDN_HEADS
    tb = min(PREP_TILES_C, n_tiles)
    causal, eye, cum = _masks(seglen, nh * tb)
    level_masks = _level_masks(seglen, nh * n_tiles)
    eye_f = eye.astype(F32)
    neg_decay_rate = -jnp.exp(alog_ref[...])

    def gate(t):
        ba = zab_ref[pl.ds(t * TILE, TILE), 1024:1152]
        return _sigmoid(ba), _split3(neg_decay_rate * _softplus(ba + dtb_ref[...]))

    project_z(4096, N_C)
    gates = [gate(t) for t in range(n_tiles)]
    chunk_starts = list(range(0, CONV_CH, 512))
    project_qkv(chunk_starts[0])
    prefix = None
    for prev, c0 in zip(chunk_starts[:-1], chunk_starts[1:]):
        project_qkv(c0)
        if prefix is None:
            prefix = [_prefix_and_total(cum, g[1]) for g in gates]
        conv_silu(prev)
    conv_silu(chunk_starts[-1])
    z_pieces = [functools.partial(project_z, c, c + 256) for c in range(CONV_CH, 4096, 256)]

    def prep_factors(j):
        q, k, v, beta, b_c, btot_c = [], [], [], [], [], []
        for t in range(tb):
            rs = pl.ds((j * tb + t) * TILE, TILE)
            beta_all = gates[j * tb + t][0]
            b_all, btot_all = prefix[j * tb + t]
            for h in range(nh):
                q.append(qkv_ref[rs, h * DV:(h + 1) * DV])
                k.append(qkv_ref[rs, 1024 + h * DV:1024 + (h + 1) * DV])
                v.append(qkv_ref[rs, 2048 + h * DV:2048 + (h + 1) * DV])
                beta.append(beta_all[:, h:h + 1])
                b_c.append(b_all[:, nh + h:nh + h + 1])
                btot_c.append(btot_all[:, nh + h:nh + h + 1])
        q, k, v = jnp.stack(q), jnp.stack(k), jnp.stack(v)
        beta, b_c, btot_c = jnp.stack(beta), jnp.stack(b_c), jnp.stack(btot_c)
        qn = q * lax.rsqrt(jnp.sum(q * q, axis=-1, keepdims=True) + EPS) * (DV ** -0.5)
        kn = k * lax.rsqrt(jnp.sum(k * k, axis=-1, keepdims=True) + EPS)
        b_r = _col_to_row(eye, b_c)
        seg = jnp.where(causal, jnp.exp(jnp.where(causal, b_c - b_r, 0.0)), 0.0)
        k_beta = kn * beta
        v_beta = v * beta
        a_mat = _bmm_nt(k_beta, kn) * seg
        inv = (eye_f - jnp.where(level_masks[0][:nh * tb], a_mat, 0.0)).astype(BF16)
        e_b = jnp.exp(b_c)
        rhs = jnp.concatenate([v_beta, k_beta * e_b], axis=2).astype(BF16)
        attn = _bmm_nt(qn, kn) * seg
        q_dec = qn * e_b
        k_out = kn * jnp.exp(btot_c - b_c)
        dec = [jnp.exp(btot_c[:, s * seglen:s * seglen + 1, :]) for s in range(nseg)]
        dec = jnp.broadcast_to(dec[0] if nseg == 1 else jnp.concatenate(dec, axis=1), (nh * tb, nseg, DV))
        for t in range(tb):
            ti = j * tb + t
            hs = slice(t * nh, (t + 1) * nh)
            qd_scr[ti] = q_dec[hs]
            ko_scr[ti] = k_out[hs]
            at_scr[ti] = attn[hs]
            dec_scr[ti] = dec[hs]
        return a_mat.astype(BF16), inv, rhs

    def solve(parts):
        a_op, inv = (jnp.concatenate([p[i] for p in parts], axis=0) if len(parts) > 1 else parts[0][i]
                     for i in range(2))
        zero = jnp.zeros((), BF16)
        for level, lm in enumerate(level_masks[1:], start=1):
            s = 2 ** level
            compact = s % 16 == 0
            lhs = jnp.concatenate([inv[:, r:r + s] for r in range(s, TILE, 2 * s)], axis=1) if compact else inv
            half = _bmm(lhs, a_op)
            if z_pieces:
                z_pieces.pop(0)()
            full = _bmm(half, inv).astype(BF16)
            if z_pieces:
                z_pieces.pop(0)()
            if compact:
                gap = jnp.zeros((full.shape[0], s, TILE), BF16)
                full = jnp.concatenate([x for i in range(TILE // (2 * s)) for x in (gap, full[:, i * s:(i + 1) * s])],
                                       axis=1)
            inv = inv - jnp.where(lm, full, zero)
        while z_pieces:
            z_pieces.pop(0)()
        for j, p in enumerate(parts):
            sol = _bmm(inv[j * nh * tb:(j + 1) * nh * tb], p[2])
            for t in range(tb):
                hs = slice(t * nh, (t + 1) * nh)
                u_scr[j * tb + t] = sol[hs, :, :DV]
                w_scr[j * tb + t] = sol[hs, :, DV:]

    def tile_body(tiles):
        u, w, q_dec, k_out, attn = (_cat([scr[i] for i in tiles]) for scr in (u_scr, w_scr, qd_scr, ko_scr, at_scr))
        v_new, o_inter, states = [], [], []
        for s in range(nseg):
            sr = slice(s * seglen, (s + 1) * seglen)
            st = _cat([s_in[_seq_index(i, s, seglen, tt)] for i in tiles])
            ws = _bmm(jnp.concatenate([w[:, sr], q_dec[:, sr]], axis=1), st)
            v_new.append(u[:, sr] - ws[:, :seglen])
            o_inter.append(ws[:, seglen:])
            states.append(st)
        v_new = _cat(v_new, axis=1)
        o = _cat(o_inter, axis=1) + _bmm(attn, v_new)
        for s in range(nseg):
            sr = slice(s * seglen, (s + 1) * seglen)
            dec = _cat([dec_scr[i, :, s:s + 1, :] for i in tiles])
            st = states[s] * dec + _bmm_tn(k_out[:, sr], v_new[:, sr])
            for p, i in enumerate(tiles):
                do_ref[_seq_index(i, s, seglen, tt)] = st[p * nh:(p + 1) * nh]
        on = o * lax.rsqrt(jnp.mean(o * o, axis=-1, keepdims=True) + EPS) * gn_ref[...]
        for p, i in enumerate(tiles):
            rs = pl.ds(i * TILE, TILE)
            for h in range(nh):
                z = zab_ref[rs, h * DV:(h + 1) * DV]
                og_ref[rs, h * DV:(h + 1) * DV] = (on[p * nh + h] * _silu(z)).astype(BF16)

    assert n_tiles % tb == 0
    solve([prep_factors(j) for j in range(n_tiles // tb)])
    for tiles in _lockstep_tiles(bb, tt, n_tiles):
        tile_body(tiles)
    _out_project(x_ref, mod_ref, gpost_ref, og_ref, wout_ref, y_ref, bb, tt)


def _c_call(x, mod, states, weights, *, bb, tt):
    nb_total, t_total, _ = x.shape
    has_state = states is not None
    seglen = min(TILE, tt)
    rows = bb * tt
    assert nb_total % bb == 0 and t_total % tt == 0 and rows % TILE == 0 and TILE % seglen == 0
    assert tt % 8 == 0 and tt >= CONV_W - 1 and (tt == seglen or tt % TILE == 0)
    grid = (nb_total // bb, t_total // tt)
    assert not has_state or grid[1] == 1
    const2 = lambda b, t: (0, 0)
    sd_spec = pl.BlockSpec((bb, GDN_HEADS, DV, DV), lambda b, t: (b, 0, 0, 0))
    sc_spec = pl.BlockSpec((bb, CONV_W - 1, CONV_CH), lambda b, t: (b, 0, 0))
    in_specs = [
        pl.BlockSpec((bb, tt, D_MODEL), lambda b, t: (b, t, 0)),
        pl.BlockSpec((bb, 1, 3 * D_MODEL), lambda b, t: (b, 0, 0)),
    ]
    if has_state:
        in_specs += [sd_spec, sc_spec]
    in_specs += [pl.BlockSpec(w.shape, const2, pipeline_mode=pl.Buffered(1)) for w in weights]
    args = (x, mod) + (tuple(states) if has_state else ()) + tuple(weights)
    return pl.pallas_call(
        functools.partial(_c_kernel, bb=bb, tt=tt, seglen=seglen, has_state=has_state),
        grid=grid,
        in_specs=in_specs,
        out_specs=[pl.BlockSpec((bb, tt, D_MODEL), lambda b, t: (b, t, 0)), sd_spec, sc_spec],
        out_shape=[jax.ShapeDtypeStruct(x.shape, F32),
                   jax.ShapeDtypeStruct((nb_total, GDN_HEADS, DV, DV), F32),
                   jax.ShapeDtypeStruct((nb_total, CONV_W - 1, CONV_CH), F32)],
        scratch_shapes=[pltpu.VMEM((rows, D_MODEL), BF16),
                        pltpu.VMEM((bb, CARRY + tt, CONV_CH), F32),
                        pltpu.VMEM((rows, CONV_CH), F32),
                        pltpu.VMEM((rows, N_C - CONV_CH), F32),
                        pltpu.VMEM((rows, D_MODEL), BF16)]
                       + [pltpu.VMEM((rows // TILE, GDN_HEADS, TILE, DV), F32)] * 4
                       + [pltpu.VMEM((rows // TILE, GDN_HEADS, TILE, TILE), F32),
                          pltpu.VMEM((rows // TILE, GDN_HEADS, TILE // seglen, DV), F32)],
        compiler_params=pltpu.CompilerParams(dimension_semantics=("arbitrary", "arbitrary"),
                                             vmem_limit_bytes=VMEM_LIMIT_BYTES),
        name="gdn_layer_" + ("sample" if has_state else "prompt"),
    )(*args)


def _rope_tables(pos):
    half = DK_AB // 2
    inv = ROPE_BASE ** (-jnp.arange(half, dtype=F32) / half)
    ang = pos.astype(F32)[:, None] * inv[None, :]
    cos, sin = jnp.cos(ang), jnp.sin(ang)
    cos_h = jnp.concatenate([cos, cos], axis=-1)
    sin_h = jnp.concatenate([-sin, sin], axis=-1)
    return jnp.tile(cos_h, (1, 4)), jnp.tile(sin_h, (1, 4))


def _row(a):
    return a.reshape(1, -1).astype(F32)


def _retnet_decay_factors(seglen):
    lgam = jnp.repeat(jnp.log1p(-jnp.exp2(-5.0 - jnp.arange(4, dtype=F32))), DK_AB)[None, :]
    steps = (jnp.arange(TILE, dtype=F32) % seglen + 1.0)[:, None]
    b, btot = steps * lgam, seglen * jnp.broadcast_to(lgam, (TILE, 256))
    return jnp.stack([jnp.exp(b), jnp.exp(-b), jnp.exp(btot - b), jnp.exp(btot)])


def _prep_ab_kernel(w_ref, o_ref):
    def cols(r0, r1):
        return w_ref[r0:r1, :].T.astype(BF16)

    lane = lax.broadcasted_iota(jnp.int32, (o_ref.shape[0], GLA_RANK_PAD), 1)
    o_ref[:, AB_GQ:AB_RQ] = cols(0, 512)
    o_ref[:, AB_RQ:AB_GATE] = cols(1552, 2064)
    o_ref[:, AB_GATE:AB_GV] = jnp.where(lane < 16, w_ref[1536:1536 + GLA_RANK_PAD, :].T, 0.0).astype(BF16)
    o_ref[:, AB_GV:AB_RV] = cols(512, 1536)
    o_ref[:, AB_RV:N_AB] = cols(2064, 3088)


def _prep_c_kernel(w_ref, o_ref):
    for r0 in range(0, 4096, 1024):
        o_ref[:, r0:r0 + 1024] = w_ref[r0:r0 + 1024, :].T.astype(BF16)
    pad = jnp.zeros((128 - 2 * GDN_HEADS, w_ref.shape[1]), F32)
    o_ref[:, 4096:N_C] = jnp.concatenate([w_ref[4096:4096 + 2 * GDN_HEADS, :], pad], axis=0).T.astype(BF16)


def _weight_prep_call(body, w, n_out):
    _, k, n_in = w.shape
    kb = 256
    return pl.pallas_call(
        body,
        grid=(k // kb,),
        in_specs=[pl.BlockSpec((n_in, kb), lambda i: (0, i))],
        out_specs=pl.BlockSpec((kb, n_out), lambda i: (i, 0)),
        out_shape=jax.ShapeDtypeStruct((k, n_out), BF16),
        compiler_params=pltpu.CompilerParams(dimension_semantics=("arbitrary",),
                                             vmem_limit_bytes=VMEM_LIMIT_BYTES),
        name="weight_prep",
    )(w[0].T)


def kernel(x_prompt, x_sample, state_gla, state_ret, state_delta, state_conv, c_prompt, c_sample, w_ada, b_ada, g_pre, g_post, w_in_ab, w_gla_gate_up, b_gla_gate, g_norm_gla, g_norm_ret, w_out_ab, w_in_c, conv_c, a_log_c, dt_bias_c, g_norm_c, w_out_c):
    nbp, t_p, _ = x_prompt.shape
    nbs, t_s, _ = x_sample.shape
    assert w_ada.shape[0] == 2 and w_in_ab.shape[0] == 1

    mod = _ada_call(jnp.concatenate([c_prompt, c_sample], axis=0), w_ada, b_ada)
    mod_p = mod[:, :nbp, None, :]
    mod_s = mod[:, nbp:, None, :]

    w_ab = _weight_prep_call(_prep_ab_kernel, w_in_ab, N_AB)
    wgu = jnp.pad(w_gla_gate_up[0], ((0, GLA_RANK_PAD - 16), (0, 0))).astype(BF16)
    weights_ab = (_row(g_pre[0]), w_ab, wgu, _row(b_gla_gate[0]), _row(g_norm_gla[0]),
                  _row(g_norm_ret[0]), w_out_ab[0].astype(BF16), _row(g_post[0]))

    w_c = _weight_prep_call(_prep_c_kernel, w_in_c, N_C)
    head_lanes = ((0, 0), (GDN_HEADS, 128 - 2 * GDN_HEADS))
    weights_c = (_row(g_pre[1]), w_c, conv_c[0], jnp.pad(_row(a_log_c[0]), head_lanes),
                 jnp.pad(_row(dt_bias_c[0]), head_lanes), _row(g_norm_c[0]), w_out_c[0].astype(BF16),
                 _row(g_post[1]))

    cos_p, sin_p = _rope_tables(jnp.arange(t_p, dtype=jnp.int32))
    cos_s, sin_s = _rope_tables(PAST_LEN + jnp.arange(t_s, dtype=jnp.int32))
    bb_s = TILE // t_s
    bb_ab = 2 * bb_s
    cos_s, sin_s = jnp.tile(cos_s, (bb_ab, 1)), jnp.tile(sin_s, (bb_ab, 1))

    tt_p, bb_p = 256, 2
    xp, gla_p, ret_p = _ab_call(x_prompt, mod_p[0], cos_p, sin_p, None, weights_ab, bb=bb_p, tt=tt_p)
    y_prompt, delta_p, conv_p = _c_call(xp, mod_p[1], None, weights_c, bb=bb_p, tt=tt_p)

    xs, gla_s, ret_s = _ab_call(x_sample, mod_s[0], cos_s, sin_s,
                                (state_gla.reshape(state_gla.shape[1:]), state_ret.reshape(state_ret.shape[1:])),
                                weights_ab, bb=bb_ab, tt=t_s)
    y_sample, delta_s, conv_s = _c_call(xs, mod_s[1],
                                        (state_delta.reshape(state_delta.shape[1:]),
                                         state_conv.reshape(state_conv.shape[1:])),
                                        weights_c, bb=bb_s, tt=t_s)
    return (y_prompt, y_sample, gla_p[None], ret_p[None], delta_p[None], conv_p[None],
            gla_s[None], ret_s[None], delta_s[None], conv_s[None])
```

```python
import functools
import math

import numpy as np
import jax
import jax.numpy as jnp
from jax import lax
from jax.experimental import pallas as pl
from jax.experimental.pallas import tpu as pltpu

F32 = jnp.float32
BF16 = jnp.bfloat16

D_MODEL = 1024
EPS = 1e-6
ROPE_BASE = 10000.0
PAST_LEN = 16384
GLA_TAU = 16.0
HEADS_AB = 8
DK_AB = 64
DV = 128
GLA_RANK_PAD = 128
N_AB = 3072 + GLA_RANK_PAD
AB_GQ, AB_GK, AB_RQ, AB_RK, AB_GATE, AB_GV, AB_GZ, AB_RV, AB_RZ, _ = (0, 256, 512, 768, 1024, 1152, 1664, 2176, 2688, 3200)
AB_CHUNKS = ((1024, 1152), (0, 512), (512, 1024), (1152, 1664), (1664, 2176), (2176, 2688), (2688, 3200))
GDN_HEADS = 8
CONV_W = 4
CONV_CH = 3072
N_C = 4096 + 128
TILE = 64
CARRY = 8
PREP_TILES_AB = 4
PREP_TILES_C = 2
VMEM_LIMIT_BYTES = 52 * 1024 * 1024


def _mm(a, b):
    return jnp.dot(a.astype(BF16), b.astype(BF16), preferred_element_type=F32)


def _mm_nt(a, b):
    return lax.dot_general(a.astype(BF16), b.astype(BF16), (((1,), (1,)), ((), ())),
                           preferred_element_type=F32)


def _mm_tn(a, b):
    return lax.dot_general(a.astype(BF16), b.astype(BF16), (((0,), (0,)), ((), ())),
                           preferred_element_type=F32)


def _bmm(a, b):
    return lax.dot_general(a.astype(BF16), b.astype(BF16), (((2,), (1,)), ((0,), (0,))),
                           preferred_element_type=F32)


def _bmm_nt(a, b):
    return lax.dot_general(a.astype(BF16), b.astype(BF16), (((2,), (2,)), ((0,), (0,))),
                           preferred_element_type=F32)


def _bmm_tn(a, b):
    return lax.dot_general(a.astype(BF16), b.astype(BF16), (((1,), (1,)), ((0,), (0,))),
                           preferred_element_type=F32)


def _split3(x):
    hi = x.astype(BF16)
    r = x - hi.astype(F32)
    mid = r.astype(BF16)
    lo = (r - mid.astype(F32)).astype(BF16)
    return hi, mid, lo


def _sigmoid(x):
    return 1.0 / (1.0 + jnp.exp(-x))


def _silu(x):
    return x * _sigmoid(x)


def _softplus(x):
    return jnp.maximum(x, 0.0) + jnp.log(1.0 + jnp.exp(-jnp.abs(x)))


def _masks(seglen):
    r = lax.broadcasted_iota(jnp.int32, (1, TILE, TILE), 1)
    c = lax.broadcasted_iota(jnp.int32, (1, TILE, TILE), 2)
    sh = int(math.log2(seglen))
    same = (r >> sh) == (c >> sh)
    causal = same & (c <= r)
    eye = r == c
    cum = jnp.concatenate([causal[0].astype(F32), same[0].astype(F32)], axis=0).astype(BF16)
    return causal, eye, cum


def _level_masks(seglen):
    r = lax.broadcasted_iota(jnp.int32, (1, TILE, TILE), 1)
    c = lax.broadcasted_iota(jnp.int32, (1, TILE, TILE), 2)
    masks = []
    for sh in range(int(math.log2(seglen))):
        masks.append(((r >> (sh + 1)) == (c >> (sh + 1))) & (((r >> sh) & 1) == 1) & (((c >> sh) & 1) == 0))
    return masks


def _seq_index(tile, seg, seglen, tt):
    return (tile * TILE + seg * seglen) // tt


def _lockstep_tiles(bb, tt, n_tiles):
    if tt < TILE:
        return [[i] for i in range(n_tiles)]
    per_seq = tt // TILE
    return [[p * per_seq + step for p in range(bb)] for step in range(per_seq)]


def _cat(xs, axis=0):
    return xs[0] if len(xs) == 1 else jnp.concatenate(xs, axis=axis)


def _prefix_and_total(cum, pieces):
    hi, mid, lo = pieces
    r = (jnp.dot(cum, hi, preferred_element_type=F32) + jnp.dot(cum, mid, preferred_element_type=F32)
         + jnp.dot(cum, lo, preferred_element_type=F32))
    return r[:TILE], r[TILE:]


def _row_to_col(eye, row):
    return jnp.sum(jnp.where(eye, jnp.broadcast_to(row, (row.shape[0], TILE, TILE)), 0.0), axis=2, keepdims=True)


def _col_to_row(eye, col):
    return jnp.sum(jnp.where(eye, jnp.broadcast_to(col, (col.shape[0], TILE, TILE)), 0.0), axis=1, keepdims=True)


def _prenorm(x_ref, mod_ref, gpre_ref, h_ref, rows):
    x3 = x_ref[...]
    shift = mod_ref[:, :, 0:D_MODEL]
    scale = mod_ref[:, :, D_MODEL:2 * D_MODEL]
    xn = x3 * lax.rsqrt(jnp.mean(x3 * x3, axis=-1, keepdims=True) + EPS) * gpre_ref[...]
    h3 = xn * (1.0 + scale) + shift
    h_ref[...] = h3.reshape(rows, D_MODEL).astype(BF16)


def _out_project(x_ref, mod_ref, gpost_ref, og_ref, wout_ref, y_ref, bb, tt):
    gate = mod_ref[:, :, 2 * D_MODEL:3 * D_MODEL]
    y3 = jnp.dot(og_ref[...], wout_ref[...], preferred_element_type=F32).reshape(bb, tt, D_MODEL)
    yn = y3 * lax.rsqrt(jnp.mean(y3 * y3, axis=-1, keepdims=True) + EPS) * gpost_ref[...]
    y_ref[...] = x_ref[...] + gate * yn


def _ada_kernel(c_ref, w_ref, b_ref, o_ref):
    o_ref[0] = _mm(_silu(c_ref[...]), w_ref[0]) + b_ref[0]


def _ada_call(c_all, w_ada, b_ada):
    n = c_all.shape[0]
    depth = w_ada.shape[0]
    nj = 3
    return pl.pallas_call(
        _ada_kernel,
        grid=(depth, nj),
        in_specs=[
            pl.BlockSpec((n, D_MODEL), lambda l, j: (0, 0)),
            pl.BlockSpec((1, D_MODEL, D_MODEL), lambda l, j: (l, 0, j)),
            pl.BlockSpec((1, 1, D_MODEL), lambda l, j: (l, 0, j)),
        ],
        out_specs=pl.BlockSpec((1, n, D_MODEL), lambda l, j: (l, 0, j)),
        out_shape=jax.ShapeDtypeStruct((depth, n, 3 * D_MODEL), F32),
        compiler_params=pltpu.CompilerParams(dimension_semantics=("arbitrary", "arbitrary")),
        name="adaln_mod",
    )(c_all, w_ada, b_ada.reshape(depth, 1, 3 * D_MODEL))


def _ab_kernel(*refs, bb, tt, seglen, has_state):
    if has_state:
        (x_ref, mod_ref, cos_ref, sin_ref, rdec_ref, sg_ref, sr_ref, gpre_ref, win_ref, wgu_ref, bgate_ref,
         ggla_ref, gret_ref, wout_ref, gpost_ref, y_ref, glao_ref, reto_ref, h_ref, og_ref,
         qd_scr, ko_scr, dec_scr, oi_scr, *proj_refs) = refs
        s_in = (sg_ref, sr_ref)
    else:
        (x_ref, mod_ref, cos_ref, sin_ref, rdec_ref, gpre_ref, win_ref, wgu_ref, bgate_ref,
         ggla_ref, gret_ref, wout_ref, gpost_ref, y_ref, glao_ref, reto_ref, h_ref, og_ref,
         qd_scr, ko_scr, dec_scr, oi_scr, *proj_refs) = refs
        s_in = (glao_ref, reto_ref)
    s_out = (glao_ref, reto_ref)
    rows = bb * tt
    nseg = TILE // seglen
    n_tiles = rows // TILE
    t_id = pl.program_id(1)

    if not has_state:
        @pl.when(t_id == 0)
        def _():
            glao_ref[...] = jnp.zeros_like(glao_ref)
            reto_ref[...] = jnp.zeros_like(reto_ref)

    _prenorm(x_ref, mod_ref, gpre_ref, h_ref, rows)

    def project(chunks):
        for c0, c1 in chunks:
            proj_refs[AB_CHUNKS.index((c0, c1))][...] = jnp.dot(h_ref[...], win_ref[:, c0:c1],
                                                                preferred_element_type=F32)

    def pcols(rs, a, b):
        i = [c0 <= a and b <= c1 for c0, c1 in AB_CHUNKS].index(True)
        return proj_refs[i][rs, a - AB_CHUNKS[i][0]:b - AB_CHUNKS[i][0]]

    project(AB_CHUNKS[:1])

    nh = HEADS_AB
    tb = min(PREP_TILES_AB, n_tiles)
    causal, eye, cum = _masks(seglen)
    lane = lax.broadcasted_iota(jnp.int32, (TILE, 256), 1)
    first_half = (lane & 63) < 32
    gnorm = (ggla_ref[...], gret_ref[...])

    def heads(parts, width):
        return jnp.stack([p[:, h * width:(h + 1) * width] for p in parts for h in range(4)])

    def prep_gate(t):
        rs = pl.ds(t * TILE, TILE)
        log_a = _mm(pcols(rs, AB_GATE, AB_GV), wgu_ref[...]) + bgate_ref[...]
        log_a = (jnp.minimum(log_a, 0.0) - jnp.log(1.0 + jnp.exp(-jnp.abs(log_a)))) / GLA_TAU
        return _split3(log_a)

    def prep_decay(t, prefix):
        rs = pl.ds(t * TILE, TILE)

        def pj(a, b):
            return pcols(rs, a, b)

        ts = rs if tt < TILE else pl.ds(t % (tt // TILE) * TILE, TILE)

        def rot(x):
            swapped = jnp.where(first_half, pltpu.roll(x, 256 - 32, 1), pltpu.roll(x, 32, 1))
            return x * cos_ref[ts, :] + swapped * sin_ref[ts, :]

        q_parts = (pj(AB_GQ, AB_GK) * (DK_AB ** -0.5), rot(pj(AB_RQ, AB_RK)) * (DK_AB ** -0.5))
        k_parts = (pj(AB_GK, AB_RQ), rot(pj(AB_RK, AB_GATE)))
        q_dec, k_inv = [], []
        b, btot = prefix
        factors = ((jnp.exp(b), jnp.exp(-b), jnp.exp(btot - b), jnp.exp(btot)), [rdec_ref[i] for i in range(4)])
        for part in range(2):
            e_b, e_nb, e_out, e_tot = factors[part]
            q_dec.append(q_parts[part] * e_b)
            k_inv.append(k_parts[part] * e_nb)
            cs = slice(part * 256, (part + 1) * 256)
            qd_scr[rs, cs] = q_dec[part]
            ko_scr[rs, cs] = k_parts[part] * e_out
            dec_scr[rs, cs] = e_tot
        return heads(q_dec, DK_AB), heads(k_inv, DK_AB)

    def prep_scores(j, qk):
        qd, ki = ([x[i] for x in qk[j * tb:(j + 1) * tb]] for i in range(2))
        qd, ki = (x[0] if tb == 1 else jnp.concatenate(x, axis=0) for x in (qd, ki))
        return jnp.where(causal, _bmm_nt(qd, ki), 0.0)

    def prep_intra(j, scores):
        v = []
        for t in range(tb):
            rs = pl.ds((j * tb + t) * TILE, TILE)
            v.append(heads((pcols(rs, AB_GV, AB_GZ), pcols(rs, AB_RV, AB_RZ)), DV))
        o_intra = _bmm(scores, v[0] if tb == 1 else jnp.concatenate(v, axis=0))
        for t in range(tb):
            oi_scr[j * tb + t] = o_intra[t * nh:(t + 1) * nh]

    assert n_tiles % tb == 0
    pieces = [prep_gate(t) for t in range(n_tiles)]
    project(AB_CHUNKS[1:3])
    prefix = [_prefix_and_total(cum, pieces[t]) for t in range(n_tiles)]
    project(AB_CHUNKS[3:])
    qk = [prep_decay(t, prefix[t]) for t in range(n_tiles)]
    scores = [prep_scores(j, qk) for j in range(n_tiles // tb)]
    for j in range(n_tiles // tb):
        prep_intra(j, scores[j])

    def tile_body(tiles):
        rss = [pl.ds(i * TILE, TILE) for i in tiles]
        qd = _cat([heads((qd_scr[rs, 0:256], qd_scr[rs, 256:512]), DK_AB) for rs in rss])
        ko = _cat([heads((ko_scr[rs, 0:256], ko_scr[rs, 256:512]), DK_AB) for rs in rss])
        v = _cat([heads((pcols(rs, AB_GV, AB_GZ), pcols(rs, AB_RV, AB_RZ)), DV) for rs in rss])
        z = _cat([heads((pcols(rs, AB_GZ, AB_RV), pcols(rs, AB_RZ, N_AB)), DV) for rs in rss])
        o_inter = []
        for s in range(nseg):
            sr = slice(s * seglen, (s + 1) * seglen)
            seqs = [_seq_index(i, s, seglen, tt) for i in tiles]
            st = _cat([x for b in seqs for x in (s_in[0][b], s_in[1][b])])
            o_inter.append(_bmm(qd[:, sr], st))
            drows = [pl.ds(i * TILE + s * seglen, 1) for i in tiles]
            dec_row = _cat([heads((dec_scr[dr, 0:256], dec_scr[dr, 256:512]), DK_AB) for dr in drows])
            st = st * _row_to_col(eye, dec_row) + _bmm_tn(ko[:, sr], v[:, sr])
            for p, b in enumerate(seqs):
                s_out[0][b] = st[p * nh:p * nh + 4]
                s_out[1][b] = st[p * nh + 4:(p + 1) * nh]
        o = _cat([oi_scr[i] for i in tiles]) + _cat(o_inter, axis=1)
        for p, rs in enumerate(rss):
            o_g, o_r = o[p * nh:p * nh + 4], o[p * nh + 4:(p + 1) * nh]
            on_g = o_g * lax.rsqrt(jnp.mean(o_g * o_g, axis=-1, keepdims=True) + EPS) * gnorm[0]
            dlt = o_r - jnp.mean(o_r, axis=-1, keepdims=True)
            on_r = dlt * lax.rsqrt(jnp.mean(dlt * dlt, axis=-1, keepdims=True) + EPS) * gnorm[1]
            og = (jnp.concatenate([on_g, on_r], axis=0) * _silu(z[p * nh:(p + 1) * nh])).astype(BF16)
            for h in range(nh):
                og_ref[rs, h * DV:(h + 1) * DV] = og[h]

    for tiles in _lockstep_tiles(bb, tt, n_tiles):
        tile_body(tiles)
    _out_project(x_ref, mod_ref, gpost_ref, og_ref, wout_ref, y_ref, bb, tt)


def _ab_call(x, mod, cos, sin, states, weights, *, bb, tt):
    nb_total, t_total, _ = x.shape
    has_state = states is not None
    seglen = min(TILE, tt)
    rows = bb * tt
    assert nb_total % bb == 0 and t_total % tt == 0 and rows % TILE == 0 and TILE % seglen == 0
    assert tt % 8 == 0 and (tt == seglen or tt % TILE == 0)
    grid = (nb_total // bb, t_total // tt)
    assert not has_state or grid[1] == 1
    const2 = lambda b, t: (0, 0)
    st_spec = pl.BlockSpec((bb, 4, DK_AB, DV), lambda b, t: (b, 0, 0, 0))
    in_specs = [
        pl.BlockSpec((bb, tt, D_MODEL), lambda b, t: (b, t, 0)),
        pl.BlockSpec((bb, 1, 3 * D_MODEL), lambda b, t: (b, 0, 0)),
    ]
    in_specs += [pl.BlockSpec((rows, 256), const2) if has_state else pl.BlockSpec((tt, 256), lambda b, t: (t, 0))] * 2
    in_specs += [pl.BlockSpec((4, TILE, 256), lambda b, t: (0, 0, 0), pipeline_mode=pl.Buffered(1))]
    if has_state:
        in_specs += [st_spec, st_spec]
    in_specs += [pl.BlockSpec(w.shape, const2, pipeline_mode=pl.Buffered(1)) for w in weights]
    args = (x, mod, cos, sin, _retnet_decay_factors(seglen)) + (tuple(states) if has_state else ()) + tuple(weights)
    st_shape = jax.ShapeDtypeStruct((nb_total, 4, DK_AB, DV), F32)
    return pl.pallas_call(
        functools.partial(_ab_kernel, bb=bb, tt=tt, seglen=seglen, has_state=has_state),
        grid=grid,
        in_specs=in_specs,
        out_specs=[pl.BlockSpec((bb, tt, D_MODEL), lambda b, t: (b, t, 0)), st_spec, st_spec],
        out_shape=[jax.ShapeDtypeStruct(x.shape, F32), st_shape, st_shape],
        scratch_shapes=[pltpu.VMEM((rows, D_MODEL), BF16), pltpu.VMEM((rows, D_MODEL), BF16)]
                       + [pltpu.VMEM((rows, HEADS_AB * DK_AB), F32)] * 3
                       + [pltpu.VMEM((rows // TILE, HEADS_AB, TILE, DV), F32)]
                       + [pltpu.VMEM((rows, c1 - c0), F32) for c0, c1 in AB_CHUNKS],
        compiler_params=pltpu.CompilerParams(dimension_semantics=("arbitrary", "arbitrary"),
                                             vmem_limit_bytes=VMEM_LIMIT_BYTES),
        name="gla_ret_layer_" + ("sample" if has_state else "prompt"),
    )(*args)


def _c_kernel(*refs, bb, tt, seglen, has_state):
    if has_state:
        (x_ref, mod_ref, sd_ref, sc_ref, gpre_ref, win_ref, convw_ref, alog_ref, dtb_ref, gn_ref, wout_ref,
         gpost_ref, y_ref, do_ref, co_ref, h_ref, cbuf_ref, qkv_ref, zab_ref, og_ref,
         u_scr, w_scr, qd_scr, ko_scr, at_scr, dec_scr) = refs
        s_in = sd_ref
    else:
        (x_ref, mod_ref, gpre_ref, win_ref, convw_ref, alog_ref, dtb_ref, gn_ref, wout_ref,
         gpost_ref, y_ref, do_ref, co_ref, h_ref, cbuf_ref, qkv_ref, zab_ref, og_ref,
         u_scr, w_scr, qd_scr, ko_scr, at_scr, dec_scr) = refs
        s_in = do_ref
    rows = bb * tt
    nseg = TILE // seglen
    n_tiles = rows // TILE
    t_id = pl.program_id(1)
    keep =slice(CARRY + tt - (CONV_W - 1), CARRY + tt)
    slot = slice(CARRY - (CONV_W - 1), CARRY)

    @pl.when(t_id > 0)
    def _():
        cbuf_ref[:, slot, :] = cbuf_ref[:, keep, :]

    @pl.when(t_id == 0)
    def _():
        cbuf_ref[:, 0:CARRY, :] = jnp.zeros((bb, CARRY, CONV_CH), F32)
        if has_state:
            cbuf_ref[:, slot, :] = sc_ref[...]
        else:
            do_ref[...] = jnp.zeros_like(do_ref)

    _prenorm(x_ref, mod_ref, gpre_ref, h_ref, rows)

    def project_z(c0, c1):
        zab_ref[:, c0 - CONV_CH:c1 - CONV_CH] = jnp.dot(h_ref[...], win_ref[:, c0:c1], preferred_element_type=F32)

    def project_qkv(c0):
        pr = jnp.dot(h_ref[...], win_ref[:, c0:c0 + 512], preferred_element_type=F32)
        cbuf_ref[:, CARRY:CARRY + tt, c0:c0 + 512] = pr.reshape(bb, tt, 512)
        co_ref[:, :, c0:c0 + 512] = cbuf_ref[:, keep, c0:c0 + 512]

    def conv_silu(c0):
        xs = cbuf_ref[:, :, c0:c0 + 512].reshape(bb, (CARRY + tt) // 8, 8, 512)
        sub = lax.broadcasted_iota(jnp.int32, (bb, tt // 8, 8, 512), 2)
        acc = None
        for i in range(CONV_W):
            back = CONV_W - 1 - i
            if back == 0:
                shifted = xs[:, 1:]
            else:
                rot = pltpu.roll(xs, back, 2)
                shifted = jnp.where(sub < back, rot[:, :-1], rot[:, 1:])
            term = shifted * convw_ref[i:i + 1, c0:c0 + 512]
            acc = term if acc is None else acc + term
        qkv_ref[:, c0:c0 + 512] = _silu(acc).reshape(rows, 512)

    nh = GDN_HEADS
    tb = min(PREP_TILES_C, n_tiles)
    causal, eye, cum = _masks(seglen)
    level_masks = _level_masks(seglen)
    eye_f = eye.astype(F32)
    neg_decay_rate = -jnp.exp(alog_ref[...])

    def gate(t):
        ba = zab_ref[pl.ds(t * TILE, TILE), 1024:1152]
        return _sigmoid(ba), _split3(neg_decay_rate * _softplus(ba + dtb_ref[...]))

    project_z(4096, N_C)
    gates = [gate(t) for t in range(n_tiles)]
    chunk_starts = list(range(0, CONV_CH, 512))
    project_qkv(chunk_starts[0])
    prefix = None
    for prev, c0 in zip(chunk_starts[:-1], chunk_starts[1:]):
        project_qkv(c0)
        if prefix is None:
            prefix = [_prefix_and_total(cum, g[1]) for g in gates]
        conv_silu(prev)
    conv_silu(chunk_starts[-1])
    z_pieces = [functools.partial(project_z, c, c + 256) for c in range(CONV_CH, 4096, 256)]

    def prep_factors(j):
        q, k, v, beta, b_c, btot_c = [], [], [], [], [], []
        for t in range(tb):
            rs = pl.ds((j * tb + t) * TILE, TILE)
            beta_all = gates[j * tb + t][0]
            b_all, btot_all = prefix[j * tb + t]
            for h in range(nh):
                q.append(qkv_ref[rs, h * DV:(h + 1) * DV])
                k.append(qkv_ref[rs, 1024 + h * DV:1024 + (h + 1) * DV])
                v.append(qkv_ref[rs, 2048 + h * DV:2048 + (h + 1) * DV])
                beta.append(beta_all[:, h:h + 1])
                b_c.append(b_all[:, nh + h:nh + h + 1])
                btot_c.append(btot_all[:, nh + h:nh + h + 1])
        q, k, v = jnp.stack(q), jnp.stack(k), jnp.stack(v)
        beta, b_c, btot_c = jnp.stack(beta), jnp.stack(b_c), jnp.stack(btot_c)
        qn = q * lax.rsqrt(jnp.sum(q * q, axis=-1, keepdims=True) + EPS) * (DV ** -0.5)
        kn = k * lax.rsqrt(jnp.sum(k * k, axis=-1, keepdims=True) + EPS)
        b_r = _col_to_row(eye, b_c)
        seg = jnp.where(causal, jnp.exp(jnp.where(causal, b_c - b_r, 0.0)), 0.0)
        k_beta = kn * beta
        v_beta = v * beta
        a_mat = _bmm_nt(k_beta, kn) * seg
        inv = (eye_f - jnp.where(level_masks[0], a_mat, 0.0)).astype(BF16)
        e_b = jnp.exp(b_c)
        rhs = jnp.concatenate([v_beta, k_beta * e_b], axis=2).astype(BF16)
        attn = _bmm_nt(qn, kn) * seg
        q_dec = qn * e_b
        k_out = kn * jnp.exp(btot_c - b_c)
        dec = [jnp.exp(btot_c[:, s * seglen:s * seglen + 1, :]) for s in range(nseg)]
        dec = jnp.broadcast_to(dec[0] if nseg == 1 else jnp.concatenate(dec, axis=1), (nh * tb, nseg, DV))
        for t in range(tb):
            ti = j * tb + t
            hs = slice(t * nh, (t + 1) * nh)
            qd_scr[ti] = q_dec[hs]
            ko_scr[ti] = k_out[hs]
            at_scr[ti] = attn[hs]
            dec_scr[ti] = dec[hs]
        return a_mat.astype(BF16), inv, rhs

    def solve(parts):
        a_op, inv = (jnp.concatenate([p[i] for p in parts], axis=0) if len(parts) > 1 else parts[0][i]
                     for i in range(2))
        zero = jnp.zeros((), BF16)
        for level, lm in enumerate(level_masks[1:], start=1):
            s = 2 ** level
            compact = s % 16 == 0
            lhs = jnp.concatenate([inv[:, r:r + s] for r in range(s, TILE, 2 * s)], axis=1) if compact else inv
            half = _bmm(lhs, a_op)
            if z_pieces:
                z_pieces.pop(0)()
            full = _bmm(half, inv).astype(BF16)
            if z_pieces:
                z_pieces.pop(0)()
            if compact:
                gap = jnp.zeros((full.shape[0], s, TILE), BF16)
                full = jnp.concatenate([x for i in range(TILE // (2 * s)) for x in (gap, full[:, i * s:(i + 1) * s])],
                                       axis=1)
            inv = inv - jnp.where(lm, full, zero)
        while z_pieces:
            z_pieces.pop(0)()
        for j, p in enumerate(parts):
            sol = _bmm(inv[j * nh * tb:(j + 1) * nh * tb], p[2])
            for t in range(tb):
                hs = slice(t * nh, (t + 1) * nh)
                u_scr[j * tb + t] = sol[hs, :, :DV]
                w_scr[j * tb + t] = sol[hs, :, DV:]

    def tile_body(tiles):
        u, w, q_dec, k_out, attn = (_cat([scr[i] for i in tiles]) for scr in (u_scr, w_scr, qd_scr, ko_scr, at_scr))
        v_new, o_inter, states = [], [], []
        for s in range(nseg):
            sr = slice(s * seglen, (s + 1) * seglen)
            st = _cat([s_in[_seq_index(i, s, seglen, tt)] for i in tiles])
            ws = _bmm(jnp.concatenate([w[:, sr], q_dec[:, sr]], axis=1), st)
            v_new.append(u[:, sr] - ws[:, :seglen])
            o_inter.append(ws[:, seglen:])
            states.append(st)
        v_new = _cat(v_new, axis=1)
        o = _cat(o_inter, axis=1) + _bmm(attn, v_new)
        for s in range(nseg):
            sr = slice(s * seglen, (s + 1) * seglen)
            dec = _cat([dec_scr[i, :, s:s + 1, :] for i in tiles])
            st = states[s] * dec + _bmm_tn(k_out[:, sr], v_new[:, sr])
            for p, i in enumerate(tiles):
                do_ref[_seq_index(i, s, seglen, tt)] = st[p * nh:(p + 1) * nh]
        on = o * lax.rsqrt(jnp.mean(o * o, axis=-1, keepdims=True) + EPS) * gn_ref[...]
        for p, i in enumerate(tiles):
            rs = pl.ds(i * TILE, TILE)
            for h in range(nh):
                z = zab_ref[rs, h * DV:(h + 1) * DV]
                og_ref[rs, h * DV:(h + 1) * DV] = (on[p * nh + h] * _silu(z)).astype(BF16)

    assert n_tiles % tb == 0
    solve([prep_factors(j) for j in range(n_tiles // tb)])
    for tiles in _lockstep_tiles(bb, tt, n_tiles):
        tile_body(tiles)
    _out_project(x_ref, mod_ref, gpost_ref, og_ref, wout_ref, y_ref, bb, tt)


def _c_call(x, mod, states, weights, *, bb, tt):
    nb_total, t_total, _ = x.shape
    has_state = states is not None
    seglen = min(TILE, tt)
    rows = bb * tt
    assert nb_total % bb == 0 and t_total % tt == 0 and rows % TILE == 0 and TILE % seglen == 0
    assert tt % 8 == 0 and tt >= CONV_W - 1 and (tt == seglen or tt % TILE == 0)
    grid = (nb_total // bb, t_total // tt)
    assert not has_state or grid[1] == 1
    const2 = lambda b, t: (0, 0)
    sd_spec = pl.BlockSpec((bb, GDN_HEADS, DV, DV), lambda b, t: (b, 0, 0, 0))
    sc_spec = pl.BlockSpec((bb, CONV_W - 1, CONV_CH), lambda b, t: (b, 0, 0))
    in_specs = [
        pl.BlockSpec((bb, tt, D_MODEL), lambda b, t: (b, t, 0)),
        pl.BlockSpec((bb, 1, 3 * D_MODEL), lambda b, t: (b, 0, 0)),
    ]
    if has_state:
        in_specs += [sd_spec, sc_spec]
    in_specs += [pl.BlockSpec(w.shape, const2, pipeline_mode=pl.Buffered(1)) for w in weights]
    args = (x, mod) + (tuple(states) if has_state else ()) + tuple(weights)
    return pl.pallas_call(
        functools.partial(_c_kernel, bb=bb, tt=tt, seglen=seglen, has_state=has_state),
        grid=grid,
        in_specs=in_specs,
        out_specs=[pl.BlockSpec((bb, tt, D_MODEL), lambda b, t: (b, t, 0)), sd_spec, sc_spec],
        out_shape=[jax.ShapeDtypeStruct(x.shape, F32),
                   jax.ShapeDtypeStruct((nb_total, GDN_HEADS, DV, DV), F32),
                   jax.ShapeDtypeStruct((nb_total, CONV_W - 1, CONV_CH), F32)],
        scratch_shapes=[pltpu.VMEM((rows, D_MODEL), BF16),
                        pltpu.VMEM((bb, CARRY + tt, CONV_CH), F32),
                        pltpu.VMEM((rows, CONV_CH), F32),
                        pltpu.VMEM((rows, N_C - CONV_CH), F32),
                        pltpu.VMEM((rows, D_MODEL), BF16)]
                       + [pltpu.VMEM((rows // TILE, GDN_HEADS, TILE, DV), F32)] * 4
                       + [pltpu.VMEM((rows // TILE, GDN_HEADS, TILE, TILE), F32),
                          pltpu.VMEM((rows // TILE, GDN_HEADS, TILE // seglen, DV), F32)],
        compiler_params=pltpu.CompilerParams(dimension_semantics=("arbitrary", "arbitrary"),
                                             vmem_limit_bytes=VMEM_LIMIT_BYTES),
        name="gdn_layer_" + ("sample" if has_state else "prompt"),
    )(*args)


def _rope_tables(pos):
    half = DK_AB // 2
    inv = ROPE_BASE ** (-jnp.arange(half, dtype=F32) / half)
    ang = pos.astype(F32)[:, None] * inv[None, :]
    cos, sin = jnp.cos(ang), jnp.sin(ang)
    cos_h = jnp.concatenate([cos, cos], axis=-1)
    sin_h = jnp.concatenate([-sin, sin], axis=-1)
    return jnp.tile(cos_h, (1, 4)), jnp.tile(sin_h, (1, 4))


def _row(a):
    return a.reshape(1, -1).astype(F32)


def _retnet_decay_factors(seglen):
    lgam = jnp.repeat(jnp.log1p(-jnp.exp2(-5.0 - jnp.arange(4, dtype=F32))), DK_AB)[None, :]
    steps = (jnp.arange(TILE, dtype=F32) % seglen + 1.0)[:, None]
    b, btot = steps * lgam, seglen * jnp.broadcast_to(lgam, (TILE, 256))
    return jnp.stack([jnp.exp(b), jnp.exp(-b), jnp.exp(btot - b), jnp.exp(btot)])


def _prep_ab_kernel(w_ref, o_ref):
    def cols(r0, r1):
        return w_ref[r0:r1, :].T.astype(BF16)

    lane = lax.broadcasted_iota(jnp.int32, (o_ref.shape[0], GLA_RANK_PAD), 1)
    o_ref[:, AB_GQ:AB_RQ] = cols(0, 512)
    o_ref[:, AB_RQ:AB_GATE] = cols(1552, 2064)
    o_ref[:, AB_GATE:AB_GV] = jnp.where(lane < 16, w_ref[1536:1536 + GLA_RANK_PAD, :].T, 0.0).astype(BF16)
    o_ref[:, AB_GV:AB_RV] = cols(512, 1536)
    o_ref[:, AB_RV:N_AB] = cols(2064, 3088)


def _prep_c_kernel(w_ref, o_ref):
    for r0 in range(0, 4096, 1024):
        o_ref[:, r0:r0 + 1024] = w_ref[r0:r0 + 1024, :].T.astype(BF16)
    pad = jnp.zeros((128 - 2 * GDN_HEADS, w_ref.shape[1]), F32)
    o_ref[:, 4096:N_C] = jnp.concatenate([w_ref[4096:4096 + 2 * GDN_HEADS, :], pad], axis=0).T.astype(BF16)


def _weight_prep_call(body, w, n_out):
    _, k, n_in = w.shape
    kb = 256
    return pl.pallas_call(
        body,
        grid=(k // kb,),
        in_specs=[pl.BlockSpec((n_in, kb), lambda i: (0, i))],
        out_specs=pl.BlockSpec((kb, n_out), lambda i: (i, 0)),
        out_shape=jax.ShapeDtypeStruct((k, n_out), BF16),
        compiler_params=pltpu.CompilerParams(dimension_semantics=("arbitrary",),
                                             vmem_limit_bytes=VMEM_LIMIT_BYTES),
        name="weight_prep",
    )(w[0].T)


def kernel(x_prompt, x_sample, state_gla, state_ret, state_delta, state_conv, c_prompt, c_sample, w_ada, b_ada, g_pre, g_post, w_in_ab, w_gla_gate_up, b_gla_gate, g_norm_gla, g_norm_ret, w_out_ab, w_in_c, conv_c, a_log_c, dt_bias_c, g_norm_c, w_out_c):
    nbp, t_p, _ = x_prompt.shape
    nbs, t_s, _ = x_sample.shape
    assert w_ada.shape[0] == 2 and w_in_ab.shape[0] == 1

    mod = _ada_call(jnp.concatenate([c_prompt, c_sample], axis=0), w_ada, b_ada)
    mod_p = mod[:, :nbp, None, :]
    mod_s = mod[:, nbp:, None, :]

    w_ab = _weight_prep_call(_prep_ab_kernel, w_in_ab, N_AB)
    wgu = jnp.pad(w_gla_gate_up[0], ((0, GLA_RANK_PAD - 16), (0, 0))).astype(BF16)
    weights_ab = (_row(g_pre[0]), w_ab, wgu, _row(b_gla_gate[0]), _row(g_norm_gla[0]),
                  _row(g_norm_ret[0]), w_out_ab[0].astype(BF16), _row(g_post[0]))

    w_c = _weight_prep_call(_prep_c_kernel, w_in_c, N_C)
    head_lanes = ((0, 0), (GDN_HEADS, 128 - 2 * GDN_HEADS))
    weights_c = (_row(g_pre[1]), w_c, conv_c[0], jnp.pad(_row(a_log_c[0]), head_lanes),
                 jnp.pad(_row(dt_bias_c[0]), head_lanes), _row(g_norm_c[0]), w_out_c[0].astype(BF16),
                 _row(g_post[1]))

    cos_p, sin_p = _rope_tables(jnp.arange(t_p, dtype=jnp.int32))
    cos_s, sin_s = _rope_tables(PAST_LEN + jnp.arange(t_s, dtype=jnp.int32))
    bb_s = TILE // t_s
    bb_ab = 2 * bb_s
    cos_s, sin_s = jnp.tile(cos_s, (bb_ab, 1)), jnp.tile(sin_s, (bb_ab, 1))

    tt_p, bb_p = 256, 2
    xp, gla_p, ret_p = _ab_call(x_prompt, mod_p[0], cos_p, sin_p, None, weights_ab, bb=2 * bb_p, tt=tt_p)
    y_prompt, delta_p, conv_p = _c_call(xp, mod_p[1], None, weights_c, bb=bb_p, tt=tt_p)

    xs, gla_s, ret_s = _ab_call(x_sample, mod_s[0], cos_s, sin_s,
                                (state_gla.reshape(state_gla.shape[1:]), state_ret.reshape(state_ret.shape[1:])),
                                weights_ab, bb=bb_ab, tt=t_s)
    y_sample, delta_s, conv_s = _c_call(xs, mod_s[1],
                                        (state_delta.reshape(state_delta.shape[1:]),
                                         state_conv.reshape(state_conv.shape[1:])),
                                        weights_c, bb=bb_s, tt=t_s)
    return (y_prompt, y_sample, gla_p[None], ret_p[None], delta_p[None], conv_p[None],
            gla_s[None], ret_s[None], delta_s[None], conv_s[None])
```

```python
import functools
import math

import numpy as np
import jax
import jax.numpy as jnp
from jax import lax
from jax.experimental import pallas as pl
from jax.experimental.pallas import tpu as pltpu

F32 = jnp.float32
BF16 = jnp.bfloat16

D_MODEL = 1024
EPS = 1e-6
ROPE_BASE = 10000.0
PAST_LEN = 16384
GLA_TAU = 16.0
HEADS_AB = 8
DK_AB = 64
DV = 128
GLA_RANK_PAD = 128
N_AB = 3072 + GLA_RANK_PAD
AB_GQ, AB_GK, AB_RQ, AB_RK, AB_GATE, AB_GV, AB_GZ, AB_RV, AB_RZ, _ = (0, 256, 512, 768, 1024, 1152, 1664, 2176, 2688, 3200)
AB_CHUNKS = ((1024, 1152), (0, 512), (512, 1024), (1152, 1664), (1664, 2176), (2176, 2688), (2688, 3200))
GDN_HEADS = 8
CONV_W = 4
CONV_CH = 3072
N_C = 4096 + 128
TILE = 64
CARRY = 8
PREP_TILES_AB = 4
PREP_TILES_C = 2
VMEM_LIMIT_BYTES = 52 * 1024 * 1024


def _mm(a, b):
    return jnp.dot(a.astype(BF16), b.astype(BF16), preferred_element_type=F32)


def _mm_nt(a, b):
    return lax.dot_general(a.astype(BF16), b.astype(BF16), (((1,), (1,)), ((), ())),
                           preferred_element_type=F32)


def _mm_tn(a, b):
    return lax.dot_general(a.astype(BF16), b.astype(BF16), (((0,), (0,)), ((), ())),
                           preferred_element_type=F32)


def _bmm(a, b):
    return lax.dot_general(a.astype(BF16), b.astype(BF16), (((2,), (1,)), ((0,), (0,))),
                           preferred_element_type=F32)


def _bmm_nt(a, b):
    return lax.dot_general(a.astype(BF16), b.astype(BF16), (((2,), (2,)), ((0,), (0,))),
                           preferred_element_type=F32)


def _bmm_tn(a, b):
    return lax.dot_general(a.astype(BF16), b.astype(BF16), (((1,), (1,)), ((0,), (0,))),
                           preferred_element_type=F32)


def _split3(x):
    hi = x.astype(BF16)
    r = x - hi.astype(F32)
    mid = r.astype(BF16)
    lo = (r - mid.astype(F32)).astype(BF16)
    return hi, mid, lo


def _sigmoid(x):
    return 1.0 / (1.0 + jnp.exp(-x))


def _silu(x):
    return x * _sigmoid(x)


def _softplus(x):
    return jnp.maximum(x, 0.0) + jnp.log(1.0 + jnp.exp(-jnp.abs(x)))


def _masks(seglen):
    r = lax.broadcasted_iota(jnp.int32, (1, TILE, TILE), 1)
    c = lax.broadcasted_iota(jnp.int32, (1, TILE, TILE), 2)
    sh = int(math.log2(seglen))
    same = (r >> sh) == (c >> sh)
    causal = same & (c <= r)
    eye = r == c
    cum = jnp.concatenate([causal[0].astype(F32), same[0].astype(F32)], axis=0).astype(BF16)
    return causal, eye, cum


def _level_masks(seglen):
    r = lax.broadcasted_iota(jnp.int32, (1, TILE, TILE), 1)
    c = lax.broadcasted_iota(jnp.int32, (1, TILE, TILE), 2)
    masks = []
    for sh in range(int(math.log2(seglen))):
        masks.append(((r >> (sh + 1)) == (c >> (sh + 1))) & (((r >> sh) & 1) == 1) & (((c >> sh) & 1) == 0))
    return masks


def _seq_index(tile, seg, seglen, tt):
    return (tile * TILE + seg * seglen) // tt


def _lockstep_tiles(bb, tt, n_tiles):
    if tt < TILE:
        return [[i] for i in range(n_tiles)]
    per_seq = tt // TILE
    return [[p * per_seq + step for p in range(bb)] for step in range(per_seq)]


def _cat(xs, axis=0):
    return xs[0] if len(xs) == 1 else jnp.concatenate(xs, axis=axis)


def _prefix_and_total(cum, pieces):
    hi, mid, lo = pieces
    r = (jnp.dot(cum, hi, preferred_element_type=F32) + jnp.dot(cum, mid, preferred_element_type=F32)
         + jnp.dot(cum, lo, preferred_element_type=F32))
    return r[:TILE], r[TILE:]


def _row_to_col(eye, row):
    return jnp.sum(jnp.where(eye, jnp.broadcast_to(row, (row.shape[0], TILE, TILE)), 0.0), axis=2, keepdims=True)


def _col_to_row(eye, col):
    return jnp.sum(jnp.where(eye, jnp.broadcast_to(col, (col.shape[0], TILE, TILE)), 0.0), axis=1, keepdims=True)


def _prenorm(x_ref, mod_ref, gpre_ref, h_ref, rows):
    x3 = x_ref[...]
    shift = mod_ref[:, :, 0:D_MODEL]
    scale = mod_ref[:, :, D_MODEL:2 * D_MODEL]
    xn = x3 * lax.rsqrt(jnp.mean(x3 * x3, axis=-1, keepdims=True) + EPS) * gpre_ref[...]
    h3 = xn * (1.0 + scale) + shift
    h_ref[...] = h3.reshape(rows, D_MODEL).astype(BF16)


def _out_project(x_ref, mod_ref, gpost_ref, og_ref, wout_ref, y_ref, bb, tt):
    gate = mod_ref[:, :, 2 * D_MODEL:3 * D_MODEL]
    y3 = jnp.dot(og_ref[...], wout_ref[...], preferred_element_type=F32).reshape(bb, tt, D_MODEL)
    yn = y3 * lax.rsqrt(jnp.mean(y3 * y3, axis=-1, keepdims=True) + EPS) * gpost_ref[...]
    y_ref[...] = x_ref[...] + gate * yn


def _ada_kernel(c_ref, w_ref, b_ref, o_ref):
    o_ref[0] = _mm(_silu(c_ref[...]), w_ref[0]) + b_ref[0]


def _ada_call(c_all, w_ada, b_ada):
    n = c_all.shape[0]
    depth = w_ada.shape[0]
    nj = 3
    return pl.pallas_call(
        _ada_kernel,
        grid=(depth, nj),
        in_specs=[
            pl.BlockSpec((n, D_MODEL), lambda l, j: (0, 0)),
            pl.BlockSpec((1, D_MODEL, D_MODEL), lambda l, j: (l, 0, j)),
            pl.BlockSpec((1, 1, D_MODEL), lambda l, j: (l, 0, j)),
        ],
        out_specs=pl.BlockSpec((1, n, D_MODEL), lambda l, j: (l, 0, j)),
        out_shape=jax.ShapeDtypeStruct((depth, n, 3 * D_MODEL), F32),
        compiler_params=pltpu.CompilerParams(dimension_semantics=("arbitrary", "arbitrary")),
        name="adaln_mod",
    )(c_all, w_ada, b_ada.reshape(depth, 1, 3 * D_MODEL))


def _ab_kernel(*refs, bb, tt, seglen, has_state):
    if has_state:
        (x_ref, mod_ref, cos_ref, sin_ref, rdec_ref, sg_ref, sr_ref, gpre_ref, win_ref, wgu_ref, bgate_ref,
         ggla_ref, gret_ref, wout_ref, gpost_ref, y_ref, glao_ref, reto_ref, h_ref, og_ref,
         qd_scr, ko_scr, dec_scr, oi_scr, *proj_refs) = refs
        s_in = (sg_ref, sr_ref)
    else:
        (x_ref, mod_ref, cos_ref, sin_ref, rdec_ref, gpre_ref, win_ref, wgu_ref, bgate_ref,
         ggla_ref, gret_ref, wout_ref, gpost_ref, y_ref, glao_ref, reto_ref, h_ref, og_ref,
         qd_scr, ko_scr, dec_scr, oi_scr, *proj_refs) = refs
        s_in = (glao_ref, reto_ref)
    s_out = (glao_ref, reto_ref)
    rows = bb * tt
    nseg = TILE // seglen
    n_tiles = rows // TILE
    t_id = pl.program_id(1)

    if not has_state:
        @pl.when(t_id == 0)
        def _():
            glao_ref[...] = jnp.zeros_like(glao_ref)
            reto_ref[...] = jnp.zeros_like(reto_ref)

    _prenorm(x_ref, mod_ref, gpre_ref, h_ref, rows)

    def project(chunks):
        for c0, c1 in chunks:
            proj_refs[AB_CHUNKS.index((c0, c1))][...] = jnp.dot(h_ref[...], win_ref[:, c0:c1],
                                                                preferred_element_type=F32)

    def pcols(rs, a, b):
        i = [c0 <= a and b <= c1 for c0, c1 in AB_CHUNKS].index(True)
        return proj_refs[i][rs, a - AB_CHUNKS[i][0]:b - AB_CHUNKS[i][0]]

    project(AB_CHUNKS[:1])

    nh = HEADS_AB
    tb = min(PREP_TILES_AB, n_tiles)
    causal, eye, cum = _masks(seglen)
    lane = lax.broadcasted_iota(jnp.int32, (TILE, 256), 1)
    first_half = (lane & 63) < 32
    gnorm = (ggla_ref[...], gret_ref[...])

    def heads(parts, width):
        return jnp.stack([p[:, h * width:(h + 1) * width] for p in parts for h in range(4)])

    def prep_gate(t):
        rs = pl.ds(t * TILE, TILE)
        log_a = _mm(pcols(rs, AB_GATE, AB_GV), wgu_ref[...]) + bgate_ref[...]
        log_a = (jnp.minimum(log_a, 0.0) - jnp.log(1.0 + jnp.exp(-jnp.abs(log_a)))) / GLA_TAU
        return _split3(log_a)

    def prep_decay(t, prefix):
        rs = pl.ds(t * TILE, TILE)

        def pj(a, b):
            return pcols(rs, a, b)

        ts = rs if tt < TILE else pl.ds(t % (tt // TILE) * TILE, TILE)

        def rot(x):
            swapped = jnp.where(first_half, pltpu.roll(x, 256 - 32, 1), pltpu.roll(x, 32, 1))
            return x * cos_ref[ts, :] + swapped * sin_ref[ts, :]

        q_parts = (pj(AB_GQ, AB_GK) * (DK_AB ** -0.5), rot(pj(AB_RQ, AB_RK)) * (DK_AB ** -0.5))
        k_parts = (pj(AB_GK, AB_RQ), rot(pj(AB_RK, AB_GATE)))
        q_dec, k_inv = [], []
        b, btot = prefix
        factors = ((jnp.exp(b), jnp.exp(-b), jnp.exp(btot - b), jnp.exp(btot)), [rdec_ref[i] for i in range(4)])
        for part in range(2):
            e_b, e_nb, e_out, e_tot = factors[part]
            q_dec.append(q_parts[part] * e_b)
            k_inv.append(k_parts[part] * e_nb)
            cs = slice(part * 256, (part + 1) * 256)
            qd_scr[rs, cs] = q_dec[part]
            ko_scr[rs, cs] = k_parts[part] * e_out
            dec_scr[rs, cs] = e_tot
        return heads(q_dec, DK_AB), heads(k_inv, DK_AB)

    def prep_scores(j, qk):
        qd, ki = ([x[i] for x in qk[j * tb:(j + 1) * tb]] for i in range(2))
        qd, ki = (x[0] if tb == 1 else jnp.concatenate(x, axis=0) for x in (qd, ki))
        return jnp.where(causal, _bmm_nt(qd, ki), 0.0)

    def prep_intra(j, scores):
        v = []
        for t in range(tb):
            rs = pl.ds((j * tb + t) * TILE, TILE)
            v.append(heads((pcols(rs, AB_GV, AB_GZ), pcols(rs, AB_RV, AB_RZ)), DV))
        o_intra = _bmm(scores, v[0] if tb == 1 else jnp.concatenate(v, axis=0))
        for t in range(tb):
            oi_scr[j * tb + t] = o_intra[t * nh:(t + 1) * nh]

    assert n_tiles % tb == 0
    pieces = [prep_gate(t) for t in range(n_tiles)]
    project(AB_CHUNKS[1:3])
    prefix = [_prefix_and_total(cum, pieces[t]) for t in range(n_tiles)]
    project(AB_CHUNKS[3:])
    qk = [prep_decay(t, prefix[t]) for t in range(n_tiles)]
    scores = [prep_scores(j, qk) for j in range(n_tiles // tb)]
    for j in range(n_tiles // tb):
        prep_intra(j, scores[j])

    def tile_body(tiles):
        rss = [pl.ds(i * TILE, TILE) for i in tiles]
        qd = _cat([heads((qd_scr[rs, 0:256], qd_scr[rs, 256:512]), DK_AB) for rs in rss])
        ko = _cat([heads((ko_scr[rs, 0:256], ko_scr[rs, 256:512]), DK_AB) for rs in rss])
        v = _cat([heads((pcols(rs, AB_GV, AB_GZ), pcols(rs, AB_RV, AB_RZ)), DV) for rs in rss])
        z = _cat([heads((pcols(rs, AB_GZ, AB_RV), pcols(rs, AB_RZ, N_AB)), DV) for rs in rss])
        o_inter = []
        for s in range(nseg):
            sr = slice(s * seglen, (s + 1) * seglen)
            seqs = [_seq_index(i, s, seglen, tt) for i in tiles]
            st = _cat([x for b in seqs for x in (s_in[0][b], s_in[1][b])])
            o_inter.append(_bmm(qd[:, sr], st))
            drows = [pl.ds(i * TILE + s * seglen, 1) for i in tiles]
            dec_row = _cat([heads((dec_scr[dr, 0:256], dec_scr[dr, 256:512]), DK_AB) for dr in drows])
            st = st * _row_to_col(eye, dec_row) + _bmm_tn(ko[:, sr], v[:, sr])
            for p, b in enumerate(seqs):
                s_out[0][b] = st[p * nh:p * nh + 4]
                s_out[1][b] = st[p * nh + 4:(p + 1) * nh]
        o = _cat([oi_scr[i] for i in tiles]) + _cat(o_inter, axis=1)
        for p, rs in enumerate(rss):
            o_g, o_r = o[p * nh:p * nh + 4], o[p * nh + 4:(p + 1) * nh]
            on_g = o_g * lax.rsqrt(jnp.mean(o_g * o_g, axis=-1, keepdims=True) + EPS) * gnorm[0]
            dlt = o_r - jnp.mean(o_r, axis=-1, keepdims=True)
            on_r = dlt * lax.rsqrt(jnp.mean(dlt * dlt, axis=-1, keepdims=True) + EPS) * gnorm[1]
            og = (jnp.concatenate([on_g, on_r], axis=0) * _silu(z[p * nh:(p + 1) * nh])).astype(BF16)
            for h in range(nh):
                og_ref[rs, h * DV:(h + 1) * DV] = og[h]

    for tiles in _lockstep_tiles(bb, tt, n_tiles):
        tile_body(tiles)
    _out_project(x_ref, mod_ref, gpost_ref, og_ref, wout_ref, y_ref, bb, tt)


def _ab_call(x, mod, cos, sin, states, weights, *, bb, tt):
    nb_total, t_total, _ = x.shape
    has_state = states is not None
    seglen = min(TILE, tt)
    rows = bb * tt
    assert nb_total % bb == 0 and t_total % tt == 0 and rows % TILE == 0 and TILE % seglen == 0
    assert tt % 8 == 0 and (tt == seglen or tt % TILE == 0)
    grid = (nb_total // bb, t_total // tt)
    assert not has_state or grid[1] == 1
    const2 = lambda b, t: (0, 0)
    st_spec = pl.BlockSpec((bb, 4, DK_AB, DV), lambda b, t: (b, 0, 0, 0))
    in_specs = [
        pl.BlockSpec((bb, tt, D_MODEL), lambda b, t: (b, t, 0)),
        pl.BlockSpec((bb, 1, 3 * D_MODEL), lambda b, t: (b, 0, 0)),
    ]
    in_specs += [pl.BlockSpec((rows, 256), const2) if has_state else pl.BlockSpec((tt, 256), lambda b, t: (t, 0))] * 2
    in_specs += [pl.BlockSpec((4, TILE, 256), lambda b, t: (0, 0, 0), pipeline_mode=pl.Buffered(1))]
    if has_state:
        in_specs += [st_spec, st_spec]
    in_specs += [pl.BlockSpec(w.shape, const2, pipeline_mode=pl.Buffered(1)) for w in weights]
    args = (x, mod, cos, sin, _retnet_decay_factors(seglen)) + (tuple(states) if has_state else ()) + tuple(weights)
    st_shape = jax.ShapeDtypeStruct((nb_total, 4, DK_AB, DV), F32)
    return pl.pallas_call(
        functools.partial(_ab_kernel, bb=bb, tt=tt, seglen=seglen, has_state=has_state),
        grid=grid,
        in_specs=in_specs,
        out_specs=[pl.BlockSpec((bb, tt, D_MODEL), lambda b, t: (b, t, 0)), st_spec, st_spec],
        out_shape=[jax.ShapeDtypeStruct(x.shape, F32), st_shape, st_shape],
        scratch_shapes=[pltpu.VMEM((rows, D_MODEL), BF16), pltpu.VMEM((rows, D_MODEL), BF16)]
                       + [pltpu.VMEM((rows, HEADS_AB * DK_AB), F32)] * 3
                       + [pltpu.VMEM((rows // TILE, HEADS_AB, TILE, DV), F32)]
                       + [pltpu.VMEM((rows, c1 - c0), F32) for c0, c1 in AB_CHUNKS],
        compiler_params=pltpu.CompilerParams(dimension_semantics=("arbitrary", "arbitrary"),
                                             vmem_limit_bytes=VMEM_LIMIT_BYTES),
        name="gla_ret_layer_" + ("sample" if has_state else "prompt"),
    )(*args)


def _c_kernel(*refs, bb, tt, seglen, has_state):
    if has_state:
        (x_ref, mod_ref, sd_ref, sc_ref, gpre_ref, win_ref, convw_ref, alog_ref, dtb_ref, gn_ref, wout_ref,
         gpost_ref, y_ref, do_ref, co_ref, h_ref, cbuf_ref, qkv_ref, zab_ref, og_ref,
         u_scr, w_scr, qd_scr, ko_scr, at_scr, dec_scr) = refs
        s_in = sd_ref
    else:
        (x_ref, mod_ref, gpre_ref, win_ref, convw_ref, alog_ref, dtb_ref, gn_ref, wout_ref,
         gpost_ref, y_ref, do_ref, co_ref, h_ref, cbuf_ref, qkv_ref, zab_ref, og_ref,
         u_scr, w_scr, qd_scr, ko_scr, at_scr, dec_scr) = refs
        s_in = do_ref
    rows = bb * tt
    nseg = TILE // seglen
    n_tiles = rows // TILE
    t_id = pl.program_id(1)
    keep =slice(CARRY + tt - (CONV_W - 1), CARRY + tt)
    slot = slice(CARRY - (CONV_W - 1), CARRY)

    @pl.when(t_id > 0)
    def _():
        cbuf_ref[:, slot, :] = cbuf_ref[:, keep, :]

    @pl.when(t_id == 0)
    def _():
        cbuf_ref[:, 0:CARRY, :] = jnp.zeros((bb, CARRY, CONV_CH), F32)
        if has_state:
            for j in range(CONV_W - 1):
                cbuf_ref[:, CARRY - (CONV_W - 1) + j, :] = sc_ref[j]
        else:
            do_ref[...] = jnp.zeros_like(do_ref)

    _prenorm(x_ref, mod_ref, gpre_ref, h_ref, rows)

    def project_z(c0, c1):
        zab_ref[:, c0 - CONV_CH:c1 - CONV_CH] = jnp.dot(h_ref[...], win_ref[:, c0:c1], preferred_element_type=F32)

    def project_qkv(c0):
        pr = jnp.dot(h_ref[...], win_ref[:, c0:c0 + 512], preferred_element_type=F32)
        cbuf_ref[:, CARRY:CARRY + tt, c0:c0 + 512] = pr.reshape(bb, tt, 512)
        if has_state:
            for j in range(CONV_W - 1):
                co_ref[j, :, c0:c0 + 512] = cbuf_ref[:, CARRY + tt - (CONV_W - 1) + j, c0:c0 + 512]
        else:
            co_ref[:, :, c0:c0 + 512] = cbuf_ref[:, keep, c0:c0 + 512]

    def conv_silu(c0):
        xs = cbuf_ref[:, :, c0:c0 + 512].reshape(bb, (CARRY + tt) // 8, 8, 512)
        sub = lax.broadcasted_iota(jnp.int32, (bb, tt // 8, 8, 512), 2)
        acc = None
        for i in range(CONV_W):
            back = CONV_W - 1 - i
            if back == 0:
                shifted = xs[:, 1:]
            else:
                rot = pltpu.roll(xs, back, 2)
                shifted = jnp.where(sub < back, rot[:, :-1], rot[:, 1:])
            term = shifted * convw_ref[i:i + 1, c0:c0 + 512]
            acc = term if acc is None else acc + term
        qkv_ref[:, c0:c0 + 512] = _silu(acc).reshape(rows, 512)

    nh = GDN_HEADS
    tb = min(PREP_TILES_C, n_tiles)
    causal, eye, cum = _masks(seglen)
    level_masks = _level_masks(seglen)
    eye_f = eye.astype(F32)
    neg_decay_rate = -jnp.exp(alog_ref[...])

    def gate(t):
        ba = zab_ref[pl.ds(t * TILE, TILE), 1024:1152]
        return _sigmoid(ba), _split3(neg_decay_rate * _softplus(ba + dtb_ref[...]))

    project_z(4096, N_C)
    gates = [gate(t) for t in range(n_tiles)]
    chunk_starts = list(range(0, CONV_CH, 512))
    project_qkv(chunk_starts[0])
    prefix = None
    for prev, c0 in zip(chunk_starts[:-1], chunk_starts[1:]):
        project_qkv(c0)
        if prefix is None:
            prefix = [_prefix_and_total(cum, g[1]) for g in gates]
        conv_silu(prev)
    conv_silu(chunk_starts[-1])
    z_pieces = [functools.partial(project_z, c, c + 256) for c in range(CONV_CH, 4096, 256)]

    def prep_factors(j):
        q, k, v, beta, b_c, btot_c = [], [], [], [], [], []
        for t in range(tb):
            rs = pl.ds((j * tb + t) * TILE, TILE)
            beta_all = gates[j * tb + t][0]
            b_all, btot_all = prefix[j * tb + t]
            for h in range(nh):
                q.append(qkv_ref[rs, h * DV:(h + 1) * DV])
                k.append(qkv_ref[rs, 1024 + h * DV:1024 + (h + 1) * DV])
                v.append(qkv_ref[rs, 2048 + h * DV:2048 + (h + 1) * DV])
                beta.append(beta_all[:, h:h + 1])
                b_c.append(b_all[:, nh + h:nh + h + 1])
                btot_c.append(btot_all[:, nh + h:nh + h + 1])
        q, k, v = jnp.stack(q), jnp.stack(k), jnp.stack(v)
        beta, b_c, btot_c = jnp.stack(beta), jnp.stack(b_c), jnp.stack(btot_c)
        qn = q * lax.rsqrt(jnp.sum(q * q, axis=-1, keepdims=True) + EPS) * (DV ** -0.5)
        kn = k * lax.rsqrt(jnp.sum(k * k, axis=-1, keepdims=True) + EPS)
        b_r = _col_to_row(eye, b_c)
        seg = jnp.where(causal, jnp.exp(jnp.where(causal, b_c - b_r, 0.0)), 0.0)
        k_beta = kn * beta
        v_beta = v * beta
        a_mat = _bmm_nt(k_beta, kn) * seg
        inv = (eye_f - jnp.where(level_masks[0], a_mat, 0.0)).astype(BF16)
        e_b = jnp.exp(b_c)
        rhs = jnp.concatenate([v_beta, k_beta * e_b], axis=2).astype(BF16)
        attn = _bmm_nt(qn, kn) * seg
        q_dec = qn * e_b
        k_out = kn * jnp.exp(btot_c - b_c)
        dec = [jnp.exp(btot_c[:, s * seglen:s * seglen + 1, :]) for s in range(nseg)]
        dec = jnp.broadcast_to(dec[0] if nseg == 1 else jnp.concatenate(dec, axis=1), (nh * tb, nseg, DV))
        for t in range(tb):
            ti = j * tb + t
            hs = slice(t * nh, (t + 1) * nh)
            qd_scr[ti] = q_dec[hs]
            ko_scr[ti] = k_out[hs]
            at_scr[ti] = attn[hs]
            dec_scr[ti] = dec[hs]
        return a_mat.astype(BF16), inv, rhs

    def solve(parts):
        a_op, inv = (jnp.concatenate([p[i] for p in parts], axis=0) if len(parts) > 1 else parts[0][i]
                     for i in range(2))
        zero = jnp.zeros((), BF16)
        for level, lm in enumerate(level_masks[1:], start=1):
            s = 2 ** level
            compact = s % 16 == 0
            lhs = jnp.concatenate([inv[:, r:r + s] for r in range(s, TILE, 2 * s)], axis=1) if compact else inv
            half = _bmm(lhs, a_op)
            if z_pieces:
                z_pieces.pop(0)()
            full = _bmm(half, inv).astype(BF16)
            if z_pieces:
                z_pieces.pop(0)()
            if compact:
                gap = jnp.zeros((full.shape[0], s, TILE), BF16)
                full = jnp.concatenate([x for i in range(TILE // (2 * s)) for x in (gap, full[:, i * s:(i + 1) * s])],
                                       axis=1)
            inv = inv - jnp.where(lm, full, zero)
        while z_pieces:
            z_pieces.pop(0)()
        for j, p in enumerate(parts):
            sol = _bmm(inv[j * nh * tb:(j + 1) * nh * tb], p[2])
            for t in range(tb):
                hs = slice(t * nh, (t + 1) * nh)
                u_scr[j * tb + t] = sol[hs, :, :DV]
                w_scr[j * tb + t] = sol[hs, :, DV:]

    def tile_body(tiles):
        u, w, q_dec, k_out, attn = (_cat([scr[i] for i in tiles]) for scr in (u_scr, w_scr, qd_scr, ko_scr, at_scr))
        v_new, o_inter, states = [], [], []
        for s in range(nseg):
            sr = slice(s * seglen, (s + 1) * seglen)
            st = _cat([s_in[_seq_index(i, s, seglen, tt)] for i in tiles])
            ws = _bmm(jnp.concatenate([w[:, sr], q_dec[:, sr]], axis=1), st)
            v_new.append(u[:, sr] - ws[:, :seglen])
            o_inter.append(ws[:, seglen:])
            states.append(st)
        v_new = _cat(v_new, axis=1)
        o = _cat(o_inter, axis=1) + _bmm(attn, v_new)
        for s in range(nseg):
            sr = slice(s * seglen, (s + 1) * seglen)
            dec = _cat([dec_scr[i, :, s:s + 1, :] for i in tiles])
            st = states[s] * dec + _bmm_tn(k_out[:, sr], v_new[:, sr])
            for p, i in enumerate(tiles):
                do_ref[_seq_index(i, s, seglen, tt)] = st[p * nh:(p + 1) * nh]
        on = o * lax.rsqrt(jnp.mean(o * o, axis=-1, keepdims=True) + EPS) * gn_ref[...]
        for p, i in enumerate(tiles):
            rs = pl.ds(i * TILE, TILE)
            for h in range(nh):
                z = zab_ref[rs, h * DV:(h + 1) * DV]
                og_ref[rs, h * DV:(h + 1) * DV] = (on[p * nh + h] * _silu(z)).astype(BF16)

    assert n_tiles % tb == 0
    solve([prep_factors(j) for j in range(n_tiles // tb)])
    for tiles in _lockstep_tiles(bb, tt, n_tiles):
        tile_body(tiles)
    _out_project(x_ref, mod_ref, gpost_ref, og_ref, wout_ref, y_ref, bb, tt)


def _c_call(x, mod, states, weights, *, bb, tt):
    nb_total, t_total, _ = x.shape
    has_state = states is not None
    seglen = min(TILE, tt)
    rows = bb * tt
    assert nb_total % bb == 0 and t_total % tt == 0 and rows % TILE == 0 and TILE % seglen == 0
    assert tt % 8 == 0 and tt >= CONV_W - 1 and (tt == seglen or tt % TILE == 0)
    grid = (nb_total // bb, t_total // tt)
    assert not has_state or grid[1] == 1
    const2 = lambda b, t: (0, 0)
    sd_spec = pl.BlockSpec((bb, GDN_HEADS, DV, DV), lambda b, t: (b, 0, 0, 0))
    if has_state:
        sc_spec = pl.BlockSpec((CONV_W - 1, bb, CONV_CH), lambda b, t: (0, b, 0))
        sc_shape = (CONV_W - 1, nb_total, CONV_CH)
    else:
        sc_spec = pl.BlockSpec((bb, CONV_W - 1, CONV_CH), lambda b, t: (b, 0, 0))
        sc_shape = (nb_total, CONV_W - 1, CONV_CH)
    in_specs = [
        pl.BlockSpec((bb, tt, D_MODEL), lambda b, t: (b, t, 0)),
        pl.BlockSpec((bb, 1, 3 * D_MODEL), lambda b, t: (b, 0, 0)),
    ]
    if has_state:
        in_specs += [sd_spec, sc_spec]
    in_specs += [pl.BlockSpec(w.shape, const2, pipeline_mode=pl.Buffered(1)) for w in weights]
    args = (x, mod) + (tuple(states) if has_state else ()) + tuple(weights)
    return pl.pallas_call(
        functools.partial(_c_kernel, bb=bb, tt=tt, seglen=seglen, has_state=has_state),
        grid=grid,
        in_specs=in_specs,
        out_specs=[pl.BlockSpec((bb, tt, D_MODEL), lambda b, t: (b, t, 0)), sd_spec, sc_spec],
        out_shape=[jax.ShapeDtypeStruct(x.shape, F32),
                   jax.ShapeDtypeStruct((nb_total, GDN_HEADS, DV, DV), F32),
                   jax.ShapeDtypeStruct(sc_shape, F32)],
        scratch_shapes=[pltpu.VMEM((rows, D_MODEL), BF16),
                        pltpu.VMEM((bb, CARRY + tt, CONV_CH), F32),
                        pltpu.VMEM((rows, CONV_CH), F32),
                        pltpu.VMEM((rows, N_C - CONV_CH), F32),
                        pltpu.VMEM((rows, D_MODEL), BF16)]
                       + [pltpu.VMEM((rows // TILE, GDN_HEADS, TILE, DV), F32)] * 4
                       + [pltpu.VMEM((rows // TILE, GDN_HEADS, TILE, TILE), F32),
                          pltpu.VMEM((rows // TILE, GDN_HEADS, TILE // seglen, DV), F32)],
        compiler_params=pltpu.CompilerParams(dimension_semantics=("arbitrary", "arbitrary"),
                                             vmem_limit_bytes=VMEM_LIMIT_BYTES),
        name="gdn_layer_" + ("sample" if has_state else "prompt"),
    )(*args)


def _rope_tables(pos):
    half = DK_AB // 2
    inv = ROPE_BASE ** (-jnp.arange(half, dtype=F32) / half)
    ang = pos.astype(F32)[:, None] * inv[None, :]
    cos, sin = jnp.cos(ang), jnp.sin(ang)
    cos_h = jnp.concatenate([cos, cos], axis=-1)
    sin_h = jnp.concatenate([-sin, sin], axis=-1)
    return jnp.tile(cos_h, (1, 4)), jnp.tile(sin_h, (1, 4))


def _row(a):
    return a.reshape(1, -1).astype(F32)


def _retnet_decay_factors(seglen):
    lgam = jnp.repeat(jnp.log1p(-jnp.exp2(-5.0 - jnp.arange(4, dtype=F32))), DK_AB)[None, :]
    steps = (jnp.arange(TILE, dtype=F32) % seglen + 1.0)[:, None]
    b, btot = steps * lgam, seglen * jnp.broadcast_to(lgam, (TILE, 256))
    return jnp.stack([jnp.exp(b), jnp.exp(-b), jnp.exp(btot - b), jnp.exp(btot)])


def _prep_ab_kernel(w_ref, o_ref):
    def cols(r0, r1):
        return w_ref[r0:r1, :].T.astype(BF16)

    lane = lax.broadcasted_iota(jnp.int32, (o_ref.shape[0], GLA_RANK_PAD), 1)
    o_ref[:, AB_GQ:AB_RQ] = cols(0, 512)
    o_ref[:, AB_RQ:AB_GATE] = cols(1552, 2064)
    o_ref[:, AB_GATE:AB_GV] = jnp.where(lane < 16, w_ref[1536:1536 + GLA_RANK_PAD, :].T, 0.0).astype(BF16)
    o_ref[:, AB_GV:AB_RV] = cols(512, 1536)
    o_ref[:, AB_RV:N_AB] = cols(2064, 3088)


def _prep_c_kernel(w_ref, o_ref):
    for r0 in range(0, 4096, 1024):
        o_ref[:, r0:r0 + 1024] = w_ref[r0:r0 + 1024, :].T.astype(BF16)
    pad = jnp.zeros((128 - 2 * GDN_HEADS, w_ref.shape[1]), F32)
    o_ref[:, 4096:N_C] = jnp.concatenate([w_ref[4096:4096 + 2 * GDN_HEADS, :], pad], axis=0).T.astype(BF16)


def _weight_prep_call(body, w, n_out):
    _, k, n_in = w.shape
    kb = 256
    return pl.pallas_call(
        body,
        grid=(k // kb,),
        in_specs=[pl.BlockSpec((n_in, kb), lambda i: (0, i))],
        out_specs=pl.BlockSpec((kb, n_out), lambda i: (i, 0)),
        out_shape=jax.ShapeDtypeStruct((k, n_out), BF16),
        compiler_params=pltpu.CompilerParams(dimension_semantics=("arbitrary",),
                                             vmem_limit_bytes=VMEM_LIMIT_BYTES),
        name="weight_prep",
    )(w[0].T)


def kernel(x_prompt, x_sample, state_gla, state_ret, state_delta, state_conv, c_prompt, c_sample, w_ada, b_ada, g_pre, g_post, w_in_ab, w_gla_gate_up, b_gla_gate, g_norm_gla, g_norm_ret, w_out_ab, w_in_c, conv_c, a_log_c, dt_bias_c, g_norm_c, w_out_c):
    nbp, t_p, _ = x_prompt.shape
    nbs, t_s, _ = x_sample.shape
    assert w_ada.shape[0] == 2 and w_in_ab.shape[0] == 1

    mod = _ada_call(jnp.concatenate([c_prompt, c_sample], axis=0), w_ada, b_ada)
    mod_p = mod[:, :nbp, None, :]
    mod_s = mod[:, nbp:, None, :]

    w_ab = _weight_prep_call(_prep_ab_kernel, w_in_ab, N_AB)
    wgu = jnp.pad(w_gla_gate_up[0], ((0, GLA_RANK_PAD - 16), (0, 0))).astype(BF16)
    weights_ab = (_row(g_pre[0]), w_ab, wgu, _row(b_gla_gate[0]), _row(g_norm_gla[0]),
                  _row(g_norm_ret[0]), w_out_ab[0].astype(BF16), _row(g_post[0]))

    w_c = _weight_prep_call(_prep_c_kernel, w_in_c, N_C)
    head_lanes = ((0, 0), (GDN_HEADS, 128 - 2 * GDN_HEADS))
    weights_c = (_row(g_pre[1]), w_c, conv_c[0], jnp.pad(_row(a_log_c[0]), head_lanes),
                 jnp.pad(_row(dt_bias_c[0]), head_lanes), _row(g_norm_c[0]), w_out_c[0].astype(BF16),
                 _row(g_post[1]))

    cos_p, sin_p = _rope_tables(jnp.arange(t_p, dtype=jnp.int32))
    cos_s, sin_s = _rope_tables(PAST_LEN + jnp.arange(t_s, dtype=jnp.int32))
    bb_s = TILE // t_s
    bb_ab = 2 * bb_s
    cos_s, sin_s = jnp.tile(cos_s, (bb_ab, 1)), jnp.tile(sin_s, (bb_ab, 1))

    tt_p, bb_p = 256, 2
    xp, gla_p, ret_p = _ab_call(x_prompt, mod_p[0], cos_p, sin_p, None, weights_ab, bb=2 * bb_p, tt=tt_p)
    y_prompt, delta_p, conv_p = _c_call(xp, mod_p[1], None, weights_c, bb=bb_p, tt=tt_p)

    xs, gla_s, ret_s = _ab_call(x_sample, mod_s[0], cos_s, sin_s,
                                (state_gla.reshape(state_gla.shape[1:]), state_ret.reshape(state_ret.shape[1:])),
                                weights_ab, bb=bb_ab, tt=t_s)
    y_sample, delta_s, conv_s = _c_call(xs, mod_s[1],
                                        (state_delta.reshape(state_delta.shape[1:]),
                                         jnp.swapaxes(state_conv.reshape(state_conv.shape[1:]), 0, 1)),
                                        weights_c, bb=bb_s, tt=t_s)
    return (y_prompt, y_sample, gla_p[None], ret_p[None], delta_p[None], conv_p[None],
            gla_s[None], ret_s[None], delta_s[None], jnp.swapaxes(conv_s, 0, 1)[None])
```

```python
import functools
import math

import numpy as np
import jax
import jax.numpy as jnp
from jax import lax
from jax.experimental import pallas as pl
from jax.experimental.pallas import tpu as pltpu

F32 = jnp.float32
BF16 = jnp.bfloat16

D_MODEL = 1024
EPS = 1e-6
ROPE_BASE = 10000.0
PAST_LEN = 16384
GLA_TAU = 16.0
HEADS_AB = 8
DK_AB = 64
DV = 128
GLA_RANK_PAD = 128
N_AB = 3072 + GLA_RANK_PAD
AB_GQ, AB_GK, AB_RQ, AB_RK, AB_GATE, AB_GV, AB_GZ, AB_RV, AB_RZ, _ = (0, 256, 512, 768, 1024, 1152, 1664, 2176, 2688, 3200)
AB_CHUNKS = ((1024, 1152), (0, 512), (512, 1024), (1152, 1664), (1664, 2176), (2176, 2688), (2688, 3200))
GDN_HEADS = 8
CONV_W = 4
CONV_CH = 3072
N_C = 4096 + 128
TILE = 64
CARRY = 8
PREP_TILES_AB = 4
PREP_TILES_C = 2
VMEM_LIMIT_BYTES = 52 * 1024 * 1024


def _mm(a, b):
    return jnp.dot(a.astype(BF16), b.astype(BF16), preferred_element_type=F32)


def _mm_nt(a, b):
    return lax.dot_general(a.astype(BF16), b.astype(BF16), (((1,), (1,)), ((), ())),
                           preferred_element_type=F32)


def _mm_tn(a, b):
    return lax.dot_general(a.astype(BF16), b.astype(BF16), (((0,), (0,)), ((), ())),
                           preferred_element_type=F32)


def _bmm(a, b):
    return lax.dot_general(a.astype(BF16), b.astype(BF16), (((2,), (1,)), ((0,), (0,))),
                           preferred_element_type=F32)


def _bmm_nt(a, b):
    return lax.dot_general(a.astype(BF16), b.astype(BF16), (((2,), (2,)), ((0,), (0,))),
                           preferred_element_type=F32)


def _bmm_tn(a, b):
    return lax.dot_general(a.astype(BF16), b.astype(BF16), (((1,), (1,)), ((0,), (0,))),
                           preferred_element_type=F32)


def _split3(x):
    hi = x.astype(BF16)
    r = x - hi.astype(F32)
    mid = r.astype(BF16)
    lo = (r - mid.astype(F32)).astype(BF16)
    return hi, mid, lo


def _sigmoid(x):
    return 1.0 / (1.0 + jnp.exp(-x))


def _silu(x):
    h = 0.5 * x
    return h + h * jnp.tanh(h)


def _softplus(x):
    return jnp.maximum(x, 0.0) + jnp.log(1.0 + jnp.exp(-jnp.abs(x)))


def _masks(seglen):
    r = lax.broadcasted_iota(jnp.int32, (1, TILE, TILE), 1)
    c = lax.broadcasted_iota(jnp.int32, (1, TILE, TILE), 2)
    sh = int(math.log2(seglen))
    same = (r >> sh) == (c >> sh)
    causal = same & (c <= r)
    eye = r == c
    cum = jnp.concatenate([causal[0].astype(F32), same[0].astype(F32)], axis=0).astype(BF16)
    return causal, eye, cum


def _level_masks(seglen):
    r = lax.broadcasted_iota(jnp.int32, (1, TILE, TILE), 1)
    c = lax.broadcasted_iota(jnp.int32, (1, TILE, TILE), 2)
    masks = []
    for sh in range(int(math.log2(seglen))):
        masks.append(((r >> (sh + 1)) == (c >> (sh + 1))) & (((r >> sh) & 1) == 1) & (((c >> sh) & 1) == 0))
    return masks


def _seq_index(tile, seg, seglen, tt):
    return (tile * TILE + seg * seglen) // tt


def _lockstep_tiles(bb, tt, n_tiles):
    if tt < TILE:
        return [[i] for i in range(n_tiles)]
    per_seq = tt // TILE
    return [[p * per_seq + step for p in range(bb)] for step in range(per_seq)]


def _cat(xs, axis=0):
    return xs[0] if len(xs) == 1 else jnp.concatenate(xs, axis=axis)


def _prefix_and_total(cum, pieces):
    hi, mid, lo = pieces
    r = (jnp.dot(cum, hi, preferred_element_type=F32) + jnp.dot(cum, mid, preferred_element_type=F32)
         + jnp.dot(cum, lo, preferred_element_type=F32))
    return r[:TILE], r[TILE:]


def _row_to_col(eye, row):
    return jnp.sum(jnp.where(eye, jnp.broadcast_to(row, (row.shape[0], TILE, TILE)), 0.0), axis=2, keepdims=True)


def _col_to_row(eye, col):
    return jnp.sum(jnp.where(eye, jnp.broadcast_to(col, (col.shape[0], TILE, TILE)), 0.0), axis=1, keepdims=True)


def _prenorm(x_ref, mod_ref, gpre_ref, h_ref, rows):
    x3 = x_ref[...]
    shift = mod_ref[:, :, 0:D_MODEL]
    scale = mod_ref[:, :, D_MODEL:2 * D_MODEL]
    xn = x3 * lax.rsqrt(jnp.mean(x3 * x3, axis=-1, keepdims=True) + EPS) * gpre_ref[...]
    h3 = xn * (1.0 + scale) + shift
    h_ref[...] = h3.reshape(rows, D_MODEL).astype(BF16)


def _out_project(x_ref, mod_ref, gpost_ref, og_ref, wout_ref, y_ref, bb, tt):
    gate = mod_ref[:, :, 2 * D_MODEL:3 * D_MODEL]
    y3 = jnp.dot(og_ref[...], wout_ref[...], preferred_element_type=F32).reshape(bb, tt, D_MODEL)
    yn = y3 * lax.rsqrt(jnp.mean(y3 * y3, axis=-1, keepdims=True) + EPS) * gpost_ref[...]
    y_ref[...] = x_ref[...] + gate * yn


def _ada_kernel(c_ref, w_ref, b_ref, o_ref):
    o_ref[0] = _mm(_silu(c_ref[...]), w_ref[0]) + b_ref[0]


def _ada_call(c_all, w_ada, b_ada):
    n = c_all.shape[0]
    depth = w_ada.shape[0]
    nj = 3
    return pl.pallas_call(
        _ada_kernel,
        grid=(depth, nj),
        in_specs=[
            pl.BlockSpec((n, D_MODEL), lambda l, j: (0, 0)),
            pl.BlockSpec((1, D_MODEL, D_MODEL), lambda l, j: (l, 0, j)),
            pl.BlockSpec((1, 1, D_MODEL), lambda l, j: (l, 0, j)),
        ],
        out_specs=pl.BlockSpec((1, n, D_MODEL), lambda l, j: (l, 0, j)),
        out_shape=jax.ShapeDtypeStruct((depth, n, 3 * D_MODEL), F32),
        compiler_params=pltpu.CompilerParams(dimension_semantics=("arbitrary", "arbitrary")),
        name="adaln_mod",
    )(c_all, w_ada, b_ada.reshape(depth, 1, 3 * D_MODEL))


def _ab_kernel(*refs, bb, tt, seglen, has_state):
    if has_state:
        (x_ref, mod_ref, cos_ref, sin_ref, rdec_ref, sg_ref, sr_ref, gpre_ref, win_ref, wgu_ref, bgate_ref,
         ggla_ref, gret_ref, wout_ref, gpost_ref, y_ref, glao_ref, reto_ref, h_ref, og_ref,
         qd_scr, ko_scr, dec_scr, oi_scr, *proj_refs) = refs
        s_in = (sg_ref, sr_ref)
    else:
        (x_ref, mod_ref, cos_ref, sin_ref, rdec_ref, gpre_ref, win_ref, wgu_ref, bgate_ref,
         ggla_ref, gret_ref, wout_ref, gpost_ref, y_ref, glao_ref, reto_ref, h_ref, og_ref,
         qd_scr, ko_scr, dec_scr, oi_scr, *proj_refs) = refs
        s_in = (glao_ref, reto_ref)
    s_out = (glao_ref, reto_ref)
    rows = bb * tt
    nseg = TILE // seglen
    n_tiles = rows // TILE
    t_id = pl.program_id(1)

    if not has_state:
        @pl.when(t_id == 0)
        def _():
            glao_ref[...] = jnp.zeros_like(glao_ref)
            reto_ref[...] = jnp.zeros_like(reto_ref)

    _prenorm(x_ref, mod_ref, gpre_ref, h_ref, rows)

    def project(chunks):
        for c0, c1 in chunks:
            proj_refs[AB_CHUNKS.index((c0, c1))][...] = jnp.dot(h_ref[...], win_ref[:, c0:c1],
                                                                preferred_element_type=F32)

    def pcols(rs, a, b):
        i = [c0 <= a and b <= c1 for c0, c1 in AB_CHUNKS].index(True)
        return proj_refs[i][rs, a - AB_CHUNKS[i][0]:b - AB_CHUNKS[i][0]]

    project(AB_CHUNKS[:1])

    nh = HEADS_AB
    tb = min(PREP_TILES_AB, n_tiles)
    causal, eye, cum = _masks(seglen)
    lane = lax.broadcasted_iota(jnp.int32, (TILE, 256), 1)
    first_half = (lane & 63) < 32
    gnorm = (ggla_ref[...], gret_ref[...])

    def heads(parts, width):
        return jnp.stack([p[:, h * width:(h + 1) * width] for p in parts for h in range(4)])

    def prep_gate(t):
        rs = pl.ds(t * TILE, TILE)
        log_a = _mm(pcols(rs, AB_GATE, AB_GV), wgu_ref[...]) + bgate_ref[...]
        log_a = (jnp.minimum(log_a, 0.0) - jnp.log(1.0 + jnp.exp(-jnp.abs(log_a)))) / GLA_TAU
        return _split3(log_a)

    def prep_decay(t, prefix):
        rs = pl.ds(t * TILE, TILE)

        def pj(a, b):
            return pcols(rs, a, b)

        ts = rs if tt < TILE else pl.ds(t % (tt // TILE) * TILE, TILE)

        def rot(x):
            swapped = jnp.where(first_half, pltpu.roll(x, 256 - 32, 1), pltpu.roll(x, 32, 1))
            return x * cos_ref[ts, :] + swapped * sin_ref[ts, :]

        q_parts = (pj(AB_GQ, AB_GK) * (DK_AB ** -0.5), rot(pj(AB_RQ, AB_RK)) * (DK_AB ** -0.5))
        k_parts = (pj(AB_GK, AB_RQ), rot(pj(AB_RK, AB_GATE)))
        q_dec, k_inv = [], []
        b, btot = prefix
        factors = ((jnp.exp(b), jnp.exp(-b), jnp.exp(btot - b), jnp.exp(btot)), [rdec_ref[i] for i in range(4)])
        for part in range(2):
            e_b, e_nb, e_out, e_tot = factors[part]
            q_dec.append(q_parts[part] * e_b)
            k_inv.append(k_parts[part] * e_nb)
            cs = slice(part * 256, (part + 1) * 256)
            qd_scr[rs, cs] = q_dec[part]
            ko_scr[rs, cs] = k_parts[part] * e_out
            dec_scr[rs, cs] = e_tot
        return heads(q_dec, DK_AB), heads(k_inv, DK_AB)

    def prep_scores(j, qk):
        qd, ki = ([x[i] for x in qk[j * tb:(j + 1) * tb]] for i in range(2))
        qd, ki = (x[0] if tb == 1 else jnp.concatenate(x, axis=0) for x in (qd, ki))
        return jnp.where(causal, _bmm_nt(qd, ki), 0.0)

    def prep_intra(j, scores):
        v = []
        for t in range(tb):
            rs = pl.ds((j * tb + t) * TILE, TILE)
            v.append(heads((pcols(rs, AB_GV, AB_GZ), pcols(rs, AB_RV, AB_RZ)), DV))
        o_intra = _bmm(scores, v[0] if tb == 1 else jnp.concatenate(v, axis=0))
        for t in range(tb):
            oi_scr[j * tb + t] = o_intra[t * nh:(t + 1) * nh]

    assert n_tiles % tb == 0
    pieces = [prep_gate(t) for t in range(n_tiles)]
    project(AB_CHUNKS[1:3])
    prefix = [_prefix_and_total(cum, pieces[t]) for t in range(n_tiles)]
    project(AB_CHUNKS[3:])
    qk = [prep_decay(t, prefix[t]) for t in range(n_tiles)]
    scores = [prep_scores(j, qk) for j in range(n_tiles // tb)]
    for j in range(n_tiles // tb):
        prep_intra(j, scores[j])

    def tile_body(tiles):
        rss = [pl.ds(i * TILE, TILE) for i in tiles]
        qd = _cat([heads((qd_scr[rs, 0:256], qd_scr[rs, 256:512]), DK_AB) for rs in rss])
        ko = _cat([heads((ko_scr[rs, 0:256], ko_scr[rs, 256:512]), DK_AB) for rs in rss])
        v = _cat([heads((pcols(rs, AB_GV, AB_GZ), pcols(rs, AB_RV, AB_RZ)), DV) for rs in rss])
        z = _cat([heads((pcols(rs, AB_GZ, AB_RV), pcols(rs, AB_RZ, N_AB)), DV) for rs in rss])
        o_inter = []
        for s in range(nseg):
            sr = slice(s * seglen, (s + 1) * seglen)
            seqs = [_seq_index(i, s, seglen, tt) for i in tiles]
            st = _cat([x for b in seqs for x in (s_in[0][b], s_in[1][b])])
            o_inter.append(_bmm(qd[:, sr], st))
            drows = [pl.ds(i * TILE + s * seglen, 1) for i in tiles]
            dec_row = _cat([heads((dec_scr[dr, 0:256], dec_scr[dr, 256:512]), DK_AB) for dr in drows])
            st = st * _row_to_col(eye, dec_row) + _bmm_tn(ko[:, sr], v[:, sr])
            for p, b in enumerate(seqs):
                s_out[0][b] = st[p * nh:p * nh + 4]
                s_out[1][b] = st[p * nh + 4:(p + 1) * nh]
        o = _cat([oi_scr[i] for i in tiles]) + _cat(o_inter, axis=1)
        for p, rs in enumerate(rss):
            o_g, o_r = o[p * nh:p * nh + 4], o[p * nh + 4:(p + 1) * nh]
            on_g = o_g * lax.rsqrt(jnp.mean(o_g * o_g, axis=-1, keepdims=True) + EPS) * gnorm[0]
            dlt = o_r - jnp.mean(o_r, axis=-1, keepdims=True)
            on_r = dlt * lax.rsqrt(jnp.mean(dlt * dlt, axis=-1, keepdims=True) + EPS) * gnorm[1]
            og = (jnp.concatenate([on_g, on_r], axis=0) * _silu(z[p * nh:(p + 1) * nh])).astype(BF16)
            for h in range(nh):
                og_ref[rs, h * DV:(h + 1) * DV] = og[h]

    for tiles in _lockstep_tiles(bb, tt, n_tiles):
        tile_body(tiles)
    _out_project(x_ref, mod_ref, gpost_ref, og_ref, wout_ref, y_ref, bb, tt)


def _ab_call(x, mod, cos, sin, states, weights, *, bb, tt):
    nb_total, t_total, _ = x.shape
    has_state = states is not None
    seglen = min(TILE, tt)
    rows = bb * tt
    assert nb_total % bb == 0 and t_total % tt == 0 and rows % TILE == 0 and TILE % seglen == 0
    assert tt % 8 == 0 and (tt == seglen or tt % TILE == 0)
    grid = (nb_total // bb, t_total // tt)
    assert not has_state or grid[1] == 1
    const2 = lambda b, t: (0, 0)
    st_spec = pl.BlockSpec((bb, 4, DK_AB, DV), lambda b, t: (b, 0, 0, 0))
    in_specs = [
        pl.BlockSpec((bb, tt, D_MODEL), lambda b, t: (b, t, 0)),
        pl.BlockSpec((bb, 1, 3 * D_MODEL), lambda b, t: (b, 0, 0)),
    ]
    in_specs += [pl.BlockSpec((rows, 256), const2) if has_state else pl.BlockSpec((tt, 256), lambda b, t: (t, 0))] * 2
    in_specs += [pl.BlockSpec((4, TILE, 256), lambda b, t: (0, 0, 0), pipeline_mode=pl.Buffered(1))]
    if has_state:
        in_specs += [st_spec, st_spec]
    in_specs += [pl.BlockSpec(w.shape, const2, pipeline_mode=pl.Buffered(1)) for w in weights]
    args = (x, mod, cos, sin, _retnet_decay_factors(seglen)) + (tuple(states) if has_state else ()) + tuple(weights)
    st_shape = jax.ShapeDtypeStruct((nb_total, 4, DK_AB, DV), F32)
    return pl.pallas_call(
        functools.partial(_ab_kernel, bb=bb, tt=tt, seglen=seglen, has_state=has_state),
        grid=grid,
        in_specs=in_specs,
        out_specs=[pl.BlockSpec((bb, tt, D_MODEL), lambda b, t: (b, t, 0)), st_spec, st_spec],
        out_shape=[jax.ShapeDtypeStruct(x.shape, F32), st_shape, st_shape],
        scratch_shapes=[pltpu.VMEM((rows, D_MODEL), BF16), pltpu.VMEM((rows, D_MODEL), BF16)]
                       + [pltpu.VMEM((rows, HEADS_AB * DK_AB), F32)] * 3
                       + [pltpu.VMEM((rows // TILE, HEADS_AB, TILE, DV), F32)]
                       + [pltpu.VMEM((rows, c1 - c0), F32) for c0, c1 in AB_CHUNKS],
        compiler_params=pltpu.CompilerParams(dimension_semantics=("arbitrary", "arbitrary"),
                                             vmem_limit_bytes=VMEM_LIMIT_BYTES),
        name="gla_ret_layer_" + ("sample" if has_state else "prompt"),
    )(*args)


def _c_kernel(*refs, bb, tt, seglen, has_state):
    if has_state:
        (x_ref, mod_ref, sd_ref, sc_ref, gpre_ref, win_ref, convw_ref, alog_ref, dtb_ref, gn_ref, wout_ref,
         gpost_ref, y_ref, do_ref, co_ref, h_ref, cbuf_ref, qkv_ref, zab_ref, og_ref,
         u_scr, w_scr, qd_scr, ko_scr, at_scr, dec_scr) = refs
        s_in = sd_ref
    else:
        (x_ref, mod_ref, gpre_ref, win_ref, convw_ref, alog_ref, dtb_ref, gn_ref, wout_ref,
         gpost_ref, y_ref, do_ref, co_ref, h_ref, cbuf_ref, qkv_ref, zab_ref, og_ref,
         u_scr, w_scr, qd_scr, ko_scr, at_scr, dec_scr) = refs
        s_in = do_ref
    rows = bb * tt
    nseg = TILE // seglen
    n_tiles = rows // TILE
    t_id = pl.program_id(1)
    keep =slice(CARRY + tt - (CONV_W - 1), CARRY + tt)
    slot = slice(CARRY - (CONV_W - 1), CARRY)

    @pl.when(t_id > 0)
    def _():
        cbuf_ref[:, slot, :] = cbuf_ref[:, keep, :]

    @pl.when(t_id == 0)
    def _():
        cbuf_ref[:, 0:CARRY, :] = jnp.zeros((bb, CARRY, CONV_CH), F32)
        if has_state:
            for j in range(CONV_W - 1):
                cbuf_ref[:, CARRY - (CONV_W - 1) + j, :] = sc_ref[j]
        else:
            do_ref[...] = jnp.zeros_like(do_ref)

    _prenorm(x_ref, mod_ref, gpre_ref, h_ref, rows)

    def project_z(c0, c1):
        zab_ref[:, c0 - CONV_CH:c1 - CONV_CH] = jnp.dot(h_ref[...], win_ref[:, c0:c1], preferred_element_type=F32)

    def project_qkv(c0):
        pr = jnp.dot(h_ref[...], win_ref[:, c0:c0 + 512], preferred_element_type=F32)
        cbuf_ref[:, CARRY:CARRY + tt, c0:c0 + 512] = pr.reshape(bb, tt, 512)
        if has_state:
            for j in range(CONV_W - 1):
                co_ref[j, :, c0:c0 + 512] = cbuf_ref[:, CARRY + tt - (CONV_W - 1) + j, c0:c0 + 512]
        else:
            co_ref[:, :, c0:c0 + 512] = cbuf_ref[:, keep, c0:c0 + 512]

    def conv_silu(c0):
        xs = cbuf_ref[:, :, c0:c0 + 512].reshape(bb, (CARRY + tt) // 8, 8, 512)
        sub = lax.broadcasted_iota(jnp.int32, (bb, tt // 8, 8, 512), 2)
        acc = None
        for i in range(CONV_W):
            back = CONV_W - 1 - i
            if back == 0:
                shifted = xs[:, 1:]
            else:
                rot = pltpu.roll(xs, back, 2)
                shifted = jnp.where(sub < back, rot[:, :-1], rot[:, 1:])
            term = shifted * convw_ref[i:i + 1, c0:c0 + 512]
            acc = term if acc is None else acc + term
        qkv_ref[:, c0:c0 + 512] = _silu(acc).reshape(rows, 512)

    nh = GDN_HEADS
    tb = min(PREP_TILES_C, n_tiles)
    causal, eye, cum = _masks(seglen)
    level_masks = _level_masks(seglen)
    eye_f = eye.astype(F32)
    neg_decay_rate = -jnp.exp(alog_ref[...])

    def gate(t):
        ba = zab_ref[pl.ds(t * TILE, TILE), 1024:1152]
        return _sigmoid(ba), _split3(neg_decay_rate * _softplus(ba + dtb_ref[...]))

    project_z(4096, N_C)
    gates = [gate(t) for t in range(n_tiles)]
    chunk_starts = list(range(0, CONV_CH, 512))
    project_qkv(chunk_starts[0])
    prefix = None
    for prev, c0 in zip(chunk_starts[:-1], chunk_starts[1:]):
        project_qkv(c0)
        if prefix is None:
            prefix = [_prefix_and_total(cum, g[1]) for g in gates]
        conv_silu(prev)
    conv_silu(chunk_starts[-1])
    z_pieces = [functools.partial(project_z, c, c + 256) for c in range(CONV_CH, 4096, 256)]

    def prep_factors(j):
        q, k, v, beta, b_c, btot_c = [], [], [], [], [], []
        for t in range(tb):
            rs = pl.ds((j * tb + t) * TILE, TILE)
            beta_all = gates[j * tb + t][0]
            b_all, btot_all = prefix[j * tb + t]
            for h in range(nh):
                q.append(qkv_ref[rs, h * DV:(h + 1) * DV])
                k.append(qkv_ref[rs, 1024 + h * DV:1024 + (h + 1) * DV])
                v.append(qkv_ref[rs, 2048 + h * DV:2048 + (h + 1) * DV])
                beta.append(beta_all[:, h:h + 1])
                b_c.append(b_all[:, nh + h:nh + h + 1])
                btot_c.append(btot_all[:, nh + h:nh + h + 1])
        q, k, v = jnp.stack(q), jnp.stack(k), jnp.stack(v)
        beta, b_c, btot_c = jnp.stack(beta), jnp.stack(b_c), jnp.stack(btot_c)
        qn = q * lax.rsqrt(jnp.sum(q * q, axis=-1, keepdims=True) + EPS) * (DV ** -0.5)
        kn = k * lax.rsqrt(jnp.sum(k * k, axis=-1, keepdims=True) + EPS)
        b_r = _col_to_row(eye, b_c)
        seg = jnp.where(causal, jnp.exp(jnp.where(causal, b_c - b_r, 0.0)), 0.0)
        k_beta = kn * beta
        v_beta = v * beta
        a_mat = _bmm_nt(k_beta, kn) * seg
        inv = (eye_f - jnp.where(level_masks[0], a_mat, 0.0)).astype(BF16)
        e_b = jnp.exp(b_c)
        rhs = jnp.concatenate([v_beta, k_beta * e_b], axis=2).astype(BF16)
        attn = _bmm_nt(qn, kn) * seg
        q_dec = qn * e_b
        k_out = kn * jnp.exp(btot_c - b_c)
        dec = [jnp.exp(btot_c[:, s * seglen:s * seglen + 1, :]) for s in range(nseg)]
        dec = jnp.broadcast_to(dec[0] if nseg == 1 else jnp.concatenate(dec, axis=1), (nh * tb, nseg, DV))
        for t in range(tb):
            ti = j * tb + t
            hs = slice(t * nh, (t + 1) * nh)
            qd_scr[ti] = q_dec[hs]
            ko_scr[ti] = k_out[hs]
            at_scr[ti] = attn[hs]
            dec_scr[ti] = dec[hs]
        return a_mat.astype(BF16), inv, rhs

    def solve(parts):
        a_op, inv = (jnp.concatenate([p[i] for p in parts], axis=0) if len(parts) > 1 else parts[0][i]
                     for i in range(2))
        zero = jnp.zeros((), BF16)
        for level, lm in enumerate(level_masks[1:], start=1):
            s = 2 ** level
            compact = s % 16 == 0
            lhs = jnp.concatenate([inv[:, r:r + s] for r in range(s, TILE, 2 * s)], axis=1) if compact else inv
            half = _bmm(lhs, a_op)
            if z_pieces:
                z_pieces.pop(0)()
            full = _bmm(half, inv).astype(BF16)
            if z_pieces:
                z_pieces.pop(0)()
            if compact:
                gap = jnp.zeros((full.shape[0], s, TILE), BF16)
                full = jnp.concatenate([x for i in range(TILE // (2 * s)) for x in (gap, full[:, i * s:(i + 1) * s])],
                                       axis=1)
            inv = inv - jnp.where(lm, full, zero)
        while z_pieces:
            z_pieces.pop(0)()
        for j, p in enumerate(parts):
            sol = _bmm(inv[j * nh * tb:(j + 1) * nh * tb], p[2])
            for t in range(tb):
                hs = slice(t * nh, (t + 1) * nh)
                u_scr[j * tb + t] = sol[hs, :, :DV]
                w_scr[j * tb + t] = sol[hs, :, DV:]

    def tile_body(tiles):
        u, w, q_dec, k_out, attn = (_cat([scr[i] for i in tiles]) for scr in (u_scr, w_scr, qd_scr, ko_scr, at_scr))
        v_new, o_inter, states = [], [], []
        for s in range(nseg):
            sr = slice(s * seglen, (s + 1) * seglen)
            st = _cat([s_in[_seq_index(i, s, seglen, tt)] for i in tiles])
            ws = _bmm(jnp.concatenate([w[:, sr], q_dec[:, sr]], axis=1), st)
            v_new.append(u[:, sr] - ws[:, :seglen])
            o_inter.append(ws[:, seglen:])
            states.append(st)
        v_new = _cat(v_new, axis=1)
        o = _cat(o_inter, axis=1) + _bmm(attn, v_new)
        for s in range(nseg):
            sr = slice(s * seglen, (s + 1) * seglen)
            dec = _cat([dec_scr[i, :, s:s + 1, :] for i in tiles])
            st = states[s] * dec + _bmm_tn(k_out[:, sr], v_new[:, sr])
            for p, i in enumerate(tiles):
                do_ref[_seq_index(i, s, seglen, tt)] = st[p * nh:(p + 1) * nh]
        on = o * lax.rsqrt(jnp.mean(o * o, axis=-1, keepdims=True) + EPS) * gn_ref[...]
        for p, i in enumerate(tiles):
            rs = pl.ds(i * TILE, TILE)
            for h in range(nh):
                z = zab_ref[rs, h * DV:(h + 1) * DV]
                og_ref[rs, h * DV:(h + 1) * DV] = (on[p * nh + h] * _silu(z)).astype(BF16)

    assert n_tiles % tb == 0
    solve([prep_factors(j) for j in range(n_tiles // tb)])
    for tiles in _lockstep_tiles(bb, tt, n_tiles):
        tile_body(tiles)
    _out_project(x_ref, mod_ref, gpost_ref, og_ref, wout_ref, y_ref, bb, tt)


def _c_call(x, mod, states, weights, *, bb, tt):
    nb_total, t_total, _ = x.shape
    has_state = states is not None
    seglen = min(TILE, tt)
    rows = bb * tt
    assert nb_total % bb == 0 and t_total % tt == 0 and rows % TILE == 0 and TILE % seglen == 0
    assert tt % 8 == 0 and tt >= CONV_W - 1 and (tt == seglen or tt % TILE == 0)
    grid = (nb_total // bb, t_total // tt)
    assert not has_state or grid[1] == 1
    const2 = lambda b, t: (0, 0)
    sd_spec = pl.BlockSpec((bb, GDN_HEADS, DV, DV), lambda b, t: (b, 0, 0, 0))
    if has_state:
        sc_spec = pl.BlockSpec((CONV_W - 1, bb, CONV_CH), lambda b, t: (0, b, 0))
        sc_shape = (CONV_W - 1, nb_total, CONV_CH)
    else:
        sc_spec = pl.BlockSpec((bb, CONV_W - 1, CONV_CH), lambda b, t: (b, 0, 0))
        sc_shape = (nb_total, CONV_W - 1, CONV_CH)
    in_specs = [
        pl.BlockSpec((bb, tt, D_MODEL), lambda b, t: (b, t, 0)),
        pl.BlockSpec((bb, 1, 3 * D_MODEL), lambda b, t: (b, 0, 0)),
    ]
    if has_state:
        in_specs += [sd_spec, sc_spec]
    in_specs += [pl.BlockSpec(w.shape, const2, pipeline_mode=pl.Buffered(1)) for w in weights]
    args = (x, mod) + (tuple(states) if has_state else ()) + tuple(weights)
    return pl.pallas_call(
        functools.partial(_c_kernel, bb=bb, tt=tt, seglen=seglen, has_state=has_state),
        grid=grid,
        in_specs=in_specs,
        out_specs=[pl.BlockSpec((bb, tt, D_MODEL), lambda b, t: (b, t, 0)), sd_spec, sc_spec],
        out_shape=[jax.ShapeDtypeStruct(x.shape, F32),
                   jax.ShapeDtypeStruct((nb_total, GDN_HEADS, DV, DV), F32),
                   jax.ShapeDtypeStruct(sc_shape, F32)],
        scratch_shapes=[pltpu.VMEM((rows, D_MODEL), BF16),
                        pltpu.VMEM((bb, CARRY + tt, CONV_CH), F32),
                        pltpu.VMEM((rows, CONV_CH), F32),
                        pltpu.VMEM((rows, N_C - CONV_CH), F32),
                        pltpu.VMEM((rows, D_MODEL), BF16)]
                       + [pltpu.VMEM((rows // TILE, GDN_HEADS, TILE, DV), F32)] * 4
                       + [pltpu.VMEM((rows // TILE, GDN_HEADS, TILE, TILE), F32),
                          pltpu.VMEM((rows // TILE, GDN_HEADS, TILE // seglen, DV), F32)],
        compiler_params=pltpu.CompilerParams(dimension_semantics=("arbitrary", "arbitrary"),
                                             vmem_limit_bytes=VMEM_LIMIT_BYTES),
        name="gdn_layer_" + ("sample" if has_state else "prompt"),
    )(*args)


def _rope_tables(pos):
    half = DK_AB // 2
    inv = ROPE_BASE ** (-jnp.arange(half, dtype=F32) / half)
    ang = pos.astype(F32)[:, None] * inv[None, :]
    cos, sin = jnp.cos(ang), jnp.sin(ang)
    cos_h = jnp.concatenate([cos, cos], axis=-1)
    sin_h = jnp.concatenate([-sin, sin], axis=-1)
    return jnp.tile(cos_h, (1, 4)), jnp.tile(sin_h, (1, 4))


def _row(a):
    return a.reshape(1, -1).astype(F32)


def _retnet_decay_factors(seglen):
    lgam = jnp.repeat(jnp.log1p(-jnp.exp2(-5.0 - jnp.arange(4, dtype=F32))), DK_AB)[None, :]
    steps = (jnp.arange(TILE, dtype=F32) % seglen + 1.0)[:, None]
    b, btot = steps * lgam, seglen * jnp.broadcast_to(lgam, (TILE, 256))
    return jnp.stack([jnp.exp(b), jnp.exp(-b), jnp.exp(btot - b), jnp.exp(btot)])


def _prep_ab_kernel(w_ref, o_ref):
    def cols(r0, r1):
        return w_ref[r0:r1, :].T.astype(BF16)

    lane = lax.broadcasted_iota(jnp.int32, (o_ref.shape[0], GLA_RANK_PAD), 1)
    o_ref[:, AB_GQ:AB_RQ] = cols(0, 512)
    o_ref[:, AB_RQ:AB_GATE] = cols(1552, 2064)
    o_ref[:, AB_GATE:AB_GV] = jnp.where(lane < 16, w_ref[1536:1536 + GLA_RANK_PAD, :].T, 0.0).astype(BF16)
    o_ref[:, AB_GV:AB_RV] = cols(512, 1536)
    o_ref[:, AB_RV:N_AB] = cols(2064, 3088)


def _prep_c_kernel(w_ref, o_ref):
    for r0 in range(0, 4096, 1024):
        o_ref[:, r0:r0 + 1024] = w_ref[r0:r0 + 1024, :].T.astype(BF16)
    pad = jnp.zeros((128 - 2 * GDN_HEADS, w_ref.shape[1]), F32)
    o_ref[:, 4096:N_C] = jnp.concatenate([w_ref[4096:4096 + 2 * GDN_HEADS, :], pad], axis=0).T.astype(BF16)


def _weight_prep_call(body, w, n_out):
    _, k, n_in = w.shape
    kb = 256
    return pl.pallas_call(
        body,
        grid=(k // kb,),
        in_specs=[pl.BlockSpec((n_in, kb), lambda i: (0, i))],
        out_specs=pl.BlockSpec((kb, n_out), lambda i: (i, 0)),
        out_shape=jax.ShapeDtypeStruct((k, n_out), BF16),
        compiler_params=pltpu.CompilerParams(dimension_semantics=("arbitrary",),
                                             vmem_limit_bytes=VMEM_LIMIT_BYTES),
        name="weight_prep",
    )(w[0].T)


def kernel(x_prompt, x_sample, state_gla, state_ret, state_delta, state_conv, c_prompt, c_sample, w_ada, b_ada, g_pre, g_post, w_in_ab, w_gla_gate_up, b_gla_gate, g_norm_gla, g_norm_ret, w_out_ab, w_in_c, conv_c, a_log_c, dt_bias_c, g_norm_c, w_out_c):
    nbp, t_p, _ = x_prompt.shape
    nbs, t_s, _ = x_sample.shape
    assert w_ada.shape[0] == 2 and w_in_ab.shape[0] == 1

    mod = _ada_call(jnp.concatenate([c_prompt, c_sample], axis=0), w_ada, b_ada)
    mod_p = mod[:, :nbp, None, :]
    mod_s = mod[:, nbp:, None, :]

    w_ab = _weight_prep_call(_prep_ab_kernel, w_in_ab, N_AB)
    wgu = jnp.pad(w_gla_gate_up[0], ((0, GLA_RANK_PAD - 16), (0, 0))).astype(BF16)
    weights_ab = (_row(g_pre[0]), w_ab, wgu, _row(b_gla_gate[0]), _row(g_norm_gla[0]),
                  _row(g_norm_ret[0]), w_out_ab[0].astype(BF16), _row(g_post[0]))

    w_c = _weight_prep_call(_prep_c_kernel, w_in_c, N_C)
    head_lanes = ((0, 0), (GDN_HEADS, 128 - 2 * GDN_HEADS))
    weights_c = (_row(g_pre[1]), w_c, conv_c[0], jnp.pad(_row(a_log_c[0]), head_lanes),
                 jnp.pad(_row(dt_bias_c[0]), head_lanes), _row(g_norm_c[0]), w_out_c[0].astype(BF16),
                 _row(g_post[1]))

    cos_p, sin_p = _rope_tables(jnp.arange(t_p, dtype=jnp.int32))
    cos_s, sin_s = _rope_tables(PAST_LEN + jnp.arange(t_s, dtype=jnp.int32))
    bb_s = TILE // t_s
    bb_ab = 2 * bb_s
    cos_s, sin_s = jnp.tile(cos_s, (bb_ab, 1)), jnp.tile(sin_s, (bb_ab, 1))

    tt_p, bb_p = 256, 2
    xp, gla_p, ret_p = _ab_call(x_prompt, mod_p[0], cos_p, sin_p, None, weights_ab, bb=2 * bb_p, tt=tt_p)
    y_prompt, delta_p, conv_p = _c_call(xp, mod_p[1], None, weights_c, bb=bb_p, tt=tt_p)

    xs, gla_s, ret_s = _ab_call(x_sample, mod_s[0], cos_s, sin_s,
                                (state_gla.reshape(state_gla.shape[1:]), state_ret.reshape(state_ret.shape[1:])),
                                weights_ab, bb=bb_ab, tt=t_s)
    y_sample, delta_s, conv_s = _c_call(xs, mod_s[1],
                                        (state_delta.reshape(state_delta.shape[1:]),
                                         jnp.swapaxes(state_conv.reshape(state_conv.shape[1:]), 0, 1)),
                                        weights_c, bb=bb_s, tt=t_s)
    return (y_prompt, y_sample, gla_p[None], ret_p[None], delta_p[None], conv_p[None],
            gla_s[None], ret_s[None], delta_s[None], jnp.swapaxes(conv_s, 0, 1)[None])
```

```python
import functools
import math

import numpy as np
import jax
import jax.numpy as jnp
from jax import lax
from jax.experimental import pallas as pl
from jax.experimental.pallas import tpu as pltpu

F32 = jnp.float32
BF16 = jnp.bfloat16

D_MODEL = 1024
EPS = 1e-6
ROPE_BASE = 10000.0
PAST_LEN = 16384
GLA_TAU = 16.0
HEADS_AB = 8
DK_AB = 64
DV = 128
GLA_RANK_PAD = 128
N_AB = 3072 + GLA_RANK_PAD
AB_GQ, AB_GK, AB_RQ, AB_RK, AB_GATE, AB_GV, AB_GZ, AB_RV, AB_RZ, _ = (0, 256, 512, 768, 1024, 1152, 1664, 2176, 2688, 3200)
AB_CHUNKS = ((1024, 1152), (0, 512), (512, 1024), (1152, 1664), (1664, 2176), (2176, 2688), (2688, 3200))
GDN_HEADS = 8
CONV_W = 4
CONV_CH = 3072
N_C = 4096 + 128
TILE = 64
CARRY = 8
PREP_TILES_AB = 4
PREP_TILES_C = 2
VMEM_LIMIT_BYTES = 52 * 1024 * 1024


def _mm(a, b):
    return jnp.dot(a.astype(BF16), b.astype(BF16), preferred_element_type=F32)


def _mm_nt(a, b):
    return lax.dot_general(a.astype(BF16), b.astype(BF16), (((1,), (1,)), ((), ())),
                           preferred_element_type=F32)


def _mm_tn(a, b):
    return lax.dot_general(a.astype(BF16), b.astype(BF16), (((0,), (0,)), ((), ())),
                           preferred_element_type=F32)


def _bmm(a, b):
    return lax.dot_general(a.astype(BF16), b.astype(BF16), (((2,), (1,)), ((0,), (0,))),
                           preferred_element_type=F32)


def _bmm_nt(a, b):
    return lax.dot_general(a.astype(BF16), b.astype(BF16), (((2,), (2,)), ((0,), (0,))),
                           preferred_element_type=F32)


def _bmm_tn(a, b):
    return lax.dot_general(a.astype(BF16), b.astype(BF16), (((1,), (1,)), ((0,), (0,))),
                           preferred_element_type=F32)


def _split3(x):
    hi = x.astype(BF16)
    r = x - hi.astype(F32)
    mid = r.astype(BF16)
    lo = (r - mid.astype(F32)).astype(BF16)
    return hi, mid, lo


def _sigmoid(x):
    return 1.0 / (1.0 + jnp.exp(-x))


def _silu(x):
    h = 0.5 * x
    return h + h * jnp.tanh(h)


def _softplus(x):
    return jnp.maximum(x, 0.0) + jnp.log(1.0 + jnp.exp(-jnp.abs(x)))


def _masks(seglen):
    r = lax.broadcasted_iota(jnp.int32, (1, TILE, TILE), 1)
    c = lax.broadcasted_iota(jnp.int32, (1, TILE, TILE), 2)
    sh = int(math.log2(seglen))
    same = (r >> sh) == (c >> sh)
    causal = same & (c <= r)
    eye = r == c
    cum = jnp.concatenate([causal[0].astype(F32), same[0].astype(F32)], axis=0).astype(BF16)
    return causal, eye, cum


def _level_masks(seglen):
    r = lax.broadcasted_iota(jnp.int32, (1, TILE, TILE), 1)
    c = lax.broadcasted_iota(jnp.int32, (1, TILE, TILE), 2)
    masks = []
    for sh in range(int(math.log2(seglen))):
        masks.append(((r >> (sh + 1)) == (c >> (sh + 1))) & (((r >> sh) & 1) == 1) & (((c >> sh) & 1) == 0))
    return masks


def _seq_index(tile, seg, seglen, tt):
    return (tile * TILE + seg * seglen) // tt


def _lockstep_tiles(bb, tt, n_tiles):
    if tt < TILE:
        return [[i] for i in range(n_tiles)]
    per_seq = tt // TILE
    return [[p * per_seq + step for p in range(bb)] for step in range(per_seq)]


def _cat(xs, axis=0):
    return xs[0] if len(xs) == 1 else jnp.concatenate(xs, axis=axis)


def _prefix_and_total(cum, pieces):
    hi, mid, lo = pieces
    r = (jnp.dot(cum, hi, preferred_element_type=F32) + jnp.dot(cum, mid, preferred_element_type=F32)
         + jnp.dot(cum, lo, preferred_element_type=F32))
    return r[:TILE], r[TILE:]


def _row_to_col(eye, row):
    return jnp.sum(jnp.where(eye, jnp.broadcast_to(row, (row.shape[0], TILE, TILE)), 0.0), axis=2, keepdims=True)


def _col_to_row(eye, col):
    return jnp.sum(jnp.where(eye, jnp.broadcast_to(col, (col.shape[0], TILE, TILE)), 0.0), axis=1, keepdims=True)


def _prenorm(x_ref, mod_ref, gpre_ref, h_ref, rows):
    x3 = x_ref[...]
    shift = mod_ref[:, :, 0:D_MODEL]
    scale = mod_ref[:, :, D_MODEL:2 * D_MODEL]
    h3 = x3 * lax.rsqrt(jnp.mean(x3 * x3, axis=-1, keepdims=True) + EPS) * (gpre_ref[...] * (1.0 + scale)) + shift
    h_ref[...] = h3.reshape(rows, D_MODEL).astype(BF16)


def _out_project(x_ref, mod_ref, gpost_ref, og_ref, wout_ref, y_ref, bb, tt):
    gate = mod_ref[:, :, 2 * D_MODEL:3 * D_MODEL]
    y3 = jnp.dot(og_ref[...], wout_ref[...], preferred_element_type=F32).reshape(bb, tt, D_MODEL)
    yn = y3 * lax.rsqrt(jnp.mean(y3 * y3, axis=-1, keepdims=True) + EPS) * (gpost_ref[...] * gate)
    y_ref[...] = x_ref[...] + yn


def _ada_kernel(c_ref, w_ref, b_ref, o_ref):
    o_ref[0] = _mm(_silu(c_ref[...]), w_ref[0]) + b_ref[0]


def _ada_call(c_all, w_ada, b_ada):
    n = c_all.shape[0]
    depth = w_ada.shape[0]
    nj = 3
    return pl.pallas_call(
        _ada_kernel,
        grid=(depth, nj),
        in_specs=[
            pl.BlockSpec((n, D_MODEL), lambda l, j: (0, 0)),
            pl.BlockSpec((1, D_MODEL, D_MODEL), lambda l, j: (l, 0, j)),
            pl.BlockSpec((1, 1, D_MODEL), lambda l, j: (l, 0, j)),
        ],
        out_specs=pl.BlockSpec((1, n, D_MODEL), lambda l, j: (l, 0, j)),
        out_shape=jax.ShapeDtypeStruct((depth, n, 3 * D_MODEL), F32),
        compiler_params=pltpu.CompilerParams(dimension_semantics=("arbitrary", "arbitrary")),
        name="adaln_mod",
    )(c_all, w_ada, b_ada.reshape(depth, 1, 3 * D_MODEL))


def _ab_kernel(*refs, bb, tt, seglen, has_state):
    if has_state:
        (x_ref, mod_ref, cos_ref, sin_ref, rdec_ref, sg_ref, sr_ref, gpre_ref, win_ref, wgu_ref, bgate_ref,
         ggla_ref, gret_ref, wout_ref, gpost_ref, y_ref, glao_ref, reto_ref, h_ref, og_ref,
         qd_scr, ko_scr, dec_scr, oi_scr, *proj_refs) = refs
        s_in = (sg_ref, sr_ref)
    else:
        (x_ref, mod_ref, cos_ref, sin_ref, rdec_ref, gpre_ref, win_ref, wgu_ref, bgate_ref,
         ggla_ref, gret_ref, wout_ref, gpost_ref, y_ref, glao_ref, reto_ref, h_ref, og_ref,
         qd_scr, ko_scr, dec_scr, oi_scr, *proj_refs) = refs
        s_in = (glao_ref, reto_ref)
    s_out = (glao_ref, reto_ref)
    rows = bb * tt
    nseg = TILE // seglen
    n_tiles = rows // TILE
    t_id = pl.program_id(1)

    if not has_state:
        @pl.when(t_id == 0)
        def _():
            glao_ref[...] = jnp.zeros_like(glao_ref)
            reto_ref[...] = jnp.zeros_like(reto_ref)

    _prenorm(x_ref, mod_ref, gpre_ref, h_ref, rows)

    def project(chunks):
        for c0, c1 in chunks:
            proj_refs[AB_CHUNKS.index((c0, c1))][...] = jnp.dot(h_ref[...], win_ref[:, c0:c1],
                                                                preferred_element_type=F32)

    def pcols(rs, a, b):
        i = [c0 <= a and b <= c1 for c0, c1 in AB_CHUNKS].index(True)
        return proj_refs[i][rs, a - AB_CHUNKS[i][0]:b - AB_CHUNKS[i][0]]

    project(AB_CHUNKS[:1])

    nh = HEADS_AB
    tb = min(PREP_TILES_AB, n_tiles)
    causal, eye, cum = _masks(seglen)
    lane = lax.broadcasted_iota(jnp.int32, (TILE, 256), 1)
    first_half = (lane & 63) < 32
    gnorm = (ggla_ref[...], gret_ref[...])

    def heads(parts, width):
        return jnp.stack([p[:, h * width:(h + 1) * width] for p in parts for h in range(4)])

    def prep_gate(t):
        rs = pl.ds(t * TILE, TILE)
        log_a = _mm(pcols(rs, AB_GATE, AB_GV), wgu_ref[...]) + bgate_ref[...]
        log_a = (jnp.minimum(log_a, 0.0) - jnp.log(1.0 + jnp.exp(-jnp.abs(log_a)))) / GLA_TAU
        return _split3(log_a)

    def prep_decay(t, prefix):
        rs = pl.ds(t * TILE, TILE)

        def pj(a, b):
            return pcols(rs, a, b)

        ts = rs if tt < TILE else pl.ds(t % (tt // TILE) * TILE, TILE)

        def rot(x):
            swapped = jnp.where(first_half, pltpu.roll(x, 256 - 32, 1), pltpu.roll(x, 32, 1))
            return x * cos_ref[ts, :] + swapped * sin_ref[ts, :]

        q_parts = (pj(AB_GQ, AB_GK) * (DK_AB ** -0.5), rot(pj(AB_RQ, AB_RK)) * (DK_AB ** -0.5))
        k_parts = (pj(AB_GK, AB_RQ), rot(pj(AB_RK, AB_GATE)))
        q_dec, k_inv = [], []
        b, btot = prefix
        factors = ((jnp.exp(b), jnp.exp(-b), jnp.exp(btot - b), jnp.exp(btot)), [rdec_ref[i] for i in range(4)])
        for part in range(2):
            e_b, e_nb, e_out, e_tot = factors[part]
            q_dec.append(q_parts[part] * e_b)
            k_inv.append(k_parts[part] * e_nb)
            cs = slice(part * 256, (part + 1) * 256)
            qd_scr[rs, cs] = q_dec[part]
            ko_scr[rs, cs] = k_parts[part] * e_out
            dec_scr[rs, cs] = e_tot
        return heads(q_dec, DK_AB), heads(k_inv, DK_AB)

    def prep_scores(j, qk):
        qd, ki = ([x[i] for x in qk[j * tb:(j + 1) * tb]] for i in range(2))
        qd, ki = (x[0] if tb == 1 else jnp.concatenate(x, axis=0) for x in (qd, ki))
        return jnp.where(causal, _bmm_nt(qd, ki), 0.0)

    def prep_intra(j, scores):
        v = []
        for t in range(tb):
            rs = pl.ds((j * tb + t) * TILE, TILE)
            v.append(heads((pcols(rs, AB_GV, AB_GZ), pcols(rs, AB_RV, AB_RZ)), DV))
        o_intra = _bmm(scores, v[0] if tb == 1 else jnp.concatenate(v, axis=0))
        for t in range(tb):
            oi_scr[j * tb + t] = o_intra[t * nh:(t + 1) * nh]

    assert n_tiles % tb == 0
    pieces = [prep_gate(t) for t in range(n_tiles)]
    project(AB_CHUNKS[1:3])
    prefix = [_prefix_and_total(cum, pieces[t]) for t in range(n_tiles)]
    project(AB_CHUNKS[3:])
    qk = [prep_decay(t, prefix[t]) for t in range(n_tiles)]
    scores = [prep_scores(j, qk) for j in range(n_tiles // tb)]
    for j in range(n_tiles // tb):
        prep_intra(j, scores[j])

    def tile_body(tiles):
        rss = [pl.ds(i * TILE, TILE) for i in tiles]
        qd = _cat([heads((qd_scr[rs, 0:256], qd_scr[rs, 256:512]), DK_AB) for rs in rss])
        ko = _cat([heads((ko_scr[rs, 0:256], ko_scr[rs, 256:512]), DK_AB) for rs in rss])
        v = _cat([heads((pcols(rs, AB_GV, AB_GZ), pcols(rs, AB_RV, AB_RZ)), DV) for rs in rss])
        z = _cat([heads((pcols(rs, AB_GZ, AB_RV), pcols(rs, AB_RZ, N_AB)), DV) for rs in rss])
        o_inter = []
        for s in range(nseg):
            sr = slice(s * seglen, (s + 1) * seglen)
            seqs = [_seq_index(i, s, seglen, tt) for i in tiles]
            st = _cat([x for b in seqs for x in (s_in[0][b], s_in[1][b])])
            o_inter.append(_bmm(qd[:, sr], st))
            drows = [pl.ds(i * TILE + s * seglen, 1) for i in tiles]
            dec_row = _cat([heads((dec_scr[dr, 0:256], dec_scr[dr, 256:512]), DK_AB) for dr in drows])
            st = st * _row_to_col(eye, dec_row) + _bmm_tn(ko[:, sr], v[:, sr])
            for p, b in enumerate(seqs):
                s_out[0][b] = st[p * nh:p * nh + 4]
                s_out[1][b] = st[p * nh + 4:(p + 1) * nh]
        o = _cat([oi_scr[i] for i in tiles]) + _cat(o_inter, axis=1)
        for p, rs in enumerate(rss):
            o_g, o_r = o[p * nh:p * nh + 4], o[p * nh + 4:(p + 1) * nh]
            on_g = o_g * lax.rsqrt(jnp.mean(o_g * o_g, axis=-1, keepdims=True) + EPS) * gnorm[0]
            dlt = o_r - jnp.mean(o_r, axis=-1, keepdims=True)
            on_r = dlt * lax.rsqrt(jnp.mean(dlt * dlt, axis=-1, keepdims=True) + EPS) * gnorm[1]
            og = (jnp.concatenate([on_g, on_r], axis=0) * _silu(z[p * nh:(p + 1) * nh])).astype(BF16)
            for h in range(nh):
                og_ref[rs, h * DV:(h + 1) * DV] = og[h]

    for tiles in _lockstep_tiles(bb, tt, n_tiles):
        tile_body(tiles)
    _out_project(x_ref, mod_ref, gpost_ref, og_ref, wout_ref, y_ref, bb, tt)


def _ab_call(x, mod, cos, sin, states, weights, *, bb, tt):
    nb_total, t_total, _ = x.shape
    has_state = states is not None
    seglen = min(TILE, tt)
    rows = bb * tt
    assert nb_total % bb == 0 and t_total % tt == 0 and rows % TILE == 0 and TILE % seglen == 0
    assert tt % 8 == 0 and (tt == seglen or tt % TILE == 0)
    grid = (nb_total // bb, t_total // tt)
    assert not has_state or grid[1] == 1
    const2 = lambda b, t: (0, 0)
    st_spec = pl.BlockSpec((bb, 4, DK_AB, DV), lambda b, t: (b, 0, 0, 0))
    in_specs = [
        pl.BlockSpec((bb, tt, D_MODEL), lambda b, t: (b, t, 0)),
        pl.BlockSpec((bb, 1, 3 * D_MODEL), lambda b, t: (b, 0, 0)),
    ]
    in_specs += [pl.BlockSpec((rows, 256), const2) if has_state else pl.BlockSpec((tt, 256), lambda b, t: (t, 0))] * 2
    in_specs += [pl.BlockSpec((4, TILE, 256), lambda b, t: (0, 0, 0), pipeline_mode=pl.Buffered(1))]
    if has_state:
        in_specs += [st_spec, st_spec]
    in_specs += [pl.BlockSpec(w.shape, const2, pipeline_mode=pl.Buffered(1)) for w in weights]
    args = (x, mod, cos, sin, _retnet_decay_factors(seglen)) + (tuple(states) if has_state else ()) + tuple(weights)
    st_shape = jax.ShapeDtypeStruct((nb_total, 4, DK_AB, DV), F32)
    return pl.pallas_call(
        functools.partial(_ab_kernel, bb=bb, tt=tt, seglen=seglen, has_state=has_state),
        grid=grid,
        in_specs=in_specs,
        out_specs=[pl.BlockSpec((bb, tt, D_MODEL), lambda b, t: (b, t, 0)), st_spec, st_spec],
        out_shape=[jax.ShapeDtypeStruct(x.shape, F32), st_shape, st_shape],
        scratch_shapes=[pltpu.VMEM((rows, D_MODEL), BF16), pltpu.VMEM((rows, D_MODEL), BF16)]
                       + [pltpu.VMEM((rows, HEADS_AB * DK_AB), F32)] * 3
                       + [pltpu.VMEM((rows // TILE, HEADS_AB, TILE, DV), F32)]
                       + [pltpu.VMEM((rows, c1 - c0), F32) for c0, c1 in AB_CHUNKS],
        compiler_params=pltpu.CompilerParams(dimension_semantics=("arbitrary", "arbitrary"),
                                             vmem_limit_bytes=VMEM_LIMIT_BYTES),
        name="gla_ret_layer_" + ("sample" if has_state else "prompt"),
    )(*args)


def _c_kernel(*refs, bb, tt, seglen, has_state):
    if has_state:
        (x_ref, mod_ref, sd_ref, sc_ref, gpre_ref, win_ref, convw_ref, alog_ref, dtb_ref, gn_ref, wout_ref,
         gpost_ref, y_ref, do_ref, co_ref, h_ref, cbuf_ref, qkv_ref, zab_ref, og_ref,
         u_scr, w_scr, qd_scr, ko_scr, at_scr, dec_scr) = refs
        s_in = sd_ref
    else:
        (x_ref, mod_ref, gpre_ref, win_ref, convw_ref, alog_ref, dtb_ref, gn_ref, wout_ref,
         gpost_ref, y_ref, do_ref, co_ref, h_ref, cbuf_ref, qkv_ref, zab_ref, og_ref,
         u_scr, w_scr, qd_scr, ko_scr, at_scr, dec_scr) = refs
        s_in = do_ref
    rows = bb * tt
    nseg = TILE // seglen
    n_tiles = rows // TILE
    t_id = pl.program_id(1)
    keep =slice(CARRY + tt - (CONV_W - 1), CARRY + tt)
    slot = slice(CARRY - (CONV_W - 1), CARRY)

    @pl.when(t_id > 0)
    def _():
        cbuf_ref[:, slot, :] = cbuf_ref[:, keep, :]

    @pl.when(t_id == 0)
    def _():
        cbuf_ref[:, 0:CARRY, :] = jnp.zeros((bb, CARRY, CONV_CH), F32)
        if has_state:
            for j in range(CONV_W - 1):
                cbuf_ref[:, CARRY - (CONV_W - 1) + j, :] = sc_ref[j]
        else:
            do_ref[...] = jnp.zeros_like(do_ref)

    _prenorm(x_ref, mod_ref, gpre_ref, h_ref, rows)

    def project_z(c0, c1):
        zab_ref[:, c0 - CONV_CH:c1 - CONV_CH] = jnp.dot(h_ref[...], win_ref[:, c0:c1], preferred_element_type=F32)

    def project_qkv(c0):
        pr = jnp.dot(h_ref[...], win_ref[:, c0:c0 + 512], preferred_element_type=F32)
        cbuf_ref[:, CARRY:CARRY + tt, c0:c0 + 512] = pr.reshape(bb, tt, 512)
        if has_state:
            for j in range(CONV_W - 1):
                co_ref[j, :, c0:c0 + 512] = cbuf_ref[:, CARRY + tt - (CONV_W - 1) + j, c0:c0 + 512]
        else:
            co_ref[:, :, c0:c0 + 512] = cbuf_ref[:, keep, c0:c0 + 512]

    def conv_silu(c0):
        xs = cbuf_ref[:, :, c0:c0 + 512].reshape(bb, (CARRY + tt) // 8, 8, 512)
        sub = lax.broadcasted_iota(jnp.int32, (bb, tt // 8, 8, 512), 2)
        acc = None
        for i in range(CONV_W):
            back = CONV_W - 1 - i
            if back == 0:
                shifted = xs[:, 1:]
            else:
                rot = pltpu.roll(xs, back, 2)
                shifted = jnp.where(sub < back, rot[:, :-1], rot[:, 1:])
            term = shifted * convw_ref[i:i + 1, c0:c0 + 512]
            acc = term if acc is None else acc + term
        qkv_ref[:, c0:c0 + 512] = _silu(acc).reshape(rows, 512)

    nh = GDN_HEADS
    tb = min(PREP_TILES_C, n_tiles)
    causal, eye, cum = _masks(seglen)
    level_masks = _level_masks(seglen)
    eye_f = eye.astype(F32)
    neg_decay_rate = -jnp.exp(alog_ref[...])

    def gate(t):
        ba = zab_ref[pl.ds(t * TILE, TILE), 1024:1152]
        return _sigmoid(ba), _split3(neg_decay_rate * _softplus(ba + dtb_ref[...]))

    project_z(4096, N_C)
    gates = [gate(t) for t in range(n_tiles)]
    chunk_starts = list(range(0, CONV_CH, 512))
    project_qkv(chunk_starts[0])
    prefix = None
    for prev, c0 in zip(chunk_starts[:-1], chunk_starts[1:]):
        project_qkv(c0)
        if prefix is None:
            prefix = [_prefix_and_total(cum, g[1]) for g in gates]
        conv_silu(prev)
    conv_silu(chunk_starts[-1])
    z_pieces = [functools.partial(project_z, c, c + 256) for c in range(CONV_CH, 4096, 256)]

    def prep_factors(j):
        q, k, v, beta, b_c, btot_c = [], [], [], [], [], []
        for t in range(tb):
            rs = pl.ds((j * tb + t) * TILE, TILE)
            beta_all = gates[j * tb + t][0]
            b_all, btot_all = prefix[j * tb + t]
            for h in range(nh):
                q.append(qkv_ref[rs, h * DV:(h + 1) * DV])
                k.append(qkv_ref[rs, 1024 + h * DV:1024 + (h + 1) * DV])
                v.append(qkv_ref[rs, 2048 + h * DV:2048 + (h + 1) * DV])
                beta.append(beta_all[:, h:h + 1])
                b_c.append(b_all[:, nh + h:nh + h + 1])
                btot_c.append(btot_all[:, nh + h:nh + h + 1])
        q, k, v = jnp.stack(q), jnp.stack(k), jnp.stack(v)
        beta, b_c, btot_c = jnp.stack(beta), jnp.stack(b_c), jnp.stack(btot_c)
        qn = q * (lax.rsqrt(jnp.sum(q * q, axis=-1, keepdims=True) + EPS) * (DV ** -0.5))
        kn = k * lax.rsqrt(jnp.sum(k * k, axis=-1, keepdims=True) + EPS)
        b_r = _col_to_row(eye, b_c)
        seg = jnp.where(causal, jnp.exp(jnp.where(causal, b_c - b_r, 0.0)), 0.0)
        k_beta = kn * beta
        v_beta = v * beta
        a_mat = _bmm_nt(k_beta, kn) * seg
        inv = (eye_f - jnp.where(level_masks[0], a_mat, 0.0)).astype(BF16)
        e_b = jnp.exp(b_c)
        rhs = jnp.concatenate([v_beta, k_beta * e_b], axis=2).astype(BF16)
        attn = _bmm_nt(qn, kn) * seg
        q_dec = qn * e_b
        k_out = kn * jnp.exp(btot_c - b_c)
        dec = [jnp.exp(btot_c[:, s * seglen:s * seglen + 1, :]) for s in range(nseg)]
        dec = jnp.broadcast_to(dec[0] if nseg == 1 else jnp.concatenate(dec, axis=1), (nh * tb, nseg, DV))
        for t in range(tb):
            ti = j * tb + t
            hs = slice(t * nh, (t + 1) * nh)
            qd_scr[ti] = q_dec[hs]
            ko_scr[ti] = k_out[hs]
            at_scr[ti] = attn[hs]
            dec_scr[ti] = dec[hs]
        return a_mat.astype(BF16), inv, rhs

    def solve(parts):
        a_op, inv = (jnp.concatenate([p[i] for p in parts], axis=0) if len(parts) > 1 else parts[0][i]
                     for i in range(2))
        zero = jnp.zeros((), BF16)
        for level, lm in enumerate(level_masks[1:], start=1):
            s = 2 ** level
            compact = s % 16 == 0
            lhs = jnp.concatenate([inv[:, r:r + s] for r in range(s, TILE, 2 * s)], axis=1) if compact else inv
            half = _bmm(lhs, a_op)
            if z_pieces:
                z_pieces.pop(0)()
            full = _bmm(half, inv).astype(BF16)
            if z_pieces:
                z_pieces.pop(0)()
            if compact:
                gap = jnp.zeros((full.shape[0], s, TILE), BF16)
                full = jnp.concatenate([x for i in range(TILE // (2 * s)) for x in (gap, full[:, i * s:(i + 1) * s])],
                                       axis=1)
            inv = inv - jnp.where(lm, full, zero)
        while z_pieces:
            z_pieces.pop(0)()
        for j, p in enumerate(parts):
            sol = _bmm(inv[j * nh * tb:(j + 1) * nh * tb], p[2])
            for t in range(tb):
                hs = slice(t * nh, (t + 1) * nh)
                u_scr[j * tb + t] = sol[hs, :, :DV]
                w_scr[j * tb + t] = sol[hs, :, DV:]

    def tile_body(tiles):
        u, w, q_dec, k_out, attn = (_cat([scr[i] for i in tiles]) for scr in (u_scr, w_scr, qd_scr, ko_scr, at_scr))
        v_new, o_inter, states = [], [], []
        for s in range(nseg):
            sr = slice(s * seglen, (s + 1) * seglen)
            st = _cat([s_in[_seq_index(i, s, seglen, tt)] for i in tiles])
            ws = _bmm(jnp.concatenate([w[:, sr], q_dec[:, sr]], axis=1), st)
            v_new.append(u[:, sr] - ws[:, :seglen])
            o_inter.append(ws[:, seglen:])
            states.append(st)
        v_new = _cat(v_new, axis=1)
        o = _cat(o_inter, axis=1) + _bmm(attn, v_new)
        for s in range(nseg):
            sr = slice(s * seglen, (s + 1) * seglen)
            dec = _cat([dec_scr[i, :, s:s + 1, :] for i in tiles])
            st = states[s] * dec + _bmm_tn(k_out[:, sr], v_new[:, sr])
            for p, i in enumerate(tiles):
                do_ref[_seq_index(i, s, seglen, tt)] = st[p * nh:(p + 1) * nh]
        on = o * lax.rsqrt(jnp.mean(o * o, axis=-1, keepdims=True) + EPS) * gn_ref[...]
        for p, i in enumerate(tiles):
            rs = pl.ds(i * TILE, TILE)
            for h in range(nh):
                z = zab_ref[rs, h * DV:(h + 1) * DV]
                og_ref[rs, h * DV:(h + 1) * DV] = (on[p * nh + h] * _silu(z)).astype(BF16)

    assert n_tiles % tb == 0
    solve([prep_factors(j) for j in range(n_tiles // tb)])
    for tiles in _lockstep_tiles(bb, tt, n_tiles):
        tile_body(tiles)
    _out_project(x_ref, mod_ref, gpost_ref, og_ref, wout_ref, y_ref, bb, tt)


def _c_call(x, mod, states, weights, *, bb, tt):
    nb_total, t_total, _ = x.shape
    has_state = states is not None
    seglen = min(TILE, tt)
    rows = bb * tt
    assert nb_total % bb == 0 and t_total % tt == 0 and rows % TILE == 0 and TILE % seglen == 0
    assert tt % 8 == 0 and tt >= CONV_W - 1 and (tt == seglen or tt % TILE == 0)
    grid = (nb_total // bb, t_total // tt)
    assert not has_state or grid[1] == 1
    const2 = lambda b, t: (0, 0)
    sd_spec = pl.BlockSpec((bb, GDN_HEADS, DV, DV), lambda b, t: (b, 0, 0, 0))
    if has_state:
        sc_spec = pl.BlockSpec((CONV_W - 1, bb, CONV_CH), lambda b, t: (0, b, 0))
        sc_shape = (CONV_W - 1, nb_total, CONV_CH)
    else:
        sc_spec = pl.BlockSpec((bb, CONV_W - 1, CONV_CH), lambda b, t: (b, 0, 0))
        sc_shape = (nb_total, CONV_W - 1, CONV_CH)
    in_specs = [
        pl.BlockSpec((bb, tt, D_MODEL), lambda b, t: (b, t, 0)),
        pl.BlockSpec((bb, 1, 3 * D_MODEL), lambda b, t: (b, 0, 0)),
    ]
    if has_state:
        in_specs += [sd_spec, sc_spec]
    in_specs += [pl.BlockSpec(w.shape, const2, pipeline_mode=pl.Buffered(1)) for w in weights]
    args = (x, mod) + (tuple(states) if has_state else ()) + tuple(weights)
    return pl.pallas_call(
        functools.partial(_c_kernel, bb=bb, tt=tt, seglen=seglen, has_state=has_state),
        grid=grid,
        in_specs=in_specs,
        out_specs=[pl.BlockSpec((bb, tt, D_MODEL), lambda b, t: (b, t, 0)), sd_spec, sc_spec],
        out_shape=[jax.ShapeDtypeStruct(x.shape, F32),
                   jax.ShapeDtypeStruct((nb_total, GDN_HEADS, DV, DV), F32),
                   jax.ShapeDtypeStruct(sc_shape, F32)],
        scratch_shapes=[pltpu.VMEM((rows, D_MODEL), BF16),
                        pltpu.VMEM((bb, CARRY + tt, CONV_CH), F32),
                        pltpu.VMEM((rows, CONV_CH), F32),
                        pltpu.VMEM((rows, N_C - CONV_CH), F32),
                        pltpu.VMEM((rows, D_MODEL), BF16)]
                       + [pltpu.VMEM((rows // TILE, GDN_HEADS, TILE, DV), F32)] * 4
                       + [pltpu.VMEM((rows // TILE, GDN_HEADS, TILE, TILE), F32),
                          pltpu.VMEM((rows // TILE, GDN_HEADS, TILE // seglen, DV), F32)],
        compiler_params=pltpu.CompilerParams(dimension_semantics=("arbitrary", "arbitrary"),
                                             vmem_limit_bytes=VMEM_LIMIT_BYTES),
        name="gdn_layer_" + ("sample" if has_state else "prompt"),
    )(*args)


def _rope_tables(pos):
    half = DK_AB // 2
    inv = ROPE_BASE ** (-jnp.arange(half, dtype=F32) / half)
    ang = pos.astype(F32)[:, None] * inv[None, :]
    cos, sin = jnp.cos(ang), jnp.sin(ang)
    cos_h = jnp.concatenate([cos, cos], axis=-1)
    sin_h = jnp.concatenate([-sin, sin], axis=-1)
    return jnp.tile(cos_h, (1, 4)), jnp.tile(sin_h, (1, 4))


def _row(a):
    return a.reshape(1, -1).astype(F32)


def _retnet_decay_factors(seglen):
    lgam = jnp.repeat(jnp.log1p(-jnp.exp2(-5.0 - jnp.arange(4, dtype=F32))), DK_AB)[None, :]
    steps = (jnp.arange(TILE, dtype=F32) % seglen + 1.0)[:, None]
    b, btot = steps * lgam, seglen * jnp.broadcast_to(lgam, (TILE, 256))
    return jnp.stack([jnp.exp(b), jnp.exp(-b), jnp.exp(btot - b), jnp.exp(btot)])


def _prep_ab_kernel(w_ref, o_ref):
    def cols(r0, r1):
        return w_ref[r0:r1, :].T.astype(BF16)

    lane = lax.broadcasted_iota(jnp.int32, (o_ref.shape[0], GLA_RANK_PAD), 1)
    o_ref[:, AB_GQ:AB_RQ] = cols(0, 512)
    o_ref[:, AB_RQ:AB_GATE] = cols(1552, 2064)
    o_ref[:, AB_GATE:AB_GV] = jnp.where(lane < 16, w_ref[1536:1536 + GLA_RANK_PAD, :].T, 0.0).astype(BF16)
    o_ref[:, AB_GV:AB_RV] = cols(512, 1536)
    o_ref[:, AB_RV:N_AB] = cols(2064, 3088)


def _prep_c_kernel(w_ref, o_ref):
    for r0 in range(0, 4096, 1024):
        o_ref[:, r0:r0 + 1024] = w_ref[r0:r0 + 1024, :].T.astype(BF16)
    pad = jnp.zeros((128 - 2 * GDN_HEADS, w_ref.shape[1]), F32)
    o_ref[:, 4096:N_C] = jnp.concatenate([w_ref[4096:4096 + 2 * GDN_HEADS, :], pad], axis=0).T.astype(BF16)


def _weight_prep_call(body, w, n_out):
    _, k, n_in = w.shape
    kb = 256
    return pl.pallas_call(
        body,
        grid=(k // kb,),
        in_specs=[pl.BlockSpec((n_in, kb), lambda i: (0, i))],
        out_specs=pl.BlockSpec((kb, n_out), lambda i: (i, 0)),
        out_shape=jax.ShapeDtypeStruct((k, n_out), BF16),
        compiler_params=pltpu.CompilerParams(dimension_semantics=("arbitrary",),
                                             vmem_limit_bytes=VMEM_LIMIT_BYTES),
        name="weight_prep",
    )(w[0].T)


def kernel(x_prompt, x_sample, state_gla, state_ret, state_delta, state_conv, c_prompt, c_sample, w_ada, b_ada, g_pre, g_post, w_in_ab, w_gla_gate_up, b_gla_gate, g_norm_gla, g_norm_ret, w_out_ab, w_in_c, conv_c, a_log_c, dt_bias_c, g_norm_c, w_out_c):
    nbp, t_p, _ = x_prompt.shape
    nbs, t_s, _ = x_sample.shape
    assert w_ada.shape[0] == 2 and w_in_ab.shape[0] == 1

    mod = _ada_call(jnp.concatenate([c_prompt, c_sample], axis=0), w_ada, b_ada)
    mod_p = mod[:, :nbp, None, :]
    mod_s = mod[:, nbp:, None, :]

    w_ab = _weight_prep_call(_prep_ab_kernel, w_in_ab, N_AB)
    wgu = jnp.pad(w_gla_gate_up[0], ((0, GLA_RANK_PAD - 16), (0, 0))).astype(BF16)
    weights_ab = (_row(g_pre[0]), w_ab, wgu, _row(b_gla_gate[0]), _row(g_norm_gla[0]),
                  _row(g_norm_ret[0]), w_out_ab[0].astype(BF16), _row(g_post[0]))

    w_c = _weight_prep_call(_prep_c_kernel, w_in_c, N_C)
    head_lanes = ((0, 0), (GDN_HEADS, 128 - 2 * GDN_HEADS))
    weights_c = (_row(g_pre[1]), w_c, conv_c[0], jnp.pad(_row(a_log_c[0]), head_lanes),
                 jnp.pad(_row(dt_bias_c[0]), head_lanes), _row(g_norm_c[0]), w_out_c[0].astype(BF16),
                 _row(g_post[1]))

    cos_p, sin_p = _rope_tables(jnp.arange(t_p, dtype=jnp.int32))
    cos_s, sin_s = _rope_tables(PAST_LEN + jnp.arange(t_s, dtype=jnp.int32))
    bb_s = TILE // t_s
    bb_ab = 2 * bb_s
    cos_s, sin_s = jnp.tile(cos_s, (bb_ab, 1)), jnp.tile(sin_s, (bb_ab, 1))

    tt_p, bb_p = 256, 2
    xp, gla_p, ret_p = _ab_call(x_prompt, mod_p[0], cos_p, sin_p, None, weights_ab, bb=2 * bb_p, tt=tt_p)
    y_prompt, delta_p, conv_p = _c_call(xp, mod_p[1], None, weights_c, bb=bb_p, tt=tt_p)

    xs, gla_s, ret_s = _ab_call(x_sample, mod_s[0], cos_s, sin_s,
                                (state_gla.reshape(state_gla.shape[1:]), state_ret.reshape(state_ret.shape[1:])),
                                weights_ab, bb=bb_ab, tt=t_s)
    y_sample, delta_s, conv_s = _c_call(xs, mod_s[1],
                                        (state_delta.reshape(state_delta.shape[1:]),
                                         jnp.swapaxes(state_conv.reshape(state_conv.shape[1:]), 0, 1)),
                                        weights_c, bb=bb_s, tt=t_s)
    return (y_prompt, y_sample, gla_p[None], ret_p[None], delta_p[None], conv_p[None],
            gla_s[None], ret_s[None], delta_s[None], jnp.swapaxes(conv_s, 0, 1)[None])
```
